```python
import math
import jax
import jax.numpy as jnp
from jax import lax
import numpy as np

D_MODEL = 4096
BATCH = 1
SEQ = 8192
DEPTH = 4

GRID_W = 64
CTX_LEN = 256
N_SUB = 3
FFN_RES = 0.5
D_FF = 5632
ADA_RANK = 256
RMS_EPS = 1e-6
MIX_HALF = D_MODEL // 2
MIX_OUT = 2 * MIX_HALF

RWKV_WIDTH = MIX_HALF
RWKV_HEAD = 64
RWKV_HEADS = RWKV_WIDTH // RWKV_HEAD
RWKV_DECAY_LORA = 96
RWKV_A_LORA = 96
RWKV_GATE_LORA = 256
RWKV_COLS = 3 * RWKV_WIDTH + RWKV_DECAY_LORA + RWKV_A_LORA + RWKV_GATE_LORA
RWKV_GN_EPS = 64e-5

NA_WIDTH = MIX_HALF
NA_HEAD_DIM = 128
NA_HEADS = NA_WIDTH // NA_HEAD_DIM
NA_KH = 8
NA_KW = 16
AB_COLS = RWKV_COLS + 3 * NA_WIDTH

ML_WIDTH = MIX_HALF
ML_HEADS = 4
ML_DV = ML_WIDTH // ML_HEADS
ML_DQK = ML_DV // 2
ML_CHUNK = 64
ML_COLS = 2 * ML_HEADS * ML_DQK + 2 * ML_WIDTH + 4 * ML_HEADS
ROPE_BASE = 10000.0

HY_WIDTH = MIX_HALF
HY_ORDER = 2
HY_EMB = 33
HY_FF = 64
HY_MIN_DECAY = abs(math.log(1e-2)) / 1.5
HY_MAX_DECAY = abs(math.log(1e-2)) / 0.3
CD_COLS = ML_COLS + (HY_ORDER + 1) * HY_WIDTH

N_EVEN = (DEPTH + 1) // 2
N_ODD = DEPTH // 2
F32 = jnp.float32

kernel_name = 'hybrid_rwkv7_natten_mlstm_hyena_dit_trunk'


def rms_norm(x, g):
    xf = x.astype(F32)
    y = xf * lax.rsqrt(jnp.mean(xf * xf, axis=-1, keepdims=True) + RMS_EPS)
    return (y * g.astype(F32)).astype(x.dtype)


def modulate(h, shift, scale):
    return h * (1 + scale[:, None, :]) + shift[:, None, :]


def ada_params(s, down, up, bias):
    m = (s @ down) @ up + bias
    return m.reshape(s.shape[0], N_SUB, 3, D_MODEL)


def ffn_half(x, mod, g_pre, g_post, w1, w3, w2):
    h = modulate(rms_norm(x, g_pre), mod[:, 0], mod[:, 1])
    y = (jax.nn.silu(h @ w1) * (h @ w3)) @ w2
    return x + FFN_RES * mod[:, 2][:, None, :] * rms_norm(y, g_post)


def shift_prev(y):
    return jnp.pad(y, ((0, 0), (1, 0), (0, 0)))[:, :-1]


def shift_next(y):
    return jnp.pad(y, ((0, 0), (0, 1), (0, 0)))[:, 1:]


def flip_if(t, rev, axis):
    return jnp.flip(t, axis=axis) if rev else t


def rwkv7_inputs(p, mu, w0, w2, a0, a2, g2, k_k, k_a):
    p = p.astype(F32)
    B, T, _ = p.shape
    p = p + mu[0] * (shift_prev(p) - p) + mu[1] * (shift_next(p) - p)
    W = RWKV_WIDTH
    cuts = (W, 2 * W, 3 * W, 3 * W + RWKV_DECAY_LORA, 3 * W + RWKV_DECAY_LORA + RWKV_A_LORA)
    r, k, v, wd, ad, gd = jnp.split(p, cuts, axis=-1)
    heads = lambda t: t.reshape(B, T, RWKV_HEADS, RWKV_HEAD)
    kk = heads(k * k_k)
    kk = kk / jnp.maximum(jnp.linalg.norm(kk, axis=-1, keepdims=True), 1e-12)
    g = jax.nn.sigmoid(gd) @ g2
    wl = jnp.tanh(wd)
    dirs = []
    for d in range(2):
        logw = -jax.nn.softplus(-(w0[d] + wl @ w2[d])) - 0.5
        decay = jnp.exp(-jnp.exp(logw))
        a = jax.nn.sigmoid(a0[d] + ad @ a2[d])
        kd = k * (1 + (a - 1) * k_a)
        dirs.append((heads(decay), heads(kd), heads(a)))
    return heads(r), heads(v), kk, g, dirs


def wkv7_scan(S0, r, w, k, v, a, b, reverse):
    def step(S, inp):
        r_t, w_t, k_t, v_t, a_t, b_t = inp
        sa = jnp.einsum('bhvk,bhk->bhv', S, a_t)
        S = S * w_t[:, :, None, :] + sa[..., None] * b_t[:, :, None, :] + v_t[..., None] * k_t[:, :, None, :]
        return S, jnp.einsum('bhvk,bhk->bhv', S, r_t)
    xs = tuple(jnp.moveaxis(t, 1, 0) for t in (r, w, k, v, a, b))
    S, ys = lax.scan(step, S0, xs, reverse=reverse)
    return S, jnp.moveaxis(ys, 0, 1)


def rwkv7_readout(y, r, v, dirs, g, r_k, gn):
    B, T, H, N = y.shape
    mean = jnp.mean(y, axis=-1, keepdims=True)
    var = jnp.mean(jnp.square(y - mean), axis=-1, keepdims=True)
    yn = ((y - mean) * lax.rsqrt(var + RWKV_GN_EPS)).reshape(B, T, H * N) * gn[0] + gn[1]
    rk = r_k.reshape(H, N)
    bonus = sum(jnp.sum(r * kd * rk, axis=-1, keepdims=True) * v for _, kd, _ in dirs)
    return (yn + bonus.reshape(B, T, H * N)) * g


def rwkv7_mix(p, pc, mu, w0, w2, a0, a2, g2, k_k, k_a, r_k, gn, need_ctx):
    prm = (mu, w0, w2, a0, a2, g2, k_k, k_a)
    r, v, kk, g, dirs = rwkv7_inputs(p, *prm)
    rc, vc, kkc, gc, dirs_c = rwkv7_inputs(pc, *prm)
    S0 = jnp.zeros((pc.shape[0], RWKV_HEADS, RWKV_HEAD, RWKV_HEAD), F32)
    ys, ycs = [], []
    for d in range(2):
        rev = d == 1
        dec_c, kd_c, a_c = dirs_c[d]
        S_c, yc = wkv7_scan(S0, rc, dec_c, kd_c, vc, -kkc, kkc * a_c, rev)
        dec, kd, a = dirs[d]
        _, y = wkv7_scan(S_c, r, dec, kd, v, -kk, kk * a, rev)
        ys.append(y)
        ycs.append(yc)
    out = rwkv7_readout(ys[0] + ys[1], r, v, dirs, g, r_k, gn).astype(p.dtype)
    out_c = rwkv7_readout(ycs[0] + ycs[1], rc, vc, dirs_c, gc, r_k, gn).astype(pc.dtype) if need_ctx else None
    return out, out_c


def na_latent(q, k, v, kc, vc, rpb):
    B, T, H, dh = q.shape
    rows = T // GRID_W
    kh = min(NA_KH, rows)
    kw = min(NA_KW, GRID_W)
    qg = (q * dh ** -0.5).reshape(B, rows, GRID_W, H, dh)
    kg = k.reshape(B, rows, GRID_W, H, dh)
    vg = v.reshape(B, rows, GRID_W, H, dh)
    col = jnp.arange(GRID_W)
    col_idx = jnp.clip(col - kw // 2, 0, GRID_W - kw)[:, None] + jnp.arange(kw)[None, :]
    dc = col_idx - col[:, None] + (NA_KW - 1)
    row_start = jnp.clip(jnp.arange(rows) - kh // 2, 0, rows - kh)

    def one_row(r):
        r0 = row_start[r]
        k_win = lax.dynamic_slice_in_dim(kg, r0, kh, axis=1)[:, :, col_idx]
        v_win = lax.dynamic_slice_in_dim(vg, r0, kh, axis=1)[:, :, col_idx]
        q_row = lax.dynamic_index_in_dim(qg, r, axis=1, keepdims=False)
        dr = r0 + jnp.arange(kh) - r + (NA_KH - 1)
        bias = rpb[:, dr[None, :, None], dc[:, None, :]]
        s_loc = jnp.einsum('bqhd,biqjhd->bhqij', q_row, k_win) + bias[None].astype(q.dtype)
        s_ctx = jnp.einsum('bqhd,bkhd->bhqk', q_row, kc)
        s = jnp.concatenate([s_loc.reshape(B, H, GRID_W, kh * kw), s_ctx], axis=-1)
        pr = jax.nn.softmax(s.astype(F32), axis=-1).astype(v.dtype)
        p_loc = pr[..., :kh * kw].reshape(B, H, GRID_W, kh, kw)
        return (jnp.einsum('bhqij,biqjhd->bqhd', p_loc, v_win)
                + jnp.einsum('bhqk,bkhd->bqhd', pr[..., kh * kw:], vc))

    out = lax.map(one_row, jnp.arange(rows))
    return jnp.moveaxis(out, 0, 1).reshape(B, T, H * dh)


def na_context(qc, kc, vc):
    B, C, H, dh = qc.shape
    s = jnp.einsum('bqhd,bkhd->bhqk', qc * dh ** -0.5, kc)
    pr = jax.nn.softmax(s.astype(F32), axis=-1).astype(vc.dtype)
    return jnp.einsum('bhqk,bkhd->bqhd', pr, vc).reshape(B, C, H * dh)


def mix_ab(xn, hn, w_in, mu, w0, w2, a0, a2, g2, k_k, k_a, r_k, gn, rpb, need_ctx):
    p = xn @ w_in
    pc = hn @ w_in
    ya, yac = rwkv7_mix(p[..., :RWKV_COLS], pc[..., :RWKV_COLS], mu, w0, w2, a0, a2, g2,
                        k_k, k_a, r_k, gn, need_ctx)
    heads = lambda t: t.reshape(t.shape[0], t.shape[1], NA_HEADS, NA_HEAD_DIM)
    q, k, v = jnp.split(p[..., RWKV_COLS:], 3, axis=-1)
    qc, kc, vc = jnp.split(pc[..., RWKV_COLS:], 3, axis=-1)
    yb = na_latent(heads(q), heads(k), heads(v), heads(kc), heads(vc), rpb)
    y = jnp.concatenate([ya, yb], axis=-1)
    yc = jnp.concatenate([yac, na_context(heads(qc), heads(kc), heads(vc))], axis=-1) if need_ctx else None
    return y, yc


def axial_rope(x):
    T, d = x.shape[1], x.shape[-1]
    half = d // 2
    nf = half // 2
    t = jnp.arange(T)
    row = (t // GRID_W).astype(F32)
    col = (t % GRID_W).astype(F32)
    inv = ROPE_BASE ** (-jnp.arange(nf, dtype=F32) / nf)
    ang = jnp.concatenate([row[:, None] * inv, col[:, None] * inv], axis=-1)[None, :, None, :]
    cos, sin = jnp.cos(ang), jnp.sin(ang)
    xf = x.astype(F32)
    x1, x2 = xf[..., :half], xf[..., half:]
    return jnp.concatenate([x1 * cos - x2 * sin, x2 * cos + x1 * sin], axis=-1).astype(x.dtype)


def mlstm_inputs(p, gate_b, rope):
    p = p.astype(F32)
    B, T, _ = p.shape
    QK = ML_HEADS * ML_DQK
    q, k, v, o, gt = jnp.split(p, (QK, 2 * QK, 2 * QK + ML_WIDTH, 2 * QK + 2 * ML_WIDTH), axis=-1)
    q = q.reshape(B, T, ML_HEADS, ML_DQK)
    k = k.reshape(B, T, ML_HEADS, ML_DQK)
    if rope:
        q = axial_rope(q)
        k = axial_rope(k)
    k = k * ML_DQK ** -0.5
    v = v.reshape(B, T, ML_HEADS, ML_DV)
    gt = gt.reshape(B, T, 2, 2, ML_HEADS) + gate_b.astype(F32)
    log_i = jnp.transpose(gt[:, :, :, 0], (2, 0, 3, 1))
    log_f = jax.nn.log_sigmoid(jnp.transpose(gt[:, :, :, 1], (2, 0, 3, 1)))
    bhtd = lambda t: jnp.transpose(t, (0, 2, 1, 3))
    return bhtd(q), bhtd(k), bhtd(v), o, log_i, log_f


def mlstm_chunkwise(q, k, v, log_i, log_f, state):
    B, H, T, _ = q.shape
    nc = T // ML_CHUNK
    chunks = lambda t: jnp.moveaxis(t.reshape(B, H, nc, ML_CHUNK, *t.shape[3:]), 2, 0)
    tri = jnp.tril(jnp.ones((ML_CHUNK, ML_CHUNK), dtype=bool))

    def step(carry, inp):
        C, n, m = carry
        qc, kc, vc, li, lf = inp
        b = jnp.cumsum(lf, axis=-1)
        dmat = jnp.where(tri, b[..., :, None] - b[..., None, :] + li[..., None, :], -jnp.inf)
        inter = b + m[..., None]
        m_t = jnp.maximum(jnp.max(dmat, axis=-1), inter)
        s = jnp.einsum('bhtd,bhsd->bhts', qc, kc) * jnp.exp(dmat - m_t[..., None])
        dec = jnp.exp(inter - m_t)
        num = jnp.einsum('bhts,bhsv->bhtv', s, vc) + dec[..., None] * jnp.einsum('bhtd,bhdv->bhtv', qc, C)
        den = jnp.sum(s, axis=-1) + dec * jnp.einsum('bhtd,bhd->bht', qc, n)
        h = num / jnp.maximum(jnp.abs(den), jnp.exp(-m_t))[..., None]
        b_last = b[..., -1]
        gl = b_last[..., None] - b + li
        m_new = jnp.maximum(b_last + m, jnp.max(gl, axis=-1))
        wk = jnp.exp(gl - m_new[..., None])
        sc = jnp.exp(b_last + m - m_new)
        C = sc[..., None, None] * C + jnp.einsum('bhs,bhsd,bhsv->bhdv', wk, kc, vc)
        n = sc[..., None] * n + jnp.einsum('bhs,bhsd->bhd', wk, kc)
        return (C, n, m_new), h

    state, h = lax.scan(step, state, tuple(chunks(t) for t in (q, k, v, log_i, log_f)))
    return state, jnp.moveaxis(h, 0, 2).reshape(B, H, T, -1)


def mlstm_readout(h, o, norm_w):
    B, H, T, dv = h.shape
    h = h * lax.rsqrt(jnp.mean(h * h, axis=-1, keepdims=True) + RMS_EPS)
    h = jnp.transpose(h, (0, 2, 1, 3)).reshape(B, T, H * dv)
    return h * norm_w.astype(F32) * jax.nn.sigmoid(o)


def mlstm_mix(p, pc, gate_b, norm_w, need_ctx):
    q, k, v, o, li, lf = mlstm_inputs(p, gate_b, True)
    qc, kc, vc, oc, lic, lfc = mlstm_inputs(pc, gate_b, False)
    B = pc.shape[0]
    zero = (jnp.zeros((B, ML_HEADS, ML_DQK, ML_DV), F32), jnp.zeros((B, ML_HEADS, ML_DQK), F32),
            jnp.zeros((B, ML_HEADS), F32))
    hs, hcs = [], []
    for d in range(2):
        rev = d == 1
        st, hc = mlstm_chunkwise(*(flip_if(t, rev, 2) for t in (qc, kc, vc, lic[d], lfc[d])), zero)
        _, h = mlstm_chunkwise(*(flip_if(t, rev, 2) for t in (q, k, v, li[d], lf[d])), st)
        hs.append(flip_if(h, rev, 2))
        hcs.append(flip_if(hc, rev, 2))
    out = mlstm_readout(hs[0] + hs[1], o, norm_w).astype(p.dtype)
    out_c = mlstm_readout(hcs[0] + hcs[1], oc, norm_w).astype(pc.dtype) if need_ctx else None
    return out, out_c


def hyena_filters(T, ff1, ff1_b, ff2, ff2_b, ff3, ff3_b, ff_out, freq, decay):
    t = jnp.linspace(0.0, 1.0, T, dtype=F32)[:, None]
    bands = (HY_EMB - 1) // 2
    f = jnp.linspace(1e-4, bands - 1, bands, dtype=F32)[None, :]
    w = (2 * math.pi / T) * jnp.arange(T, dtype=F32)[:, None]
    z = jnp.concatenate([t, jnp.cos(f * w), -jnp.sin(f * w)], axis=-1)
    fr = freq.astype(F32)
    h = jnp.sin(fr * (z @ ff1.astype(F32) + ff1_b))
    h = jnp.sin(fr * (h @ ff2.astype(F32) + ff2_b))
    h = jnp.sin(fr * (h @ ff3.astype(F32) + ff3_b))
    h = (h @ ff_out.astype(F32)) * jnp.exp(-t * jnp.abs(decay.astype(F32)))
    return h.reshape(T, HY_ORDER, 2, HY_WIDTH)


def two_sided_long_conv(z, h_fwd, h_bwd):
    T, C = h_fwd.shape
    f = jnp.concatenate([h_fwd, jnp.zeros((1, C), F32), h_bwd[:0:-1]], axis=0)
    zf = jnp.fft.rfft(z, n=2 * T, axis=1)
    ff = jnp.fft.rfft(f, axis=0)
    return jnp.fft.irfft(zf * ff[None], n=2 * T, axis=1)[:, :T]


def hyena_seq(p, short_w, short_b, filt, bias):
    p = p.astype(F32)
    sw = short_w.astype(F32)
    u = sw[0] * shift_prev(p) + sw[1] * p + sw[2] * shift_next(p) + short_b
    z, x1, x2 = jnp.split(u, 3, axis=-1)
    for n, gate in enumerate((x1, x2)):
        z = gate * (two_sided_long_conv(z, filt[:, n, 0], filt[:, n, 1]) + bias[n] * z)
    return z


def hyena_mix(p, pc, short_w, short_b, filt_params, bias, need_ctx):
    out = hyena_seq(p, short_w, short_b, hyena_filters(p.shape[1], *filt_params), bias).astype(p.dtype)
    out_c = (hyena_seq(pc, short_w, short_b, hyena_filters(pc.shape[1], *filt_params), bias).astype(pc.dtype)
             if need_ctx else None)
    return out, out_c


def mix_cd(xn, hn, w_in, gate_b, ml_norm, short_w, short_b, filt_params, hy_bias, need_ctx):
    p = xn @ w_in
    pc = hn @ w_in
    yc_m, ycc_m = mlstm_mix(p[..., :ML_COLS], pc[..., :ML_COLS], gate_b, ml_norm, need_ctx)
    yd, ydc = hyena_mix(p[..., ML_COLS:], pc[..., ML_COLS:], short_w, short_b, filt_params, hy_bias, need_ctx)
    y = jnp.concatenate([yc_m, yd], axis=-1)
    yc = jnp.concatenate([ycc_m, ydc], axis=-1) if need_ctx else None
    return y, yc


def setup_inputs(seed: int = 0) -> dict:
    key = jax.random.key(seed)
    ks = iter(jax.random.split(key, 64))
    nrm = lambda shape, s: jax.random.normal(next(ks), shape, F32) * s
    uni = lambda shape, lo, hi: jax.random.uniform(next(ks), shape, F32, lo, hi)
    D = D_MODEL
    return {
        'x': nrm((BATCH, SEQ, D), 1.0),
        'c': nrm((BATCH, D), 1.0),
        'ctx': nrm((BATCH, CTX_LEN, D), 1.0),
        'c_ctx': nrm((D,), 1.0),
        'ada_down': nrm((DEPTH, D, ADA_RANK), D ** -0.5),
        'ada_up': nrm((DEPTH, ADA_RANK, N_SUB * 3 * D), 0.3 * ADA_RANK ** -0.5),
        'ada_bias': nrm((DEPTH, N_SUB * 3 * D), 0.02),
        'norm_g': 1.0 + nrm((DEPTH, 2 * N_SUB, D), 0.02),
        'ffn_w1': nrm((DEPTH, 2, D, D_FF), D ** -0.5),
        'ffn_w3': nrm((DEPTH, 2, D, D_FF), D ** -0.5),
        'ffn_w2': nrm((DEPTH, 2, D_FF, D), D_FF ** -0.5),
        'ab_w_in': nrm((N_EVEN, D, AB_COLS), D ** -0.5),
        'ab_w_out': nrm((N_EVEN, MIX_OUT, D), MIX_OUT ** -0.5),
        'rwkv_mu': uni((N_EVEN, 2, RWKV_COLS), 0.0, 0.5),
        'rwkv_w0': uni((N_EVEN, 2, RWKV_WIDTH), -6.0, 0.5),
        'rwkv_w2': nrm((N_EVEN, 2, RWKV_DECAY_LORA, RWKV_WIDTH), 0.5 * RWKV_DECAY_LORA ** -0.5),
        'rwkv_a0': nrm((N_EVEN, 2, RWKV_WIDTH), 0.1),
        'rwkv_a2': nrm((N_EVEN, 2, RWKV_A_LORA, RWKV_WIDTH), 0.5 * RWKV_A_LORA ** -0.5),
        'rwkv_g2': nrm((N_EVEN, RWKV_GATE_LORA, RWKV_WIDTH), RWKV_GATE_LORA ** -0.5),
        'rwkv_kk': 0.85 + nrm((N_EVEN, RWKV_WIDTH), 0.02),
        'rwkv_ka': 1.0 + nrm((N_EVEN, RWKV_WIDTH), 0.02),
        'rwkv_rk': nrm((N_EVEN, RWKV_WIDTH), 0.1),
        'rwkv_gn': jnp.stack([1.0 + nrm((N_EVEN, RWKV_WIDTH), 0.02), nrm((N_EVEN, RWKV_WIDTH), 0.02)], axis=1),
        'na_rpb': nrm((N_EVEN, NA_HEADS, 2 * NA_KH - 1, 2 * NA_KW - 1), 0.1),
        'cd_w_in': nrm((N_ODD, D, CD_COLS), D ** -0.5),
        'cd_w_out': nrm((N_ODD, MIX_OUT, D), MIX_OUT ** -0.5),
        'ml_gate_b': jnp.stack([nrm((N_ODD, 2, ML_HEADS), 0.1), uni((N_ODD, 2, ML_HEADS), 3.0, 6.0)], axis=2),
        'ml_norm': 1.0 + nrm((N_ODD, ML_WIDTH), 0.02),
        'hy_short_w': nrm((N_ODD, 3, 3 * HY_WIDTH), 3 ** -0.5),
        'hy_short_b': nrm((N_ODD, 3 * HY_WIDTH), 0.02),
        'hy_ff1': nrm((N_ODD, HY_EMB, HY_FF), HY_EMB ** -0.5),
        'hy_ff1_b': nrm((N_ODD, HY_FF), 0.1),
        'hy_ff2': nrm((N_ODD, HY_FF, HY_FF), HY_FF ** -0.5),
        'hy_ff2_b': nrm((N_ODD, HY_FF), 0.1),
        'hy_ff3': nrm((N_ODD, HY_FF, HY_FF), HY_FF ** -0.5),
        'hy_ff3_b': nrm((N_ODD, HY_FF), 0.1),
        'hy_ff_out': nrm((N_ODD, HY_FF, HY_ORDER * 2 * HY_WIDTH), 0.01),
        'hy_freq': 1.0 + nrm((N_ODD, HY_FF), 0.05),
        'hy_decay': uni((N_ODD, HY_ORDER * 2 * HY_WIDTH), HY_MIN_DECAY, HY_MAX_DECAY),
        'hy_bias': nrm((N_ODD, HY_ORDER, HY_WIDTH), 0.1),
    }


def reference(x, c, ctx, c_ctx, ada_down, ada_up, ada_bias, norm_g, ffn_w1, ffn_w3, ffn_w2,
              ab_w_in, ab_w_out, rwkv_mu, rwkv_w0, rwkv_w2, rwkv_a0, rwkv_a2, rwkv_g2,
              rwkv_kk, rwkv_ka, rwkv_rk, rwkv_gn, na_rpb, cd_w_in, cd_w_out, ml_gate_b, ml_norm,
              hy_short_w, hy_short_b, hy_ff1, hy_ff1_b, hy_ff2, hy_ff2_b, hy_ff3, hy_ff3_b,
              hy_ff_out, hy_freq, hy_decay, hy_bias):
    s_lat = jax.nn.silu(c)
    s_ctx = jax.nn.silu(c_ctx)[None, :]
    h = ctx
    for l in range(DEPTH):
        need_ctx = l < DEPTH - 1
        i = l // 2
        m = ada_params(s_lat, ada_down[l], ada_up[l], ada_bias[l])
        mc = ada_params(s_ctx, ada_down[l], ada_up[l], ada_bias[l])
        g = norm_g[l]
        x = ffn_half(x, m[:, 0], g[0], g[1], ffn_w1[l, 0], ffn_w3[l, 0], ffn_w2[l, 0])
        h = ffn_half(h, mc[:, 0], g[0], g[1], ffn_w1[l, 0], ffn_w3[l, 0], ffn_w2[l, 0])
        xn = modulate(rms_norm(x, g[2]), m[:, 1, 0], m[:, 1, 1])
        hn = modulate(rms_norm(h, g[2]), mc[:, 1, 0], mc[:, 1, 1])
        if l % 2 == 0:
            y, yc = mix_ab(xn, hn, ab_w_in[i], rwkv_mu[i], rwkv_w0[i], rwkv_w2[i], rwkv_a0[i], rwkv_a2[i],
                           rwkv_g2[i], rwkv_kk[i], rwkv_ka[i], rwkv_rk[i], rwkv_gn[i], na_rpb[i], need_ctx)
            w_out = ab_w_out[i]
        else:
            filt_params = (hy_ff1[i], hy_ff1_b[i], hy_ff2[i], hy_ff2_b[i], hy_ff3[i], hy_ff3_b[i],
                           hy_ff_out[i], hy_freq[i], hy_decay[i])
            y, yc = mix_cd(xn, hn, cd_w_in[i], ml_gate_b[i], ml_norm[i], hy_short_w[i], hy_short_b[i],
                           filt_params, hy_bias[i], need_ctx)
            w_out = cd_w_out[i]
        x = x + m[:, 1, 2][:, None, :] * rms_norm(y @ w_out, g[3])
        x = ffn_half(x, m[:, 2], g[4], g[5], ffn_w1[l, 1], ffn_w3[l, 1], ffn_w2[l, 1])
        if need_ctx:
            h = h + mc[:, 1, 2][:, None, :] * rms_norm(yc @ w_out, g[3])
            h = ffn_half(h, mc[:, 2], g[4], g[5], ffn_w1[l, 1], ffn_w3[l, 1], ffn_w2[l, 1])
    return x
```

```python
import functools
import math

import jax
import jax.numpy as jnp
import numpy as np
from jax import lax
from jax.experimental import pallas as pl
from jax.experimental.pallas import tpu as pltpu

F32 = jnp.float32
BF16 = jnp.bfloat16

D_MODEL = 4096
DEPTH = 4
GRID_W = 64
CTX_LEN = 256
N_SUB = 3
FFN_RES = 0.5
D_FF = 5632
RMS_EPS = 1e-6
MIX_HALF = D_MODEL // 2

RWKV_HEAD = 64
RWKV_HEADS = MIX_HALF // RWKV_HEAD
RWKV_DECAY_LORA = 96
RWKV_A_LORA = 96
RWKV_GATE_LORA = 256
RWKV_COLS = 3 * MIX_HALF + RWKV_DECAY_LORA + RWKV_A_LORA + RWKV_GATE_LORA
RWKV_GN_EPS = 64e-5

NA_HEAD_DIM = 128
NA_HEADS = MIX_HALF // NA_HEAD_DIM
NA_KH = 8
NA_KW = 16

ML_HEADS = 4
ML_DV = MIX_HALF // ML_HEADS
ML_DQK = ML_DV // 2
ML_COLS = 2 * ML_HEADS * ML_DQK + 2 * MIX_HALF + 4 * ML_HEADS
ROPE_BASE = 10000.0

HY_ORDER = 2
HY_EMB = 33

LANE = 128
ROW_TILE = CTX_LEN
VMEM_LIMIT = 56 * 1024 * 1024
P_COLS = 12800
NEG = -1e30


def _params(sem):
    return pltpu.CompilerParams(dimension_semantics=sem, vmem_limit_bytes=VMEM_LIMIT)


def _pick(n, cands):
    for c in cands:
        if n % c == 0:
            return c
    return n


def _mm_kernel(a_ref, b_ref, o_ref):
    o_ref[...] = jnp.dot(a_ref[...], b_ref[...], preferred_element_type=F32).astype(o_ref.dtype)


def _split(x):
    hi = x.astype(BF16)
    lo = (x - hi.astype(F32)).astype(BF16)
    return hi, lo


def _mm3_kernel(a_ref, b_ref, o_ref):
    a_hi, a_lo = _split(a_ref[...])
    b_hi, b_lo = _split(b_ref[...])
    acc = jnp.dot(a_hi, b_hi, preferred_element_type=F32)
    acc += jnp.dot(a_hi, b_lo, preferred_element_type=F32)
    acc += jnp.dot(a_lo, b_hi, preferred_element_type=F32)
    o_ref[...] = acc


def _mm_call(kern, a, b, out_dtype, name):
    M, K = a.shape
    N = b.shape[1]
    tm = _pick(M, (1024, 768, 512, 256, 128, 64, 32, 16, 8))
    tn = _pick(N, (512, 256, 128))
    return pl.pallas_call(
        kern,
        grid=(M // tm, N // tn),
        in_specs=[pl.BlockSpec((tm, K), lambda i, j: (i, 0)),
                  pl.BlockSpec((K, tn), lambda i, j: (0, j))],
        out_specs=pl.BlockSpec((tm, tn), lambda i, j: (i, j)),
        out_shape=jax.ShapeDtypeStruct((M, N), out_dtype),
        compiler_params=_params(("parallel", "parallel")),
        name=name,
    )(a, b)


def mm(a, b, out_dtype=F32):
    return _mm_call(_mm_kernel, a.astype(BF16), b.astype(BF16), out_dtype, "mm")


def mm3(a, b):
    return _mm_call(_mm3_kernel, a.astype(F32), b.astype(F32), F32, "mm3")


def _swiglu_kernel(a_ref, w1_ref, w3_ref, o_ref):
    a = a_ref[...]
    h1 = jnp.dot(a, w1_ref[...], preferred_element_type=F32)
    h3 = jnp.dot(a, w3_ref[...], preferred_element_type=F32)
    o_ref[...] = (h1 * jax.nn.sigmoid(h1) * h3).astype(o_ref.dtype)


def swiglu(a, w1, w3):
    M, K = a.shape
    N = w1.shape[1]
    tm = _pick(M, (768, 512, 256))
    tn = _pick(N, (512, 256, 128))
    wspec = pl.BlockSpec((K, tn), lambda i, j: (0, j))
    return pl.pallas_call(
        _swiglu_kernel,
        grid=(M // tm, N // tn),
        in_specs=[pl.BlockSpec((tm, K), lambda i, j: (i, 0)), wspec, wspec],
        out_specs=pl.BlockSpec((tm, tn), lambda i, j: (i, j)),
        out_shape=jax.ShapeDtypeStruct((M, N), BF16),
        compiler_params=_params(("parallel", "parallel")),
        name="swiglu",
    )(a, w1, w3)


def _group_spec(d):
    return pl.BlockSpec((1, 1, d), lambda i: (jnp.minimum(i, 1), 0, 0))


def _norm_mod_kernel(x_ref, g_ref, sh_ref, sc_ref, o_ref):
    x = x_ref[...]
    y = x * lax.rsqrt(jnp.mean(x * x, axis=-1, keepdims=True) + RMS_EPS) * g_ref[...]
    o_ref[...] = (y * (1 + sc_ref[0]) + sh_ref[0]).astype(o_ref.dtype)


def norm_mod(x, g, shift, scale):
    R, D = x.shape
    row = pl.BlockSpec((ROW_TILE, D), lambda i: (i, 0))
    return pl.pallas_call(
        _norm_mod_kernel,
        grid=(R // ROW_TILE,),
        in_specs=[row, pl.BlockSpec((1, D), lambda i: (0, 0)), _group_spec(D), _group_spec(D)],
        out_specs=row,
        out_shape=jax.ShapeDtypeStruct((R, D), BF16),
        compiler_params=_params(("parallel",)),
        name="norm_mod",
    )(x, g.reshape(1, D), shift.reshape(2, 1, D), scale.reshape(2, 1, D))


def _post_res_kernel(x_ref, y_ref, g_ref, gate_ref, o_ref, *, coef):
    y = y_ref[...]
    yn = y * lax.rsqrt(jnp.mean(y * y, axis=-1, keepdims=True) + RMS_EPS) * g_ref[...]
    o_ref[...] = x_ref[...] + coef * gate_ref[0] * yn


def post_res(x, y, g, gate, coef):
    R, D = x.shape
    row = pl.BlockSpec((ROW_TILE, D), lambda i: (i, 0))
    return pl.pallas_call(
        functools.partial(_post_res_kernel, coef=coef),
        grid=(R // ROW_TILE,),
        in_specs=[row, row, pl.BlockSpec((1, D), lambda i: (0, 0)), _group_spec(D)],
        out_specs=row,
        out_shape=jax.ShapeDtypeStruct((R, D), F32),
        compiler_params=_params(("parallel",)),
        name="post_res",
    )(x, y, g.reshape(1, D), gate.reshape(2, 1, D))


WKV_TB = 32
WKV_VS = RWKV_HEAD // 2


def _wkv_kernel(a_ref, b_ref, k_ref, w_ref, r_ref, v_ref, y_ref, s_ref):
    @pl.when(pl.program_id(0) == 0)
    def _():
        s_ref[...] = jnp.zeros_like(s_ref)

    def step(t, carry):
        parts = [None] * 4
        for k in range(RWKV_HEAD):
            term = s_ref[k] * a_ref[t, pl.ds(k, 1), :]
            parts[k % 4] = term if parts[k % 4] is None else parts[k % 4] + term
        sa = (parts[0] + parts[1]) + (parts[2] + parts[3])
        vv = v_ref[t]
        parts = [None] * 4
        for k in range(RWKV_HEAD):
            s_new = (s_ref[k] * w_ref[t, pl.ds(k, 1), :] + sa * b_ref[t, pl.ds(k, 1), :]
                     + vv * k_ref[t, pl.ds(k, 1), :])
            s_ref[k] = s_new
            term = s_new * r_ref[t, pl.ds(k, 1), :]
            parts[k % 4] = term if parts[k % 4] is None else parts[k % 4] + term
        y_ref[t] = (parts[0] + parts[1]) + (parts[2] + parts[3])
        return carry

    lax.fori_loop(0, WKV_TB, step, 0)


def wkv_scan(a, b, k, w, r, v):
    S = a.shape[0]
    kspec = pl.BlockSpec((WKV_TB, RWKV_HEAD, LANE), lambda i: (i, 0, 0))
    vspec = pl.BlockSpec((WKV_TB, WKV_VS, LANE), lambda i: (i, 0, 0))
    return pl.pallas_call(
        _wkv_kernel,
        grid=(S // WKV_TB,),
        in_specs=[kspec] * 5 + [vspec],
        out_specs=vspec,
        out_shape=jax.ShapeDtypeStruct((S, WKV_VS, LANE), F32),
        scratch_shapes=[pltpu.VMEM((RWKV_HEAD, WKV_VS, LANE), F32)],
        compiler_params=_params(("arbitrary",)),
        name="wkv_scan",
    )(a, b, k, w, r, v)


def _seq_rev(t):
    return jnp.concatenate([t[:CTX_LEN][::-1], t[CTX_LEN:][::-1]], axis=0)


def _wkv_rows(x0, x1):
    R = x0.shape[0]
    t = jnp.stack([x0, _seq_rev(x1)], axis=1).reshape(R, 2, RWKV_HEADS, RWKV_HEAD)
    t = jnp.transpose(t, (0, 3, 1, 2)).reshape(R, RWKV_HEAD, 2 * RWKV_HEADS)
    return jnp.concatenate([t, t], axis=-1)


def _wkv_vals(v):
    R = v.shape[0]
    t = jnp.stack([v, _seq_rev(v)], axis=1).reshape(R, 2, RWKV_HEADS, 2, WKV_VS)
    return jnp.transpose(t, (0, 4, 3, 1, 2)).reshape(R, WKV_VS, LANE)


def _wkv_unpack(y):
    R = y.shape[0]
    t = y.reshape(R, WKV_VS, 2, 2, RWKV_HEADS)
    t = jnp.transpose(t, (3, 0, 4, 2, 1)).reshape(2, R, MIX_HALF)
    return t[0] + _seq_rev(t[1])


def _shift_prev(y):
    return jnp.pad(y, ((1, 0), (0, 0)))[:-1]


def _shift_next(y):
    return jnp.pad(y, ((0, 1), (0, 0)))[1:]


def _per_seq(fn, t):
    return jnp.concatenate([fn(t[:CTX_LEN]), fn(t[CTX_LEN:])], axis=0)


AB_WD = 3 * MIX_HALF
AB_AD = AB_WD + LANE
AB_GD = AB_AD + LANE
AB_NA = AB_GD + RWKV_GATE_LORA


def _pack_cols(w, cuts, widths):
    pieces = jnp.split(w, cuts, axis=-1)
    pad = [(0, 0)] * (w.ndim - 1)
    return jnp.concatenate([jnp.pad(p, pad + [(0, wd - p.shape[-1])]) for p, wd in zip(pieces, widths)],
                           axis=-1)


_AB_CUTS = (AB_WD, AB_WD + RWKV_DECAY_LORA, AB_WD + RWKV_DECAY_LORA + RWKV_A_LORA, RWKV_COLS)
_AB_WIDTHS = (AB_WD, LANE, LANE, RWKV_GATE_LORA, 3 * MIX_HALF)


def rwkv_mix(p, mu, w0, w2, a0, a2, g2, k_k, k_a, r_k, gn):
    R = p.shape[0]
    mu_p = _pack_cols(mu, _AB_CUTS[:3], _AB_WIDTHS[:4])
    p = p + mu_p[0] * (_per_seq(_shift_prev, p) - p) + mu_p[1] * (_per_seq(_shift_next, p) - p)
    W = MIX_HALF
    r, k, v = p[:, :W], p[:, W:2 * W], p[:, 2 * W:3 * W]
    wd, ad, gd = p[:, AB_WD:AB_AD], p[:, AB_AD:AB_GD], p[:, AB_GD:AB_NA]
    heads = lambda t: t.reshape(R, RWKV_HEADS, RWKV_HEAD)
    kk = heads(k * k_k)
    kk = (kk / jnp.maximum(jnp.linalg.norm(kk, axis=-1, keepdims=True), 1e-12)).reshape(R, W)
    g = mm(jax.nn.sigmoid(gd), g2)
    lora_pad = lambda w: jnp.pad(jnp.concatenate([w[0], w[1]], axis=-1), ((0, LANE - w.shape[1]), (0, 0)))
    wlo = mm(jnp.tanh(wd), lora_pad(w2))
    alo = mm(ad, lora_pad(a2))
    decay, kd, ag = [], [], []
    for d in range(2):
        logw = -jax.nn.softplus(-(w0[d] + wlo[:, d * W:(d + 1) * W])) - 0.5
        decay.append(jnp.exp(-jnp.exp(logw)))
        a = jax.nn.sigmoid(a0[d] + alo[:, d * W:(d + 1) * W])
        ag.append(a)
        kd.append(k * (1 + (a - 1) * k_a))
    y = wkv_scan(_wkv_rows(-kk, -kk), _wkv_rows(kk * ag[0], kk * ag[1]), _wkv_rows(kd[0], kd[1]),
                 _wkv_rows(decay[0], decay[1]), _wkv_rows(r, r), _wkv_vals(v))
    y = heads(_wkv_unpack(y))
    mean = jnp.mean(y, axis=-1, keepdims=True)
    var = jnp.mean(jnp.square(y - mean), axis=-1, keepdims=True)
    yn = ((y - mean) * lax.rsqrt(var + RWKV_GN_EPS)).reshape(R, W) * gn[0] + gn[1]
    bonus = sum(jnp.sum(heads(r * kd_d * r_k), axis=-1, keepdims=True) for kd_d in kd) * heads(v)
    return (yn + bonus.reshape(R, W)) * g


NA_WIN = NA_KH * GRID_W


def _na_kernel(q_ref, k_ref, v_ref, bias_ref, o_ref, *, rows):
    r = pl.program_id(1)
    r0 = jnp.clip(r - NA_KH // 2, 0, rows - NA_KH)
    start = pl.multiple_of(CTX_LEN + r0 * GRID_W, GRID_W)
    q = (q_ref[...] * NA_HEAD_DIM ** -0.5).astype(BF16)
    dn = (((1,), (1,)), ((), ()))
    k_win = k_ref[pl.ds(start, NA_WIN), :].astype(BF16)
    s_loc = lax.dot_general(q, k_win, dn, preferred_element_type=F32) + bias_ref[0, 0]
    s_ctx = lax.dot_general(q, k_ref[pl.ds(0, CTX_LEN), :].astype(BF16), dn, preferred_element_type=F32)
    m = jnp.maximum(jnp.max(s_loc, axis=-1, keepdims=True), jnp.max(s_ctx, axis=-1, keepdims=True))
    p_loc = jnp.exp(s_loc - m)
    p_ctx = jnp.exp(s_ctx - m)
    den = jnp.sum(p_loc, axis=-1, keepdims=True) + jnp.sum(p_ctx, axis=-1, keepdims=True)
    num = jnp.dot(p_loc.astype(BF16), v_ref[pl.ds(start, NA_WIN), :].astype(BF16), preferred_element_type=F32)
    num += jnp.dot(p_ctx.astype(BF16), v_ref[pl.ds(0, CTX_LEN), :].astype(BF16), preferred_element_type=F32)
    o_ref[...] = num / den


def _na_ctx_kernel(q_ref, k_ref, v_ref, o_ref):
    q = (q_ref[...] * NA_HEAD_DIM ** -0.5).astype(BF16)
    s = lax.dot_general(q, k_ref[...].astype(BF16), (((1,), (1,)), ((), ())), preferred_element_type=F32)
    p = jnp.exp(s - jnp.max(s, axis=-1, keepdims=True))
    num = jnp.dot(p.astype(BF16), v_ref[...].astype(BF16), preferred_element_type=F32)
    o_ref[...] = num / jnp.sum(p, axis=-1, keepdims=True)


def _na_bias(rpb, rows):
    kh = min(NA_KH, rows)
    col = np.arange(GRID_W)
    c0 = np.clip(col - NA_KW // 2, 0, GRID_W - NA_KW)
    key = np.arange(GRID_W)
    inside = (key[None, :] >= c0[:, None]) & (key[None, :] < c0[:, None] + NA_KW)
    dc = np.clip(key[None, :] - col[:, None] + (NA_KW - 1), 0, 2 * NA_KW - 2)
    off = np.arange(NA_KH)
    dr = np.clip(np.arange(kh)[None, :] - off[:, None] + (NA_KH - 1), 0, 2 * NA_KH - 2)
    tab = rpb[:, dr[:, None, :, None], dc[None, :, None, :]]
    tab = jnp.where(jnp.asarray(inside)[None, None, :, None, :], tab, NEG)
    return tab.reshape(rpb.shape[0], NA_KH, GRID_W, kh * GRID_W)


def na_mix(p, rpb):
    R = p.shape[0]
    T = R - CTX_LEN
    rows = T // GRID_W
    qb, kb, vb = [(AB_NA + i * MIX_HALF) // NA_HEAD_DIM for i in range(3)]
    cb = CTX_LEN // GRID_W
    bias = _na_bias(rpb, rows)
    strip = lambda b: pl.BlockSpec((R, NA_HEAD_DIM), lambda h, r: (0, b + h))
    lat = pl.pallas_call(
        functools.partial(_na_kernel, rows=rows),
        grid=(NA_HEADS, rows),
        in_specs=[pl.BlockSpec((GRID_W, NA_HEAD_DIM), lambda h, r: (cb + r, qb + h)),
                  strip(kb), strip(vb),
                  pl.BlockSpec((1, 1, GRID_W, NA_WIN),
                               lambda h, r: (h, r - jnp.clip(r - NA_KH // 2, 0, rows - NA_KH), 0, 0))],
        out_specs=pl.BlockSpec((GRID_W, NA_HEAD_DIM), lambda h, r: (r, h)),
        out_shape=jax.ShapeDtypeStruct((T, MIX_HALF), F32),
        compiler_params=_params(("parallel", "arbitrary")),
        name="na_latent",
    )(p, p, p, bias)
    blk = lambda b: pl.BlockSpec((CTX_LEN, NA_HEAD_DIM), lambda h: (0, b + h))
    ctx = pl.pallas_call(
        _na_ctx_kernel,
        grid=(NA_HEADS,),
        in_specs=[blk(qb), blk(kb), blk(vb)],
        out_specs=pl.BlockSpec((CTX_LEN, NA_HEAD_DIM), lambda h: (0, h)),
        out_shape=jax.ShapeDtypeStruct((CTX_LEN, MIX_HALF), F32),
        compiler_params=_params(("parallel",)),
        name="na_context",
    )(p, p, p)
    return jnp.concatenate([ctx, lat], axis=0)


ML_CHUNK = CTX_LEN
CD_GT = 2 * ML_HEADS * ML_DQK + 2 * MIX_HALF
CD_HY = CD_GT + LANE


def _mlstm_kernel(q_ref, k_ref, v_ref, lir_ref, lfr_ref, lic_ref, lfc_ref, o_ref, c_ref, n_ref, m_ref):
    L = ML_CHUNK

    @pl.when(pl.program_id(2) == 0)
    def _():
        c_ref[...] = jnp.zeros_like(c_ref)
        n_ref[...] = jnp.zeros_like(n_ref)
        m_ref[...] = jnp.zeros_like(m_ref)

    sgn = 1 - 2 * pl.program_id(0)
    row = lax.broadcasted_iota(jnp.int32, (L, L), 0)
    col = lax.broadcasted_iota(jnp.int32, (L, L), 1)
    seen = (row - col) * sgn >= 0
    seen_t = (col - row) * sgn >= 0
    q = q_ref[...]
    k = k_ref[...]
    vb = v_ref[...].astype(BF16)
    li_r, lf_r, li_c, lf_c = lir_ref[0], lfr_ref[0], lic_ref[0], lfc_ref[0]
    b_c = jnp.sum(jnp.where(seen, lf_r, 0.0), axis=1, keepdims=True)
    b_r = jnp.sum(jnp.where(seen_t, lf_c, 0.0), axis=0, keepdims=True)
    m_prev = m_ref[...]
    dmat = jnp.where(seen, b_c - b_r + li_r, NEG)
    inter = b_c + m_prev
    m_t = jnp.maximum(jnp.max(dmat, axis=1, keepdims=True), inter)
    qb = q.astype(BF16)
    s = lax.dot_general(qb, k.astype(BF16), (((1,), (1,)), ((), ())), preferred_element_type=F32)
    s = s * jnp.exp(dmat - m_t)
    dec = jnp.exp(inter - m_t)
    num = jnp.dot(s.astype(BF16), vb, preferred_element_type=F32)
    num += dec * jnp.dot(qb, c_ref[...].astype(BF16), preferred_element_type=F32)
    den = jnp.sum(s, axis=1, keepdims=True) + dec * jnp.sum(q * n_ref[...], axis=1, keepdims=True)
    o_ref[0] = num / jnp.maximum(jnp.abs(den), jnp.exp(-m_t))
    total = jnp.sum(lf_r, axis=1, keepdims=True)
    gl = total - b_c + li_c
    m_new = jnp.maximum(total + m_prev, jnp.max(gl, axis=0, keepdims=True))
    kw = k * jnp.exp(gl - m_new)
    sc = jnp.exp(total + m_prev - m_new)
    c_ref[...] = sc * c_ref[...] + lax.dot_general(kw.astype(BF16), vb, (((0,), (0,)), ((), ())),
                                                   preferred_element_type=F32)
    n_ref[...] = sc * n_ref[...] + jnp.sum(kw, axis=0, keepdims=True)
    m_ref[...] = m_new


def _axial_rope(x):
    T, d = x.shape[0], x.shape[-1]
    half = d // 2
    nf = half // 2
    t = jnp.arange(T)
    row = (t // GRID_W).astype(F32)
    col = (t % GRID_W).astype(F32)
    inv = ROPE_BASE ** (-jnp.arange(nf, dtype=F32) / nf)
    ang = jnp.concatenate([row[:, None] * inv, col[:, None] * inv], axis=-1)[:, None, :]
    cos, sin = jnp.cos(ang), jnp.sin(ang)
    x1, x2 = x[..., :half], x[..., half:]
    return jnp.concatenate([x1 * cos - x2 * sin, x2 * cos + x1 * sin], axis=-1)


def mlstm_mix(p, gate_b, norm_w):
    R = p.shape[0]
    L = ML_CHUNK
    nc = R // L
    QK = ML_HEADS * ML_DQK

    def rope_lat(t):
        th = t.reshape(R, ML_HEADS, ML_DQK)
        return jnp.concatenate([th[:CTX_LEN], _axial_rope(th[CTX_LEN:])], axis=0).reshape(R, QK)

    q = rope_lat(p[:, :QK])
    k = rope_lat(p[:, QK:2 * QK]) * ML_DQK ** -0.5
    o = p[:, 2 * QK + MIX_HALF:CD_GT]
    gt = p[:, CD_GT:CD_GT + 4 * ML_HEADS].reshape(R, 2, 2, ML_HEADS) + gate_b
    log_i = jnp.transpose(gt[:, :, 0], (1, 2, 0))
    log_f = jax.nn.log_sigmoid(jnp.transpose(gt[:, :, 1], (1, 2, 0)))
    as_rows = lambda t: t.reshape(2 * ML_HEADS * nc, 1, L)
    as_cols = lambda t: t.reshape(2 * ML_HEADS * nc, L, 1)

    def chunk(d, c):
        return jnp.where(d == 0, c, jnp.where(c == 0, 0, nc - c))

    gidx = lambda d, h, c: ((d * ML_HEADS + h) * nc + chunk(d, c), 0, 0)
    vb = 2 * QK // ML_DV
    h_dir = pl.pallas_call(
        _mlstm_kernel,
        grid=(2, ML_HEADS, nc),
        in_specs=[pl.BlockSpec((L, ML_DQK), lambda d, h, c: (chunk(d, c), h)),
                  pl.BlockSpec((L, ML_DQK), lambda d, h, c: (chunk(d, c), h)),
                  pl.BlockSpec((L, ML_DV), lambda d, h, c: (chunk(d, c), vb + h)),
                  pl.BlockSpec((1, 1, L), gidx), pl.BlockSpec((1, 1, L), gidx),
                  pl.BlockSpec((1, L, 1), gidx), pl.BlockSpec((1, L, 1), gidx)],
        out_specs=pl.BlockSpec((1, L, ML_DV), lambda d, h, c: (d, chunk(d, c), h)),
        out_shape=jax.ShapeDtypeStruct((2, R, MIX_HALF), F32),
        scratch_shapes=[pltpu.VMEM((ML_DQK, ML_DV), F32), pltpu.VMEM((1, ML_DQK), F32),
                        pltpu.VMEM((1, 1), F32)],
        compiler_params=_params(("parallel", "parallel", "arbitrary")),
        name="mlstm",
    )(q, k, p, as_rows(log_i), as_rows(log_f), as_cols(log_i), as_cols(log_f))
    h = (h_dir[0] + h_dir[1]).reshape(R, ML_HEADS, ML_DV)
    h = h * lax.rsqrt(jnp.mean(h * h, axis=-1, keepdims=True) + RMS_EPS)
    return h.reshape(R, MIX_HALF) * norm_w * jax.nn.sigmoid(o)


def _pad_to(x, axis, mult):
    n = -x.shape[axis] % mult
    if n == 0:
        return x
    pad = [(0, 0)] * x.ndim
    pad[axis] = (0, n)
    return jnp.pad(x, pad)


def mm3p(a, b):
    M, N = a.shape[0], b.shape[1]
    a = _pad_to(_pad_to(a, 1, LANE), 0, 8)
    b = _pad_to(_pad_to(b, 0, LANE), 1, LANE)
    return mm3(a, b)[:M, :N]


def hyena_filters(T, ff1, ff1_b, ff2, ff2_b, ff3, ff3_b, ff_out, freq, decay):
    t = jnp.linspace(0.0, 1.0, T, dtype=F32)[:, None]
    bands = (HY_EMB - 1) // 2
    f = jnp.linspace(1e-4, bands - 1, bands, dtype=F32)[None, :]
    w = (2 * math.pi / T) * jnp.arange(T, dtype=F32)[:, None]
    z = jnp.concatenate([t, jnp.cos(f * w), -jnp.sin(f * w)], axis=-1)
    h = jnp.sin(freq * (mm3p(z, ff1) + ff1_b))
    h = jnp.sin(freq * (mm3p(h, ff2) + ff2_b))
    h = jnp.sin(freq * (mm3p(h, ff3) + ff3_b))
    h = mm3p(h, ff_out) * jnp.exp(-t * jnp.abs(decay))
    return h.reshape(T, HY_ORDER, 2, MIX_HALF)


FFT_N2 = LANE


@functools.lru_cache(maxsize=None)
def _dft_tables(n1):
    n2 = FFT_N2
    n = n1 * n2
    i1, i2 = np.arange(n1), np.arange(n2)
    a1 = 2 * np.pi * np.outer(i1, i1) / n1
    a2 = 2 * np.pi * np.outer(i2, i2) / n2
    c1, s1 = np.cos(a1), -np.sin(a1)
    c2, s2 = np.cos(a2), -np.sin(a2)
    atw = 2 * np.pi * np.outer(i2, i1) / n
    f32 = lambda x: np.asarray(x, np.float32)
    return dict(
        wa=f32(np.concatenate([c1, s1], axis=1)),
        wb=f32(np.block([[c2, s2], [-s2, c2]])),
        wb_inv=f32(np.block([[c2, -s2], [s2, c2]])),
        wa_inv=f32(np.concatenate([c1, s1], axis=0)[:, :n1 // 2] / n),
        twr=f32(np.cos(atw)), twi=f32(-np.sin(atw)))


def _fft_fwd(x_t, n1):
    C = x_t.shape[0]
    n2 = FFT_N2
    tb = _dft_tables(n1)
    xa = jnp.transpose(x_t.reshape(C, n1, n2), (0, 2, 1)).reshape(C * n2, n1)
    a = mm3p(xa, tb["wa"]).reshape(C, n2, 2, n1)
    ar, ai = a[:, :, 0], a[:, :, 1]
    br = ar * tb["twr"] - ai * tb["twi"]
    bi = ar * tb["twi"] + ai * tb["twr"]
    b = jnp.concatenate([jnp.transpose(br, (0, 2, 1)), jnp.transpose(bi, (0, 2, 1))], axis=-1)
    return mm3(b.reshape(C * n1, 2 * n2), tb["wb"])


def _fft_inv(y, C, n1):
    n2 = FFT_N2
    tb = _dft_tables(n1)
    g = mm3(y, tb["wb_inv"]).reshape(C, n1, 2, n2)
    gr, gi = g[:, :, 0], g[:, :, 1]
    twr, twi = tb["twr"].T, tb["twi"].T
    hr = gr * twr + gi * twi
    hi = gi * twr - gr * twi
    h = jnp.concatenate([jnp.transpose(hr, (0, 2, 1)), jnp.transpose(hi, (0, 2, 1))], axis=-1)
    out = mm3p(h.reshape(C * n2, 2 * n1), tb["wa_inv"])
    return jnp.transpose(out.reshape(C, n2, n1 // 2), (0, 2, 1)).reshape(C, n1 // 2 * n2)


def _long_conv(z, h_fwd, h_bwd):
    T, C = z.shape
    n1 = 2 * T // FFT_N2
    f = jnp.concatenate([h_fwd, jnp.zeros((1, C), F32), h_bwd[:0:-1]], axis=0)
    zs = _fft_fwd(jnp.pad(z.T, ((0, 0), (0, T))), n1).reshape(C * n1, 2, FFT_N2)
    fs = _fft_fwd(f.T, n1).reshape(C * n1, 2, FFT_N2)
    yr = zs[:, 0] * fs[:, 0] - zs[:, 1] * fs[:, 1]
    yi = zs[:, 0] * fs[:, 1] + zs[:, 1] * fs[:, 0]
    return _fft_inv(jnp.concatenate([yr, yi], axis=-1), C, n1).T


def hyena_seq(p, short_w, short_b, filt, bias):
    u = short_w[0] * _shift_prev(p) + short_w[1] * p + short_w[2] * _shift_next(p) + short_b
    z, x1, x2 = jnp.split(u, 3, axis=-1)
    for n, gate in enumerate((x1, x2)):
        z = gate * (_long_conv(z, filt[:, n, 0], filt[:, n, 1]) + bias[n] * z)
    return z


def hyena_mix(p, short_w, short_b, filt_params, bias):
    T = p.shape[0] - CTX_LEN
    out_c = hyena_seq(p[:CTX_LEN], short_w, short_b, hyena_filters(CTX_LEN, *filt_params), bias)
    out = hyena_seq(p[CTX_LEN:], short_w, short_b, hyena_filters(T, *filt_params), bias)
    return jnp.concatenate([out_c, out], axis=0)


def ffn_half(x, mod, g_pre, g_post, w1, w3, w2):
    h = norm_mod(x, g_pre, mod[:, 0], mod[:, 1])
    a = swiglu(h, w1.astype(BF16), w3.astype(BF16))
    return post_res(x, mm(a, w2), g_post, mod[:, 2], FFN_RES)


def kernel(x, c, ctx, c_ctx, ada_down, ada_up, ada_bias, norm_g, ffn_w1, ffn_w3, ffn_w2, ab_w_in, ab_w_out, rwkv_mu, rwkv_w0, rwkv_w2, rwkv_a0, rwkv_a2, rwkv_g2, rwkv_kk, rwkv_ka, rwkv_rk, rwkv_gn, na_rpb, cd_w_in, cd_w_out, ml_gate_b, ml_norm, hy_short_w, hy_short_b, hy_ff1, hy_ff1_b, hy_ff2, hy_ff2_b, hy_ff3, hy_ff3_b, hy_ff_out, hy_freq, hy_decay, hy_bias):
    assert x.shape[0] == 1 and ctx.shape[1] == CTX_LEN
    X = jnp.concatenate([ctx[0], x[0]], axis=0)
    s2 = _pad_to(jnp.stack([jax.nn.silu(c_ctx), jax.nn.silu(c[0])]), 0, 16)
    for l in range(DEPTH):
        i = l // 2
        mod = (mm(mm(s2, ada_down[l]), ada_up[l])[:2] + ada_bias[l]).reshape(2, N_SUB, 3, D_MODEL)
        g = norm_g[l]
        X = ffn_half(X, mod[:, 0], g[0], g[1], ffn_w1[l, 0], ffn_w3[l, 0], ffn_w2[l, 0])
        xn = norm_mod(X, g[2], mod[:, 1, 0], mod[:, 1, 1])
        if l % 2 == 0:
            p = mm(xn, _pack_cols(ab_w_in[i], _AB_CUTS, _AB_WIDTHS))
            ya = rwkv_mix(p[:, :AB_NA], rwkv_mu[i], rwkv_w0[i], rwkv_w2[i], rwkv_a0[i], rwkv_a2[i],
                          rwkv_g2[i], rwkv_kk[i], rwkv_ka[i], rwkv_rk[i], rwkv_gn[i])
            y = jnp.concatenate([ya, na_mix(p, na_rpb[i])], axis=-1)
            w_out = ab_w_out[i]
        else:
            w_in = _pack_cols(cd_w_in[i], (CD_GT, ML_COLS), (CD_GT, LANE, P_COLS - CD_HY))
            p = mm(xn, w_in)
            filt_params = (hy_ff1[i], hy_ff1_b[i], hy_ff2[i], hy_ff2_b[i], hy_ff3[i], hy_ff3_b[i],
                           hy_ff_out[i], hy_freq[i], hy_decay[i])
            yd = hyena_mix(p[:, CD_HY:CD_HY + 3 * MIX_HALF], hy_short_w[i], hy_short_b[i], filt_params,
                           hy_bias[i])
            y = jnp.concatenate([mlstm_mix(p, ml_gate_b[i], ml_norm[i]), yd], axis=-1)
            w_out = cd_w_out[i]
        X = post_res(X, mm(y, w_out), g[3], mod[:, 1, 2], 1.0)
        X = ffn_half(X, mod[:, 2], g[4], g[5], ffn_w1[l, 1], ffn_w3[l, 1], ffn_w2[l, 1])
    return X[CTX_LEN:][None]
```

```python
import functools
import math

import jax
import jax.numpy as jnp
import numpy as np
from jax import lax
from jax.experimental import pallas as pl
from jax.experimental.pallas import tpu as pltpu

F32 = jnp.float32
BF16 = jnp.bfloat16

D_MODEL = 4096
DEPTH = 4
GRID_W = 64
CTX_LEN = 256
N_SUB = 3
FFN_RES = 0.5
D_FF = 5632
RMS_EPS = 1e-6
MIX_HALF = D_MODEL // 2

RWKV_HEAD = 64
RWKV_HEADS = MIX_HALF // RWKV_HEAD
RWKV_DECAY_LORA = 96
RWKV_A_LORA = 96
RWKV_GATE_LORA = 256
RWKV_COLS = 3 * MIX_HALF + RWKV_DECAY_LORA + RWKV_A_LORA + RWKV_GATE_LORA
RWKV_GN_EPS = 64e-5

NA_HEAD_DIM = 128
NA_HEADS = MIX_HALF // NA_HEAD_DIM
NA_KH = 8
NA_KW = 16

ML_HEADS = 4
ML_DV = MIX_HALF // ML_HEADS
ML_DQK = ML_DV // 2
ML_COLS = 2 * ML_HEADS * ML_DQK + 2 * MIX_HALF + 4 * ML_HEADS
ROPE_BASE = 10000.0

HY_ORDER = 2
HY_EMB = 33

LANE = 128
ROW_TILE = CTX_LEN
VMEM_LIMIT = 56 * 1024 * 1024
P_COLS = 12800
NEG = -1e30


def _params(sem):
    return pltpu.CompilerParams(dimension_semantics=sem, vmem_limit_bytes=VMEM_LIMIT)


def _pick(n, cands):
    for c in cands:
        if n % c == 0:
            return c
    return n


def _mm_kernel(a_ref, b_ref, o_ref):
    o_ref[...] = jnp.dot(a_ref[...], b_ref[...], preferred_element_type=F32).astype(o_ref.dtype)


def _split(x):
    hi = x.astype(BF16)
    lo = (x - hi.astype(F32)).astype(BF16)
    return hi, lo


def _mm3_kernel(a_ref, b_ref, o_ref):
    a_hi, a_lo = _split(a_ref[...])
    b_hi, b_lo = _split(b_ref[...])
    acc = jnp.dot(a_hi, b_hi, preferred_element_type=F32)
    acc += jnp.dot(a_hi, b_lo, preferred_element_type=F32)
    acc += jnp.dot(a_lo, b_hi, preferred_element_type=F32)
    o_ref[...] = acc


def _mm_call(kern, a, b, out_dtype, name):
    M, K = a.shape
    N = b.shape[1]
    tm = _pick(M, (1024, 768, 512, 256, 128, 64, 32, 16, 8))
    tn = _pick(N, (512, 256, 128))
    return pl.pallas_call(
        kern,
        grid=(M // tm, N // tn),
        in_specs=[pl.BlockSpec((tm, K), lambda i, j: (i, 0)),
                  pl.BlockSpec((K, tn), lambda i, j: (0, j))],
        out_specs=pl.BlockSpec((tm, tn), lambda i, j: (i, j)),
        out_shape=jax.ShapeDtypeStruct((M, N), out_dtype),
        compiler_params=_params(("parallel", "parallel")),
        name=name,
    )(a, b)


def mm(a, b, out_dtype=F32):
    return _mm_call(_mm_kernel, a.astype(BF16), b.astype(BF16), out_dtype, "mm")


def mm3(a, b):
    return _mm_call(_mm3_kernel, a.astype(F32), b.astype(F32), F32, "mm3")


def _swiglu_kernel(a_ref, w1_ref, w3_ref, o_ref):
    a = a_ref[...]
    h1 = jnp.dot(a, w1_ref[...], preferred_element_type=F32)
    h3 = jnp.dot(a, w3_ref[...], preferred_element_type=F32)
    o_ref[...] = (h1 * jax.nn.sigmoid(h1) * h3).astype(o_ref.dtype)


def swiglu(a, w1, w3):
    M, K = a.shape
    N = w1.shape[1]
    tm = _pick(M, (768, 512, 256))
    tn = _pick(N, (512, 256, 128))
    wspec = pl.BlockSpec((K, tn), lambda i, j: (0, j))
    return pl.pallas_call(
        _swiglu_kernel,
        grid=(M // tm, N // tn),
        in_specs=[pl.BlockSpec((tm, K), lambda i, j: (i, 0)), wspec, wspec],
        out_specs=pl.BlockSpec((tm, tn), lambda i, j: (i, j)),
        out_shape=jax.ShapeDtypeStruct((M, N), BF16),
        compiler_params=_params(("parallel", "parallel")),
        name="swiglu",
    )(a, w1, w3)


def _group_spec(d):
    return pl.BlockSpec((1, 1, d), lambda i: (jnp.minimum(i, 1), 0, 0))


def _norm_mod_kernel(x_ref, g_ref, sh_ref, sc_ref, o_ref):
    x = x_ref[...]
    y = x * lax.rsqrt(jnp.mean(x * x, axis=-1, keepdims=True) + RMS_EPS) * g_ref[...]
    o_ref[...] = (y * (1 + sc_ref[0]) + sh_ref[0]).astype(o_ref.dtype)


def norm_mod(x, g, shift, scale):
    R, D = x.shape
    row = pl.BlockSpec((ROW_TILE, D), lambda i: (i, 0))
    return pl.pallas_call(
        _norm_mod_kernel,
        grid=(R // ROW_TILE,),
        in_specs=[row, pl.BlockSpec((1, D), lambda i: (0, 0)), _group_spec(D), _group_spec(D)],
        out_specs=row,
        out_shape=jax.ShapeDtypeStruct((R, D), BF16),
        compiler_params=_params(("parallel",)),
        name="norm_mod",
    )(x, g.reshape(1, D), shift.reshape(2, 1, D), scale.reshape(2, 1, D))


def _post_res_kernel(x_ref, y_ref, g_ref, gate_ref, o_ref, *, coef):
    y = y_ref[...]
    yn = y * lax.rsqrt(jnp.mean(y * y, axis=-1, keepdims=True) + RMS_EPS) * g_ref[...]
    o_ref[...] = x_ref[...] + coef * gate_ref[0] * yn


def post_res(x, y, g, gate, coef):
    R, D = x.shape
    row = pl.BlockSpec((ROW_TILE, D), lambda i: (i, 0))
    return pl.pallas_call(
        functools.partial(_post_res_kernel, coef=coef),
        grid=(R // ROW_TILE,),
        in_specs=[row, row, pl.BlockSpec((1, D), lambda i: (0, 0)), _group_spec(D)],
        out_specs=row,
        out_shape=jax.ShapeDtypeStruct((R, D), F32),
        compiler_params=_params(("parallel",)),
        name="post_res",
    )(x, y, g.reshape(1, D), gate.reshape(2, 1, D))


WKV_TB = 8
WKV_KG = MIX_HALF // LANE
WKV_VC = RWKV_HEAD // 8
PREP_TB = 128


def _kmajor(w):
    lead = w.shape[:-1]
    return jnp.swapaxes(w.reshape(*lead, RWKV_HEADS, RWKV_HEAD), -1, -2).reshape(*lead, MIX_HALF)


def _fold_lanes(s):
    s = s + pltpu.roll(s, 64, 1)
    return s + pltpu.roll(s, 32, 1)


def _rwkv_prep_kernel(p_ref, hp_ref, hn_ref, mu_ref, w2_ref, a2_ref, g2_ref, vec_ref,
                      a_out, b0_out, b1_out, k0_out, k1_out, w0_out, w1_out, r_out, v_out, g_out, bonus_out):
    W = MIX_HALF
    i = pl.program_id(0)
    cb = CTX_LEN // PREP_TB
    has_prev = jnp.logical_and(i != 0, i != cb).astype(F32)
    has_next = jnp.logical_and(i != cb - 1, i != pl.num_programs(0) - 1).astype(F32)
    row = lax.broadcasted_iota(jnp.int32, (PREP_TB, 1), 0)

    def shifted(c0, width):
        x = p_ref[:, c0:c0 + width]
        up = jnp.where(row == 0, hp_ref[7:8, c0:c0 + width] * has_prev, pltpu.roll(x, 1, 0))
        dn = jnp.where(row == PREP_TB - 1, hn_ref[0:1, c0:c0 + width] * has_next,
                       pltpu.roll(x, PREP_TB - 1, 0))
        return x + mu_ref[0:1, c0:c0 + width] * (up - x) + mu_ref[1:2, c0:c0 + width] * (dn - x)

    lo = shifted(AB_WD, AB_NA - AB_WD)
    wd, ad, gd = lo[:, :LANE], lo[:, LANE:2 * LANE], lo[:, 2 * LANE:]
    wlo = jnp.dot(jnp.tanh(wd).astype(BF16), w2_ref[...], preferred_element_type=F32)
    alo = jnp.dot(ad.astype(BF16), a2_ref[...], preferred_element_type=F32)
    g_out[...] = jnp.dot(jax.nn.sigmoid(gd).astype(BF16), g2_ref[...], preferred_element_type=F32)

    ss = None
    for g in range(WKV_KG):
        sl = slice(g * LANE, (g + 1) * LANE)
        kk = shifted(W + g * LANE, LANE) * vec_ref[4:5, sl]
        ss = kk * kk if ss is None else ss + kk * kk
    nrm = jnp.maximum(jnp.sqrt(_fold_lanes(ss)), 1e-12)
    bonus = None
    for g in range(WKV_KG):
        sl = slice(g * LANE, (g + 1) * LANE)
        r = shifted(g * LANE, LANE)
        k = shifted(W + g * LANE, LANE)
        r_out[:, sl] = r
        v_out[:, sl] = shifted(2 * W + g * LANE, LANE)
        kk = k * vec_ref[4:5, sl] / nrm
        a_out[:, sl] = -kk
        for d, (w_out, k_out, b_out) in enumerate(((w0_out, k0_out, b0_out), (w1_out, k1_out, b1_out))):
            dl = slice(d * W + g * LANE, d * W + (g + 1) * LANE)
            logw = -jax.nn.softplus(-(vec_ref[d:d + 1, sl] + wlo[:, dl])) - 0.5
            w_out[:, sl] = jnp.exp(-jnp.exp(logw))
            a = jax.nn.sigmoid(vec_ref[2 + d:3 + d, sl] + alo[:, dl])
            kd = k * (1 + (a - 1) * vec_ref[5:6, sl])
            k_out[:, sl] = kd
            b_out[:, sl] = kk * a
            term = r * kd * vec_ref[6:7, sl]
            bonus = term if bonus is None else bonus + term
    bonus_out[...] = _fold_lanes(bonus)


def rwkv_prep(p, mu, w2c, a2c, g2, vec):
    R = p.shape[0]
    W = MIX_HALF
    hb = PREP_TB // 8
    full = lambda a: pl.BlockSpec(a.shape, lambda i: (0,) * a.ndim)
    wide = pl.BlockSpec((PREP_TB, W), lambda i: (i, 0))
    return pl.pallas_call(
        _rwkv_prep_kernel,
        grid=(R // PREP_TB,),
        in_specs=[pl.BlockSpec((PREP_TB, AB_NA), lambda i: (i, 0)),
                  pl.BlockSpec((8, AB_NA), lambda i: (jnp.maximum(i * hb - 1, 0), 0)),
                  pl.BlockSpec((8, AB_NA), lambda i: (jnp.minimum((i + 1) * hb, R // 8 - 1), 0)),
                  full(mu), full(w2c), full(a2c), full(g2), full(vec)],
        out_specs=[wide] * 10 + [pl.BlockSpec((PREP_TB, LANE), lambda i: (i, 0))],
        out_shape=[jax.ShapeDtypeStruct((R, W), F32)] * 10 + [jax.ShapeDtypeStruct((R, LANE), F32)],
        compiler_params=_params(("parallel",)),
        name="rwkv_prep",
    )(p, p, p, mu, w2c, a2c, g2, vec)


def _wkv_kernel(a0_ref, b0_ref, k0_ref, w0_ref, r0_ref, v0_ref, a1_ref, b1_ref, k1_ref, w1_ref, r1_ref, v1_ref,
                y0_ref, y1_ref, s_ref):
    @pl.when(pl.program_id(0) == 0)
    def _():
        s_ref[...] = jnp.zeros_like(s_ref)

    dirs = ((a0_ref, b0_ref, k0_ref, w0_ref, r0_ref, v0_ref, y0_ref),
            (a1_ref, b1_ref, k1_ref, w1_ref, r1_ref, v1_ref, y1_ref))

    for t in range(WKV_TB):
        for d, (a_ref, b_ref, k_ref, w_ref, r_ref, v_ref, y_ref) in enumerate(dirs):
            tt = t if d == 0 else WKV_TB - 1 - t
            row = lambda ref, g: ref[tt:tt + 1, g * LANE:(g + 1) * LANE]
            acc = [None] * WKV_VC
            for g in range(WKV_KG):
                ar = row(a_ref, g)
                for c in range(WKV_VC):
                    term = s_ref[d, g, pl.ds(c * 8, 8), :] * ar
                    acc[c] = term if acc[c] is None else acc[c] + term
            sa = [_fold_lanes(x) for x in acc]
            vv = [v_ref[tt, pl.ds(c * 8, 8), :] for c in range(WKV_VC)]
            acc = [None] * WKV_VC
            for g in range(WKV_KG):
                wr, br, kr, rr = row(w_ref, g), row(b_ref, g), row(k_ref, g), row(r_ref, g)
                for c in range(WKV_VC):
                    s_new = s_ref[d, g, pl.ds(c * 8, 8), :] * wr + sa[c] * br + vv[c] * kr
                    s_ref[d, g, pl.ds(c * 8, 8), :] = s_new
                    term = s_new * rr
                    acc[c] = term if acc[c] is None else acc[c] + term
            for c in range(WKV_VC):
                y_ref[tt, pl.ds(c * 8, 8), :] = _fold_lanes(acc[c])


def wkv_scan(a, b0, b1, k0, k1, w0, w1, r, vp):
    R = a.shape[0]
    nb = R // WKV_TB
    cb = CTX_LEN // WKV_TB
    fwd = lambda i: i
    bwd = lambda i: jnp.where(i < cb, cb - 1 - i, nb + cb - 1 - i)
    kspec = lambda f: pl.BlockSpec((WKV_TB, MIX_HALF), lambda i: (f(i), 0))
    vspec = lambda f: pl.BlockSpec((WKV_TB, RWKV_HEAD, LANE), lambda i: (f(i), 0, 0))
    yshape = jax.ShapeDtypeStruct((R, RWKV_HEAD, LANE), F32)
    return pl.pallas_call(
        _wkv_kernel,
        grid=(nb,),
        in_specs=[kspec(fwd)] * 5 + [vspec(fwd)] + [kspec(bwd)] * 5 + [vspec(bwd)],
        out_specs=[vspec(fwd), vspec(bwd)],
        out_shape=[yshape, yshape],
        scratch_shapes=[pltpu.VMEM((2, WKV_KG, RWKV_HEAD, LANE), F32)],
        compiler_params=_params(("arbitrary",)),
        name="wkv_scan",
    )(a, b0, k0, w0, r, vp, a, b1, k1, w1, r, vp)


def _shift_prev(y):
    return jnp.pad(y, ((1, 0), (0, 0)))[:-1]


def _shift_next(y):
    return jnp.pad(y, ((0, 1), (0, 0)))[1:]


AB_WD = 3 * MIX_HALF
AB_AD = AB_WD + LANE
AB_GD = AB_AD + LANE
AB_NA = AB_GD + RWKV_GATE_LORA


def _pack_cols(w, cuts, widths):
    pieces = jnp.split(w, cuts, axis=-1)
    pad = [(0, 0)] * (w.ndim - 1)
    return jnp.concatenate([jnp.pad(p, pad + [(0, wd - p.shape[-1])]) for p, wd in zip(pieces, widths)],
                           axis=-1)


_AB_CUTS = (AB_WD, AB_WD + RWKV_DECAY_LORA, AB_WD + RWKV_DECAY_LORA + RWKV_A_LORA, RWKV_COLS)
_AB_WIDTHS = (AB_WD, LANE, LANE, RWKV_GATE_LORA, 3 * MIX_HALF)


def rwkv_mix(p, mu, w0, w2, a0, a2, g2, k_k, k_a, r_k, gn):
    R = p.shape[0]
    km3 = lambda t: jnp.concatenate([_kmajor(x) for x in jnp.split(t, 3, axis=-1)], axis=-1)
    mu_p = _pack_cols(mu, _AB_CUTS[:3], _AB_WIDTHS[:4])
    mu_p = jnp.concatenate([km3(mu_p[:, :AB_WD]), mu_p[:, AB_WD:]], axis=-1)
    lora = lambda w: jnp.pad(jnp.concatenate([_kmajor(w[0]), _kmajor(w[1])], axis=-1),
                             ((0, LANE - w.shape[1]), (0, 0))).astype(BF16)
    vec = jnp.stack([_kmajor(t) for t in (w0[0], w0[1], a0[0], a0[1], k_k, k_a, r_k, jnp.zeros_like(r_k))])
    a, b0, b1, k0, k1, d0, d1, r, v, g, bonus = rwkv_prep(p, mu_p, lora(w2), lora(a2),
                                                          _kmajor(g2).astype(BF16), vec)
    tile = lambda t: t.reshape(R, RWKV_HEAD, RWKV_HEADS)
    vp = jnp.broadcast_to(tile(v)[:, :, None, :], (R, RWKV_HEAD, 4, RWKV_HEADS)).reshape(R, RWKV_HEAD, LANE)
    y0, y1 = wkv_scan(a, b0, b1, k0, k1, d0, d1, r, vp)
    y = (y0 + y1)[:, :, :RWKV_HEADS]
    mean = jnp.mean(y, axis=1, keepdims=True)
    var = jnp.mean(jnp.square(y - mean), axis=1, keepdims=True)
    gn_t = _kmajor(gn).reshape(2, RWKV_HEAD, RWKV_HEADS)
    yn = (y - mean) * lax.rsqrt(var + RWKV_GN_EPS) * gn_t[0] + gn_t[1]
    out = (yn + bonus[:, None, :RWKV_HEADS] * tile(v)) * tile(g)
    return out.reshape(R, MIX_HALF)


NA_WIN = NA_KH * GRID_W


def _na_kernel(q_ref, k_ref, v_ref, bias_ref, o_ref, *, rows):
    r = pl.program_id(1)
    r0 = jnp.clip(r - NA_KH // 2, 0, rows - NA_KH)
    start = pl.multiple_of(CTX_LEN + r0 * GRID_W, GRID_W)
    q = (q_ref[...] * NA_HEAD_DIM ** -0.5).astype(BF16)
    dn = (((1,), (1,)), ((), ()))
    k_win = k_ref[pl.ds(start, NA_WIN), :].astype(BF16)
    s_loc = lax.dot_general(q, k_win, dn, preferred_element_type=F32) + bias_ref[0, 0]
    s_ctx = lax.dot_general(q, k_ref[pl.ds(0, CTX_LEN), :].astype(BF16), dn, preferred_element_type=F32)
    m = jnp.maximum(jnp.max(s_loc, axis=-1, keepdims=True), jnp.max(s_ctx, axis=-1, keepdims=True))
    p_loc = jnp.exp(s_loc - m)
    p_ctx = jnp.exp(s_ctx - m)
    den = jnp.sum(p_loc, axis=-1, keepdims=True) + jnp.sum(p_ctx, axis=-1, keepdims=True)
    num = jnp.dot(p_loc.astype(BF16), v_ref[pl.ds(start, NA_WIN), :].astype(BF16), preferred_element_type=F32)
    num += jnp.dot(p_ctx.astype(BF16), v_ref[pl.ds(0, CTX_LEN), :].astype(BF16), preferred_element_type=F32)
    o_ref[...] = num / den


def _na_ctx_kernel(q_ref, k_ref, v_ref, o_ref):
    q = (q_ref[...] * NA_HEAD_DIM ** -0.5).astype(BF16)
    s = lax.dot_general(q, k_ref[...].astype(BF16), (((1,), (1,)), ((), ())), preferred_element_type=F32)
    p = jnp.exp(s - jnp.max(s, axis=-1, keepdims=True))
    num = jnp.dot(p.astype(BF16), v_ref[...].astype(BF16), preferred_element_type=F32)
    o_ref[...] = num / jnp.sum(p, axis=-1, keepdims=True)


def _na_bias(rpb, rows):
    kh = min(NA_KH, rows)
    col = np.arange(GRID_W)
    c0 = np.clip(col - NA_KW // 2, 0, GRID_W - NA_KW)
    key = np.arange(GRID_W)
    inside = (key[None, :] >= c0[:, None]) & (key[None, :] < c0[:, None] + NA_KW)
    dc = np.clip(key[None, :] - col[:, None] + (NA_KW - 1), 0, 2 * NA_KW - 2)
    onehot = np.zeros((GRID_W, GRID_W, 2 * NA_KW - 1), np.float32)
    onehot[col[:, None], key[None, :], dc] = 1.0
    toep = jnp.einsum("qcd,hrd->hrqc", onehot, rpb, precision=lax.Precision.HIGHEST)
    toep = jnp.where(jnp.asarray(inside), toep, NEG)
    return jnp.stack([jnp.concatenate([toep[:, i - j + NA_KH - 1] for i in range(kh)], axis=-1)
                      for j in range(NA_KH)], axis=1)


def na_mix(p, rpb):
    R = p.shape[0]
    T = R - CTX_LEN
    rows = T // GRID_W
    qb, kb, vb = [(AB_NA + i * MIX_HALF) // NA_HEAD_DIM for i in range(3)]
    cb = CTX_LEN // GRID_W
    bias = _na_bias(rpb, rows)
    strip = lambda b: pl.BlockSpec((R, NA_HEAD_DIM), lambda h, r: (0, b + h))
    lat = pl.pallas_call(
        functools.partial(_na_kernel, rows=rows),
        grid=(NA_HEADS, rows),
        in_specs=[pl.BlockSpec((GRID_W, NA_HEAD_DIM), lambda h, r: (cb + r, qb + h)),
                  strip(kb), strip(vb),
                  pl.BlockSpec((1, 1, GRID_W, NA_WIN),
                               lambda h, r: (h, r - jnp.clip(r - NA_KH // 2, 0, rows - NA_KH), 0, 0))],
        out_specs=pl.BlockSpec((GRID_W, NA_HEAD_DIM), lambda h, r: (r, h)),
        out_shape=jax.ShapeDtypeStruct((T, MIX_HALF), F32),
        compiler_params=_params(("parallel", "arbitrary")),
        name="na_latent",
    )(p, p, p, bias)
    blk = lambda b: pl.BlockSpec((CTX_LEN, NA_HEAD_DIM), lambda h: (0, b + h))
    ctx = pl.pallas_call(
        _na_ctx_kernel,
        grid=(NA_HEADS,),
        in_specs=[blk(qb), blk(kb), blk(vb)],
        out_specs=pl.BlockSpec((CTX_LEN, NA_HEAD_DIM), lambda h: (0, h)),
        out_shape=jax.ShapeDtypeStruct((CTX_LEN, MIX_HALF), F32),
        compiler_params=_params(("parallel",)),
        name="na_context",
    )(p, p, p)
    return jnp.concatenate([ctx, lat], axis=0)


ML_CHUNK = CTX_LEN
CD_GT = 2 * ML_HEADS * ML_DQK + 2 * MIX_HALF
CD_HY = CD_GT + LANE


def _mlstm_kernel(q_ref, k_ref, v_ref, lir_ref, lfr_ref, lic_ref, lfc_ref, o_ref, c_ref, n_ref, m_ref):
    L = ML_CHUNK

    @pl.when(pl.program_id(2) == 0)
    def _():
        c_ref[...] = jnp.zeros_like(c_ref)
        n_ref[...] = jnp.zeros_like(n_ref)
        m_ref[...] = jnp.zeros_like(m_ref)

    sgn = 1 - 2 * pl.program_id(0)
    row = lax.broadcasted_iota(jnp.int32, (L, L), 0)
    col = lax.broadcasted_iota(jnp.int32, (L, L), 1)
    seen = (row - col) * sgn >= 0
    seen_t = (col - row) * sgn >= 0
    q = q_ref[...]
    k = k_ref[...]
    vb = v_ref[...].astype(BF16)
    li_r, lf_r, li_c, lf_c = lir_ref[0], lfr_ref[0], lic_ref[0], lfc_ref[0]
    b_c = jnp.sum(jnp.where(seen, lf_r, 0.0), axis=1, keepdims=True)
    b_r = jnp.sum(jnp.where(seen_t, lf_c, 0.0), axis=0, keepdims=True)
    m_prev = m_ref[...]
    dmat = jnp.where(seen, b_c - b_r + li_r, NEG)
    inter = b_c + m_prev
    m_t = jnp.maximum(jnp.max(dmat, axis=1, keepdims=True), inter)
    qb = q.astype(BF16)
    s = lax.dot_general(qb, k.astype(BF16), (((1,), (1,)), ((), ())), preferred_element_type=F32)
    s = s * jnp.exp(dmat - m_t)
    dec = jnp.exp(inter - m_t)
    num = jnp.dot(s.astype(BF16), vb, preferred_element_type=F32)
    num += dec * jnp.dot(qb, c_ref[...].astype(BF16), preferred_element_type=F32)
    den = jnp.sum(s, axis=1, keepdims=True) + dec * jnp.sum(q * n_ref[...], axis=1, keepdims=True)
    o_ref[0] = num / jnp.maximum(jnp.abs(den), jnp.exp(-m_t))
    total = jnp.sum(lf_r, axis=1, keepdims=True)
    gl = total - b_c + li_c
    m_new = jnp.maximum(total + m_prev, jnp.max(gl, axis=0, keepdims=True))
    kw = k * jnp.exp(gl - m_new)
    sc = jnp.exp(total + m_prev - m_new)
    c_ref[...] = sc * c_ref[...] + lax.dot_general(kw.astype(BF16), vb, (((0,), (0,)), ((), ())),
                                                   preferred_element_type=F32)
    n_ref[...] = sc * n_ref[...] + jnp.sum(kw, axis=0, keepdims=True)
    m_ref[...] = m_new


def _axial_rope(x):
    T, d = x.shape[0], x.shape[-1]
    half = d // 2
    nf = half // 2
    t = jnp.arange(T)
    row = (t // GRID_W).astype(F32)
    col = (t % GRID_W).astype(F32)
    inv = ROPE_BASE ** (-jnp.arange(nf, dtype=F32) / nf)
    ang = jnp.concatenate([row[:, None] * inv, col[:, None] * inv], axis=-1)[:, None, :]
    cos, sin = jnp.cos(ang), jnp.sin(ang)
    x1, x2 = x[..., :half], x[..., half:]
    return jnp.concatenate([x1 * cos - x2 * sin, x2 * cos + x1 * sin], axis=-1)


def mlstm_mix(p, gate_b, norm_w):
    R = p.shape[0]
    L = ML_CHUNK
    nc = R // L
    QK = ML_HEADS * ML_DQK

    def rope_lat(t):
        th = t.reshape(R, ML_HEADS, ML_DQK)
        return jnp.concatenate([th[:CTX_LEN], _axial_rope(th[CTX_LEN:])], axis=0).reshape(R, QK)

    q = rope_lat(p[:, :QK])
    k = rope_lat(p[:, QK:2 * QK]) * ML_DQK ** -0.5
    o = p[:, 2 * QK + MIX_HALF:CD_GT]
    gt = p[:, CD_GT:CD_GT + 4 * ML_HEADS].reshape(R, 2, 2, ML_HEADS) + gate_b
    log_i = jnp.transpose(gt[:, :, 0], (1, 2, 0))
    log_f = jax.nn.log_sigmoid(jnp.transpose(gt[:, :, 1], (1, 2, 0)))
    as_rows = lambda t: t.reshape(2 * ML_HEADS * nc, 1, L)
    as_cols = lambda t: t.reshape(2 * ML_HEADS * nc, L, 1)

    def chunk(d, c):
        return jnp.where(d == 0, c, jnp.where(c == 0, 0, nc - c))

    gidx = lambda d, h, c: ((d * ML_HEADS + h) * nc + chunk(d, c), 0, 0)
    vb = 2 * QK // ML_DV
    h_dir = pl.pallas_call(
        _mlstm_kernel,
        grid=(2, ML_HEADS, nc),
        in_specs=[pl.BlockSpec((L, ML_DQK), lambda d, h, c: (chunk(d, c), h)),
                  pl.BlockSpec((L, ML_DQK), lambda d, h, c: (chunk(d, c), h)),
                  pl.BlockSpec((L, ML_DV), lambda d, h, c: (chunk(d, c), vb + h)),
                  pl.BlockSpec((1, 1, L), gidx), pl.BlockSpec((1, 1, L), gidx),
                  pl.BlockSpec((1, L, 1), gidx), pl.BlockSpec((1, L, 1), gidx)],
        out_specs=pl.BlockSpec((1, L, ML_DV), lambda d, h, c: (d, chunk(d, c), h)),
        out_shape=jax.ShapeDtypeStruct((2, R, MIX_HALF), F32),
        scratch_shapes=[pltpu.VMEM((ML_DQK, ML_DV), F32), pltpu.VMEM((1, ML_DQK), F32),
                        pltpu.VMEM((1, 1), F32)],
        compiler_params=_params(("parallel", "parallel", "arbitrary")),
        name="mlstm",
    )(q, k, p, as_rows(log_i), as_rows(log_f), as_cols(log_i), as_cols(log_f))
    h = (h_dir[0] + h_dir[1]).reshape(R, ML_HEADS, ML_DV)
    h = h * lax.rsqrt(jnp.mean(h * h, axis=-1, keepdims=True) + RMS_EPS)
    return h.reshape(R, MIX_HALF) * norm_w * jax.nn.sigmoid(o)


def _pad_to(x, axis, mult):
    n = -x.shape[axis] % mult
    if n == 0:
        return x
    pad = [(0, 0)] * x.ndim
    pad[axis] = (0, n)
    return jnp.pad(x, pad)


def mm3p(a, b):
    M, N = a.shape[0], b.shape[1]
    a = _pad_to(_pad_to(a, 1, LANE), 0, 8)
    b = _pad_to(_pad_to(b, 0, LANE), 1, LANE)
    return mm3(a, b)[:M, :N]


def hyena_filters(T, ff1, ff1_b, ff2, ff2_b, ff3, ff3_b, ff_out, freq, decay):
    t = jnp.linspace(0.0, 1.0, T, dtype=F32)[:, None]
    bands = (HY_EMB - 1) // 2
    f = jnp.linspace(1e-4, bands - 1, bands, dtype=F32)[None, :]
    w = (2 * math.pi / T) * jnp.arange(T, dtype=F32)[:, None]
    z = jnp.concatenate([t, jnp.cos(f * w), -jnp.sin(f * w)], axis=-1)
    h = jnp.sin(freq * (mm3p(z, ff1) + ff1_b))
    h = jnp.sin(freq * (mm3p(h, ff2) + ff2_b))
    h = jnp.sin(freq * (mm3p(h, ff3) + ff3_b))
    h = mm3p(h, ff_out) * jnp.exp(-t * jnp.abs(decay))
    return h.reshape(T, HY_ORDER, 2, MIX_HALF)


FFT_N2 = LANE


@functools.lru_cache(maxsize=None)
def _dft_tables(n1):
    n2 = FFT_N2
    n = n1 * n2
    i1, i2 = np.arange(n1), np.arange(n2)
    a1 = 2 * np.pi * np.outer(i1, i1) / n1
    a2 = 2 * np.pi * np.outer(i2, i2) / n2
    c1, s1 = np.cos(a1), -np.sin(a1)
    c2, s2 = np.cos(a2), -np.sin(a2)
    atw = 2 * np.pi * np.outer(i2, i1) / n
    f32 = lambda x: np.asarray(x, np.float32)
    return dict(
        wa=f32(np.concatenate([c1, s1], axis=1)),
        wb=f32(np.block([[c2, s2], [-s2, c2]])),
        wb_inv=f32(np.block([[c2, -s2], [s2, c2]])),
        wa_inv=f32(np.concatenate([c1, s1], axis=0)[:, :n1 // 2] / n),
        twr=f32(np.cos(atw)), twi=f32(-np.sin(atw)))


def _fft_fwd(x_t, n1):
    C = x_t.shape[0]
    n2 = FFT_N2
    tb = _dft_tables(n1)
    xa = jnp.transpose(x_t.reshape(C, n1, n2), (0, 2, 1)).reshape(C * n2, n1)
    a = mm3p(xa, tb["wa"]).reshape(C, n2, 2, n1)
    ar, ai = a[:, :, 0], a[:, :, 1]
    br = ar * tb["twr"] - ai * tb["twi"]
    bi = ar * tb["twi"] + ai * tb["twr"]
    b = jnp.concatenate([jnp.transpose(br, (0, 2, 1)), jnp.transpose(bi, (0, 2, 1))], axis=-1)
    return mm3(b.reshape(C * n1, 2 * n2), tb["wb"])


def _fft_inv(y, C, n1):
    n2 = FFT_N2
    tb = _dft_tables(n1)
    g = mm3(y, tb["wb_inv"]).reshape(C, n1, 2, n2)
    gr, gi = g[:, :, 0], g[:, :, 1]
    twr, twi = tb["twr"].T, tb["twi"].T
    hr = gr * twr + gi * twi
    hi = gi * twr - gr * twi
    h = jnp.concatenate([jnp.transpose(hr, (0, 2, 1)), jnp.transpose(hi, (0, 2, 1))], axis=-1)
    out = mm3p(h.reshape(C * n2, 2 * n1), tb["wa_inv"])
    return jnp.transpose(out.reshape(C, n2, n1 // 2), (0, 2, 1)).reshape(C, n1 // 2 * n2)


def _long_conv(z, h_fwd, h_bwd):
    T, C = z.shape
    n1 = 2 * T // FFT_N2
    f = jnp.concatenate([h_fwd, jnp.zeros((1, C), F32), h_bwd[:0:-1]], axis=0)
    zs = _fft_fwd(jnp.pad(z.T, ((0, 0), (0, T))), n1).reshape(C * n1, 2, FFT_N2)
    fs = _fft_fwd(f.T, n1).reshape(C * n1, 2, FFT_N2)
    yr = zs[:, 0] * fs[:, 0] - zs[:, 1] * fs[:, 1]
    yi = zs[:, 0] * fs[:, 1] + zs[:, 1] * fs[:, 0]
    return _fft_inv(jnp.concatenate([yr, yi], axis=-1), C, n1).T


def hyena_seq(p, short_w, short_b, filt, bias):
    u = short_w[0] * _shift_prev(p) + short_w[1] * p + short_w[2] * _shift_next(p) + short_b
    z, x1, x2 = jnp.split(u, 3, axis=-1)
    for n, gate in enumerate((x1, x2)):
        z = gate * (_long_conv(z, filt[:, n, 0], filt[:, n, 1]) + bias[n] * z)
    return z


def hyena_mix(p, short_w, short_b, filt_params, bias):
    T = p.shape[0] - CTX_LEN
    out_c = hyena_seq(p[:CTX_LEN], short_w, short_b, hyena_filters(CTX_LEN, *filt_params), bias)
    out = hyena_seq(p[CTX_LEN:], short_w, short_b, hyena_filters(T, *filt_params), bias)
    return jnp.concatenate([out_c, out], axis=0)


def ffn_half(x, mod, g_pre, g_post, w1, w3, w2):
    h = norm_mod(x, g_pre, mod[:, 0], mod[:, 1])
    a = swiglu(h, w1.astype(BF16), w3.astype(BF16))
    return post_res(x, mm(a, w2), g_post, mod[:, 2], FFN_RES)


def kernel(x, c, ctx, c_ctx, ada_down, ada_up, ada_bias, norm_g, ffn_w1, ffn_w3, ffn_w2, ab_w_in, ab_w_out, rwkv_mu, rwkv_w0, rwkv_w2, rwkv_a0, rwkv_a2, rwkv_g2, rwkv_kk, rwkv_ka, rwkv_rk, rwkv_gn, na_rpb, cd_w_in, cd_w_out, ml_gate_b, ml_norm, hy_short_w, hy_short_b, hy_ff1, hy_ff1_b, hy_ff2, hy_ff2_b, hy_ff3, hy_ff3_b, hy_ff_out, hy_freq, hy_decay, hy_bias):
    assert x.shape[0] == 1 and ctx.shape[1] == CTX_LEN
    X = jnp.concatenate([ctx[0], x[0]], axis=0)
    s2 = _pad_to(jnp.stack([jax.nn.silu(c_ctx), jax.nn.silu(c[0])]), 0, 16)
    for l in range(DEPTH):
        i = l // 2
        mod = (mm(mm(s2, ada_down[l]), ada_up[l])[:2] + ada_bias[l]).reshape(2, N_SUB, 3, D_MODEL)
        g = norm_g[l]
        X = ffn_half(X, mod[:, 0], g[0], g[1], ffn_w1[l, 0], ffn_w3[l, 0], ffn_w2[l, 0])
        xn = norm_mod(X, g[2], mod[:, 1, 0], mod[:, 1, 1])
        if l % 2 == 0:
            w_ab = ab_w_in[i].astype(BF16)
            w_rkv = [_kmajor(t) for t in jnp.split(w_ab[:, :AB_WD], 3, axis=-1)]
            p = mm(xn, _pack_cols(jnp.concatenate(w_rkv + [w_ab[:, AB_WD:]], axis=-1), _AB_CUTS, _AB_WIDTHS))
            ya = rwkv_mix(p, rwkv_mu[i], rwkv_w0[i], rwkv_w2[i], rwkv_a0[i], rwkv_a2[i],
                          rwkv_g2[i], rwkv_kk[i], rwkv_ka[i], rwkv_rk[i], rwkv_gn[i])
            y = jnp.concatenate([ya, na_mix(p, na_rpb[i])], axis=-1)
            w_out = ab_w_out[i].astype(BF16)
            w_rw = jnp.swapaxes(w_out[:MIX_HALF].reshape(RWKV_HEADS, RWKV_HEAD, D_MODEL), 0, 1)
            w_out = jnp.concatenate([w_rw.reshape(MIX_HALF, D_MODEL), w_out[MIX_HALF:]], axis=0)
        else:
            w_in = _pack_cols(cd_w_in[i], (CD_GT, ML_COLS), (CD_GT, LANE, P_COLS - CD_HY))
            p = mm(xn, w_in)
            filt_params = (hy_ff1[i], hy_ff1_b[i], hy_ff2[i], hy_ff2_b[i], hy_ff3[i], hy_ff3_b[i],
                           hy_ff_out[i], hy_freq[i], hy_decay[i])
            yd = hyena_mix(p[:, CD_HY:CD_HY + 3 * MIX_HALF], hy_short_w[i], hy_short_b[i], filt_params,
                           hy_bias[i])
            y = jnp.concatenate([mlstm_mix(p, ml_gate_b[i], ml_norm[i]), yd], axis=-1)
            w_out = cd_w_out[i]
        X = post_res(X, mm(y, w_out), g[3], mod[:, 1, 2], 1.0)
        X = ffn_half(X, mod[:, 2], g[4], g[5], ffn_w1[l, 1], ffn_w3[l, 1], ffn_w2[l, 1])
    return X[CTX_LEN:][None]
```

```python
import functools
import math

import jax
import jax.numpy as jnp
import numpy as np
from jax import lax
from jax.experimental import pallas as pl
from jax.experimental.pallas import tpu as pltpu

F32 = jnp.float32
BF16 = jnp.bfloat16

D_MODEL = 4096
DEPTH = 4
GRID_W = 64
CTX_LEN = 256
N_SUB = 3
FFN_RES = 0.5
D_FF = 5632
RMS_EPS = 1e-6
MIX_HALF = D_MODEL // 2

RWKV_HEAD = 64
RWKV_HEADS = MIX_HALF // RWKV_HEAD
RWKV_DECAY_LORA = 96
RWKV_A_LORA = 96
RWKV_GATE_LORA = 256
RWKV_COLS = 3 * MIX_HALF + RWKV_DECAY_LORA + RWKV_A_LORA + RWKV_GATE_LORA
RWKV_GN_EPS = 64e-5

NA_HEAD_DIM = 128
NA_HEADS = MIX_HALF // NA_HEAD_DIM
NA_KH = 8
NA_KW = 16

ML_HEADS = 4
ML_DV = MIX_HALF // ML_HEADS
ML_DQK = ML_DV // 2
ML_COLS = 2 * ML_HEADS * ML_DQK + 2 * MIX_HALF + 4 * ML_HEADS
ROPE_BASE = 10000.0

HY_ORDER = 2
HY_EMB = 33

LANE = 128
ROW_TILE = CTX_LEN
VMEM_LIMIT = 56 * 1024 * 1024
P_COLS = 12800
NEG = -1e30


def _params(sem):
    return pltpu.CompilerParams(dimension_semantics=sem, vmem_limit_bytes=VMEM_LIMIT)


def _pick(n, cands):
    for c in cands:
        if n % c == 0:
            return c
    return n


def _mm_kernel(a_ref, b_ref, o_ref):
    o_ref[...] = jnp.dot(a_ref[...], b_ref[...], preferred_element_type=F32).astype(o_ref.dtype)


def _split(x):
    hi = x.astype(BF16)
    lo = (x - hi.astype(F32)).astype(BF16)
    return hi, lo


def _mm3_kernel(a_ref, b_ref, o_ref):
    a_hi, a_lo = _split(a_ref[...])
    b_hi, b_lo = _split(b_ref[...])
    acc = jnp.dot(a_hi, b_hi, preferred_element_type=F32)
    acc += jnp.dot(a_hi, b_lo, preferred_element_type=F32)
    acc += jnp.dot(a_lo, b_hi, preferred_element_type=F32)
    o_ref[...] = acc


def _mm_call(kern, a, b, out_dtype, name):
    M, K = a.shape
    N = b.shape[1]
    tm = _pick(M, (1024, 768, 512, 256, 128, 64, 32, 16, 8))
    tn = _pick(N, (512, 256, 128))
    return pl.pallas_call(
        kern,
        grid=(M // tm, N // tn),
        in_specs=[pl.BlockSpec((tm, K), lambda i, j: (i, 0)),
                  pl.BlockSpec((K, tn), lambda i, j: (0, j))],
        out_specs=pl.BlockSpec((tm, tn), lambda i, j: (i, j)),
        out_shape=jax.ShapeDtypeStruct((M, N), out_dtype),
        compiler_params=_params(("parallel", "parallel")),
        name=name,
    )(a, b)


def mm(a, b, out_dtype=F32):
    return _mm_call(_mm_kernel, a.astype(BF16), b.astype(BF16), out_dtype, "mm")


def mm3(a, b):
    return _mm_call(_mm3_kernel, a.astype(F32), b.astype(F32), F32, "mm3")


def _swiglu_kernel(a_ref, w1_ref, w3_ref, o_ref):
    a = a_ref[...]
    h1 = jnp.dot(a, w1_ref[...], preferred_element_type=F32)
    h3 = jnp.dot(a, w3_ref[...], preferred_element_type=F32)
    o_ref[...] = (h1 * jax.nn.sigmoid(h1) * h3).astype(o_ref.dtype)


def swiglu(a, w1, w3):
    M, K = a.shape
    N = w1.shape[1]
    tm = _pick(M, (768, 512, 256))
    tn = _pick(N, (512, 256, 128))
    wspec = pl.BlockSpec((K, tn), lambda i, j: (0, j))
    return pl.pallas_call(
        _swiglu_kernel,
        grid=(M // tm, N // tn),
        in_specs=[pl.BlockSpec((tm, K), lambda i, j: (i, 0)), wspec, wspec],
        out_specs=pl.BlockSpec((tm, tn), lambda i, j: (i, j)),
        out_shape=jax.ShapeDtypeStruct((M, N), BF16),
        compiler_params=_params(("parallel", "parallel")),
        name="swiglu",
    )(a, w1, w3)


def _group_spec(d):
    return pl.BlockSpec((1, 1, d), lambda i: (jnp.minimum(i, 1), 0, 0))


def _norm_mod_kernel(x_ref, g_ref, sh_ref, sc_ref, o_ref):
    x = x_ref[...]
    y = x * lax.rsqrt(jnp.mean(x * x, axis=-1, keepdims=True) + RMS_EPS) * g_ref[...]
    o_ref[...] = (y * (1 + sc_ref[0]) + sh_ref[0]).astype(o_ref.dtype)


def norm_mod(x, g, shift, scale):
    R, D = x.shape
    row = pl.BlockSpec((ROW_TILE, D), lambda i: (i, 0))
    return pl.pallas_call(
        _norm_mod_kernel,
        grid=(R // ROW_TILE,),
        in_specs=[row, pl.BlockSpec((1, D), lambda i: (0, 0)), _group_spec(D), _group_spec(D)],
        out_specs=row,
        out_shape=jax.ShapeDtypeStruct((R, D), BF16),
        compiler_params=_params(("parallel",)),
        name="norm_mod",
    )(x, g.reshape(1, D), shift.reshape(2, 1, D), scale.reshape(2, 1, D))


def _post_res_kernel(x_ref, y_ref, g_ref, gate_ref, o_ref, *, coef):
    y = y_ref[...]
    yn = y * lax.rsqrt(jnp.mean(y * y, axis=-1, keepdims=True) + RMS_EPS) * g_ref[...]
    o_ref[...] = x_ref[...] + coef * gate_ref[0] * yn


def post_res(x, y, g, gate, coef):
    R, D = x.shape
    row = pl.BlockSpec((ROW_TILE, D), lambda i: (i, 0))
    return pl.pallas_call(
        functools.partial(_post_res_kernel, coef=coef),
        grid=(R // ROW_TILE,),
        in_specs=[row, row, pl.BlockSpec((1, D), lambda i: (0, 0)), _group_spec(D)],
        out_specs=row,
        out_shape=jax.ShapeDtypeStruct((R, D), F32),
        compiler_params=_params(("parallel",)),
        name="post_res",
    )(x, y, g.reshape(1, D), gate.reshape(2, 1, D))


WKV_TB = 8
WKV_KG = MIX_HALF // LANE
WKV_VC = RWKV_HEAD // 8
PREP_TB = 128


def _kmajor(w):
    lead = w.shape[:-1]
    return jnp.swapaxes(w.reshape(*lead, RWKV_HEADS, RWKV_HEAD), -1, -2).reshape(*lead, MIX_HALF)


def _fold_lanes(s):
    s = s + pltpu.roll(s, 64, 1)
    return s + pltpu.roll(s, 32, 1)


def _rwkv_prep_kernel(p_ref, hp_ref, hn_ref, mu_ref, w2_ref, a2_ref, g2_ref, vec_ref,
                      a_out, b0_out, b1_out, k0_out, k1_out, w0_out, w1_out, r_out, v_out, g_out, bonus_out):
    W = MIX_HALF
    i = pl.program_id(0)
    cb = CTX_LEN // PREP_TB
    has_prev = jnp.logical_and(i != 0, i != cb).astype(F32)
    has_next = jnp.logical_and(i != cb - 1, i != pl.num_programs(0) - 1).astype(F32)
    row = lax.broadcasted_iota(jnp.int32, (PREP_TB, 1), 0)

    def shifted(c0, width):
        x = p_ref[:, c0:c0 + width]
        up = jnp.where(row == 0, hp_ref[7:8, c0:c0 + width] * has_prev, pltpu.roll(x, 1, 0))
        dn = jnp.where(row == PREP_TB - 1, hn_ref[0:1, c0:c0 + width] * has_next,
                       pltpu.roll(x, PREP_TB - 1, 0))
        return x + mu_ref[0:1, c0:c0 + width] * (up - x) + mu_ref[1:2, c0:c0 + width] * (dn - x)

    lo = shifted(AB_WD, AB_NA - AB_WD)
    wd, ad, gd = lo[:, :LANE], lo[:, LANE:2 * LANE], lo[:, 2 * LANE:]
    wlo = jnp.dot(jnp.tanh(wd).astype(BF16), w2_ref[...], preferred_element_type=F32)
    alo = jnp.dot(ad.astype(BF16), a2_ref[...], preferred_element_type=F32)
    g_out[...] = jnp.dot(jax.nn.sigmoid(gd).astype(BF16), g2_ref[...], preferred_element_type=F32)

    ss = None
    for g in range(WKV_KG):
        sl = slice(g * LANE, (g + 1) * LANE)
        kk = shifted(W + g * LANE, LANE) * vec_ref[4:5, sl]
        ss = kk * kk if ss is None else ss + kk * kk
    nrm = jnp.maximum(jnp.sqrt(_fold_lanes(ss)), 1e-12)
    bonus = None
    for g in range(WKV_KG):
        sl = slice(g * LANE, (g + 1) * LANE)
        r = shifted(g * LANE, LANE)
        k = shifted(W + g * LANE, LANE)
        r_out[:, sl] = r
        v_out[:, sl] = shifted(2 * W + g * LANE, LANE)
        kk = k * vec_ref[4:5, sl] / nrm
        a_out[:, sl] = -kk
        for d, (w_out, k_out, b_out) in enumerate(((w0_out, k0_out, b0_out), (w1_out, k1_out, b1_out))):
            dl = slice(d * W + g * LANE, d * W + (g + 1) * LANE)
            logw = -jax.nn.softplus(-(vec_ref[d:d + 1, sl] + wlo[:, dl])) - 0.5
            w_out[:, sl] = jnp.exp(-jnp.exp(logw))
            a = jax.nn.sigmoid(vec_ref[2 + d:3 + d, sl] + alo[:, dl])
            kd = k * (1 + (a - 1) * vec_ref[5:6, sl])
            k_out[:, sl] = kd
            b_out[:, sl] = kk * a
            term = r * kd * vec_ref[6:7, sl]
            bonus = term if bonus is None else bonus + term
    bonus_out[...] = _fold_lanes(bonus)


def rwkv_prep(p, mu, w2c, a2c, g2, vec):
    R = p.shape[0]
    W = MIX_HALF
    hb = PREP_TB // 8
    full = lambda a: pl.BlockSpec(a.shape, lambda i: (0,) * a.ndim)
    wide = pl.BlockSpec((PREP_TB, W), lambda i: (i, 0))
    return pl.pallas_call(
        _rwkv_prep_kernel,
        grid=(R // PREP_TB,),
        in_specs=[pl.BlockSpec((PREP_TB, AB_NA), lambda i: (i, 0)),
                  pl.BlockSpec((8, AB_NA), lambda i: (jnp.maximum(i * hb - 1, 0), 0)),
                  pl.BlockSpec((8, AB_NA), lambda i: (jnp.minimum((i + 1) * hb, R // 8 - 1), 0)),
                  full(mu), full(w2c), full(a2c), full(g2), full(vec)],
        out_specs=[wide] * 10 + [pl.BlockSpec((PREP_TB, LANE), lambda i: (i, 0))],
        out_shape=[jax.ShapeDtypeStruct((R, W), F32)] * 10 + [jax.ShapeDtypeStruct((R, LANE), F32)],
        compiler_params=_params(("parallel",)),
        name="rwkv_prep",
    )(p, p, p, mu, w2c, a2c, g2, vec)


def _wkv_kernel(a0_ref, b0_ref, k0_ref, w0_ref, r0_ref, v0_ref, a1_ref, b1_ref, k1_ref, w1_ref, r1_ref, v1_ref,
                y0_ref, y1_ref, s_ref):
    @pl.when(pl.program_id(0) == 0)
    def _():
        s_ref[...] = jnp.zeros_like(s_ref)

    dirs = ((a0_ref, b0_ref, k0_ref, w0_ref, r0_ref, v0_ref, y0_ref),
            (a1_ref, b1_ref, k1_ref, w1_ref, r1_ref, v1_ref, y1_ref))

    for t in range(WKV_TB):
        for d, (a_ref, b_ref, k_ref, w_ref, r_ref, v_ref, y_ref) in enumerate(dirs):
            tt = t if d == 0 else WKV_TB - 1 - t
            row = lambda ref, g: ref[tt:tt + 1, g * LANE:(g + 1) * LANE]
            acc = [None] * WKV_VC
            for g in range(WKV_KG):
                ar = row(a_ref, g)
                for c in range(WKV_VC):
                    term = s_ref[d, g, pl.ds(c * 8, 8), :] * ar
                    acc[c] = term if acc[c] is None else acc[c] + term
            sa = [_fold_lanes(x) for x in acc]
            vv = [v_ref[tt, pl.ds(c * 8, 8), :] for c in range(WKV_VC)]
            acc = [None] * WKV_VC
            for g in range(WKV_KG):
                wr, br, kr, rr = row(w_ref, g), row(b_ref, g), row(k_ref, g), row(r_ref, g)
                for c in range(WKV_VC):
                    s_new = s_ref[d, g, pl.ds(c * 8, 8), :] * wr + sa[c] * br + vv[c] * kr
                    s_ref[d, g, pl.ds(c * 8, 8), :] = s_new
                    term = s_new * rr
                    acc[c] = term if acc[c] is None else acc[c] + term
            for c in range(WKV_VC):
                y_ref[tt, pl.ds(c * 8, 8), :] = _fold_lanes(acc[c])


def wkv_scan(a, b0, b1, k0, k1, w0, w1, r, vp):
    R = a.shape[0]
    nb = R // WKV_TB
    cb = CTX_LEN // WKV_TB
    fwd = lambda i: i
    bwd = lambda i: jnp.where(i < cb, cb - 1 - i, nb + cb - 1 - i)
    kspec = lambda f: pl.BlockSpec((WKV_TB, MIX_HALF), lambda i: (f(i), 0))
    vspec = lambda f: pl.BlockSpec((WKV_TB, RWKV_HEAD, LANE), lambda i: (f(i), 0, 0))
    yshape = jax.ShapeDtypeStruct((R, RWKV_HEAD, LANE), F32)
    return pl.pallas_call(
        _wkv_kernel,
        grid=(nb,),
        in_specs=[kspec(fwd)] * 5 + [vspec(fwd)] + [kspec(bwd)] * 5 + [vspec(bwd)],
        out_specs=[vspec(fwd), vspec(bwd)],
        out_shape=[yshape, yshape],
        scratch_shapes=[pltpu.VMEM((2, WKV_KG, RWKV_HEAD, LANE), F32)],
        compiler_params=_params(("arbitrary",)),
        name="wkv_scan",
    )(a, b0, k0, w0, r, vp, a, b1, k1, w1, r, vp)


def _shift_prev(y):
    return jnp.pad(y, ((1, 0), (0, 0)))[:-1]


def _shift_next(y):
    return jnp.pad(y, ((0, 1), (0, 0)))[1:]


AB_WD = 3 * MIX_HALF
AB_AD = AB_WD + LANE
AB_GD = AB_AD + LANE
AB_NA = AB_GD + RWKV_GATE_LORA


def _pack_cols(w, cuts, widths):
    pieces = jnp.split(w, cuts, axis=-1)
    pad = [(0, 0)] * (w.ndim - 1)
    return jnp.concatenate([jnp.pad(p, pad + [(0, wd - p.shape[-1])]) for p, wd in zip(pieces, widths)],
                           axis=-1)


_AB_CUTS = (AB_WD, AB_WD + RWKV_DECAY_LORA, AB_WD + RWKV_DECAY_LORA + RWKV_A_LORA, RWKV_COLS)
_AB_WIDTHS = (AB_WD, LANE, LANE, RWKV_GATE_LORA, 3 * MIX_HALF)


def rwkv_mix(p, mu, w0, w2, a0, a2, g2, k_k, k_a, r_k, gn):
    R = p.shape[0]
    km3 = lambda t: jnp.concatenate([_kmajor(x) for x in jnp.split(t, 3, axis=-1)], axis=-1)
    mu_p = _pack_cols(mu, _AB_CUTS[:3], _AB_WIDTHS[:4])
    mu_p = jnp.concatenate([km3(mu_p[:, :AB_WD]), mu_p[:, AB_WD:]], axis=-1)
    lora = lambda w: jnp.pad(jnp.concatenate([_kmajor(w[0]), _kmajor(w[1])], axis=-1),
                             ((0, LANE - w.shape[1]), (0, 0))).astype(BF16)
    vec = jnp.stack([_kmajor(t) for t in (w0[0], w0[1], a0[0], a0[1], k_k, k_a, r_k, jnp.zeros_like(r_k))])
    a, b0, b1, k0, k1, d0, d1, r, v, g, bonus = rwkv_prep(p, mu_p, lora(w2), lora(a2),
                                                          _kmajor(g2).astype(BF16), vec)
    tile = lambda t: t.reshape(R, RWKV_HEAD, RWKV_HEADS)
    vp = jnp.broadcast_to(tile(v)[:, :, None, :], (R, RWKV_HEAD, 4, RWKV_HEADS)).reshape(R, RWKV_HEAD, LANE)
    y0, y1 = wkv_scan(a, b0, b1, k0, k1, d0, d1, r, vp)
    y = (y0 + y1)[:, :, :RWKV_HEADS]
    mean = jnp.mean(y, axis=1, keepdims=True)
    var = jnp.mean(jnp.square(y - mean), axis=1, keepdims=True)
    gn_t = _kmajor(gn).reshape(2, RWKV_HEAD, RWKV_HEADS)
    yn = (y - mean) * lax.rsqrt(var + RWKV_GN_EPS) * gn_t[0] + gn_t[1]
    out = (yn + bonus[:, None, :RWKV_HEADS] * tile(v)) * tile(g)
    return out.reshape(R, MIX_HALF)


NA_WIN = NA_KH * GRID_W


def _na_kernel(q_ref, k_ref, v_ref, bias_ref, o_ref, *, rows):
    r = pl.program_id(1)
    r0 = jnp.clip(r - NA_KH // 2, 0, rows - NA_KH)
    start = pl.multiple_of(CTX_LEN + r0 * GRID_W, GRID_W)
    q = (q_ref[...] * NA_HEAD_DIM ** -0.5).astype(BF16)
    dn = (((1,), (1,)), ((), ()))
    k_win = k_ref[pl.ds(start, NA_WIN), :].astype(BF16)
    s_loc = lax.dot_general(q, k_win, dn, preferred_element_type=F32) + bias_ref[0, 0]
    s_ctx = lax.dot_general(q, k_ref[pl.ds(0, CTX_LEN), :].astype(BF16), dn, preferred_element_type=F32)
    m = jnp.maximum(jnp.max(s_loc, axis=-1, keepdims=True), jnp.max(s_ctx, axis=-1, keepdims=True))
    p_loc = jnp.exp(s_loc - m)
    p_ctx = jnp.exp(s_ctx - m)
    den = jnp.sum(p_loc, axis=-1, keepdims=True) + jnp.sum(p_ctx, axis=-1, keepdims=True)
    num = jnp.dot(p_loc.astype(BF16), v_ref[pl.ds(start, NA_WIN), :].astype(BF16), preferred_element_type=F32)
    num += jnp.dot(p_ctx.astype(BF16), v_ref[pl.ds(0, CTX_LEN), :].astype(BF16), preferred_element_type=F32)
    o_ref[...] = num / den


def _na_ctx_kernel(q_ref, k_ref, v_ref, o_ref):
    q = (q_ref[...] * NA_HEAD_DIM ** -0.5).astype(BF16)
    s = lax.dot_general(q, k_ref[...].astype(BF16), (((1,), (1,)), ((), ())), preferred_element_type=F32)
    p = jnp.exp(s - jnp.max(s, axis=-1, keepdims=True))
    num = jnp.dot(p.astype(BF16), v_ref[...].astype(BF16), preferred_element_type=F32)
    o_ref[...] = num / jnp.sum(p, axis=-1, keepdims=True)


def _na_bias(rpb, rows):
    kh = min(NA_KH, rows)
    col = np.arange(GRID_W)
    c0 = np.clip(col - NA_KW // 2, 0, GRID_W - NA_KW)
    key = np.arange(GRID_W)
    inside = (key[None, :] >= c0[:, None]) & (key[None, :] < c0[:, None] + NA_KW)
    dc = np.clip(key[None, :] - col[:, None] + (NA_KW - 1), 0, 2 * NA_KW - 2)
    onehot = np.zeros((GRID_W, GRID_W, 2 * NA_KW - 1), np.float32)
    onehot[col[:, None], key[None, :], dc] = 1.0
    toep = jnp.einsum("qcd,hrd->hrqc", onehot, rpb, precision=lax.Precision.HIGHEST)
    toep = jnp.where(jnp.asarray(inside), toep, NEG)
    return jnp.stack([jnp.concatenate([toep[:, i - j + NA_KH - 1] for i in range(kh)], axis=-1)
                      for j in range(NA_KH)], axis=1)


def na_mix(p, rpb):
    R = p.shape[0]
    T = R - CTX_LEN
    rows = T // GRID_W
    qb, kb, vb = [(AB_NA + i * MIX_HALF) // NA_HEAD_DIM for i in range(3)]
    cb = CTX_LEN // GRID_W
    bias = _na_bias(rpb, rows)
    strip = lambda b: pl.BlockSpec((R, NA_HEAD_DIM), lambda h, r: (0, b + h))
    lat = pl.pallas_call(
        functools.partial(_na_kernel, rows=rows),
        grid=(NA_HEADS, rows),
        in_specs=[pl.BlockSpec((GRID_W, NA_HEAD_DIM), lambda h, r: (cb + r, qb + h)),
                  strip(kb), strip(vb),
                  pl.BlockSpec((1, 1, GRID_W, NA_WIN),
                               lambda h, r: (h, r - jnp.clip(r - NA_KH // 2, 0, rows - NA_KH), 0, 0))],
        out_specs=pl.BlockSpec((GRID_W, NA_HEAD_DIM), lambda h, r: (r, h)),
        out_shape=jax.ShapeDtypeStruct((T, MIX_HALF), F32),
        compiler_params=_params(("parallel", "arbitrary")),
        name="na_latent",
    )(p, p, p, bias)
    blk = lambda b: pl.BlockSpec((CTX_LEN, NA_HEAD_DIM), lambda h: (0, b + h))
    ctx = pl.pallas_call(
        _na_ctx_kernel,
        grid=(NA_HEADS,),
        in_specs=[blk(qb), blk(kb), blk(vb)],
        out_specs=pl.BlockSpec((CTX_LEN, NA_HEAD_DIM), lambda h: (0, h)),
        out_shape=jax.ShapeDtypeStruct((CTX_LEN, MIX_HALF), F32),
        compiler_params=_params(("parallel",)),
        name="na_context",
    )(p, p, p)
    return jnp.concatenate([ctx, lat], axis=0)


ML_CHUNK = CTX_LEN
CD_GT = 2 * ML_HEADS * ML_DQK + 2 * MIX_HALF
CD_HY = CD_GT + LANE


def _mlstm_kernel(q_ref, k_ref, v_ref, lir_ref, lfr_ref, lic_ref, lfc_ref, o_ref, c_ref, n_ref, m_ref):
    L = ML_CHUNK

    @pl.when(pl.program_id(2) == 0)
    def _():
        c_ref[...] = jnp.zeros_like(c_ref)
        n_ref[...] = jnp.zeros_like(n_ref)
        m_ref[...] = jnp.zeros_like(m_ref)

    sgn = 1 - 2 * pl.program_id(0)
    row = lax.broadcasted_iota(jnp.int32, (L, L), 0)
    col = lax.broadcasted_iota(jnp.int32, (L, L), 1)
    seen = (row - col) * sgn >= 0
    seen_t = (col - row) * sgn >= 0
    q = q_ref[...]
    k = k_ref[...]
    vb = v_ref[...].astype(BF16)
    li_r, lf_r, li_c, lf_c = lir_ref[0], lfr_ref[0], lic_ref[0], lfc_ref[0]
    b_c = jnp.sum(jnp.where(seen, lf_r, 0.0), axis=1, keepdims=True)
    b_r = jnp.sum(jnp.where(seen_t, lf_c, 0.0), axis=0, keepdims=True)
    m_prev = m_ref[...]
    dmat = jnp.where(seen, b_c - b_r + li_r, NEG)
    inter = b_c + m_prev
    m_t = jnp.maximum(jnp.max(dmat, axis=1, keepdims=True), inter)
    qb = q.astype(BF16)
    s = lax.dot_general(qb, k.astype(BF16), (((1,), (1,)), ((), ())), preferred_element_type=F32)
    s = s * jnp.exp(dmat - m_t)
    dec = jnp.exp(inter - m_t)
    num = jnp.dot(s.astype(BF16), vb, preferred_element_type=F32)
    num += dec * jnp.dot(qb, c_ref[...].astype(BF16), preferred_element_type=F32)
    den = jnp.sum(s, axis=1, keepdims=True) + dec * jnp.sum(q * n_ref[...], axis=1, keepdims=True)
    o_ref[0] = num / jnp.maximum(jnp.abs(den), jnp.exp(-m_t))
    total = jnp.sum(lf_r, axis=1, keepdims=True)
    gl = total - b_c + li_c
    m_new = jnp.maximum(total + m_prev, jnp.max(gl, axis=0, keepdims=True))
    kw = k * jnp.exp(gl - m_new)
    sc = jnp.exp(total + m_prev - m_new)
    c_ref[...] = sc * c_ref[...] + lax.dot_general(kw.astype(BF16), vb, (((0,), (0,)), ((), ())),
                                                   preferred_element_type=F32)
    n_ref[...] = sc * n_ref[...] + jnp.sum(kw, axis=0, keepdims=True)
    m_ref[...] = m_new


def _axial_rope(x):
    T, d = x.shape[0], x.shape[-1]
    half = d // 2
    nf = half // 2
    t = jnp.arange(T)
    row = (t // GRID_W).astype(F32)
    col = (t % GRID_W).astype(F32)
    inv = ROPE_BASE ** (-jnp.arange(nf, dtype=F32) / nf)
    ang = jnp.concatenate([row[:, None] * inv, col[:, None] * inv], axis=-1)[:, None, :]
    cos, sin = jnp.cos(ang), jnp.sin(ang)
    x1, x2 = x[..., :half], x[..., half:]
    return jnp.concatenate([x1 * cos - x2 * sin, x2 * cos + x1 * sin], axis=-1)


def mlstm_mix(p, gate_b, norm_w):
    R = p.shape[0]
    L = ML_CHUNK
    nc = R // L
    QK = ML_HEADS * ML_DQK

    def rope_lat(t):
        th = t.reshape(R, ML_HEADS, ML_DQK)
        return jnp.concatenate([th[:CTX_LEN], _axial_rope(th[CTX_LEN:])], axis=0).reshape(R, QK)

    q = rope_lat(p[:, :QK])
    k = rope_lat(p[:, QK:2 * QK]) * ML_DQK ** -0.5
    o = p[:, 2 * QK + MIX_HALF:CD_GT]
    gt = p[:, CD_GT:CD_GT + 4 * ML_HEADS].reshape(R, 2, 2, ML_HEADS) + gate_b
    log_i = jnp.transpose(gt[:, :, 0], (1, 2, 0))
    log_f = jax.nn.log_sigmoid(jnp.transpose(gt[:, :, 1], (1, 2, 0)))
    as_rows = lambda t: t.reshape(2 * ML_HEADS * nc, 1, L)
    as_cols = lambda t: t.reshape(2 * ML_HEADS * nc, L, 1)

    def chunk(d, c):
        return jnp.where(d == 0, c, jnp.where(c == 0, 0, nc - c))

    gidx = lambda d, h, c: ((d * ML_HEADS + h) * nc + chunk(d, c), 0, 0)
    vb = 2 * QK // ML_DV
    h_dir = pl.pallas_call(
        _mlstm_kernel,
        grid=(2, ML_HEADS, nc),
        in_specs=[pl.BlockSpec((L, ML_DQK), lambda d, h, c: (chunk(d, c), h)),
                  pl.BlockSpec((L, ML_DQK), lambda d, h, c: (chunk(d, c), h)),
                  pl.BlockSpec((L, ML_DV), lambda d, h, c: (chunk(d, c), vb + h)),
                  pl.BlockSpec((1, 1, L), gidx), pl.BlockSpec((1, 1, L), gidx),
                  pl.BlockSpec((1, L, 1), gidx), pl.BlockSpec((1, L, 1), gidx)],
        out_specs=pl.BlockSpec((1, L, ML_DV), lambda d, h, c: (d, chunk(d, c), h)),
        out_shape=jax.ShapeDtypeStruct((2, R, MIX_HALF), F32),
        scratch_shapes=[pltpu.VMEM((ML_DQK, ML_DV), F32), pltpu.VMEM((1, ML_DQK), F32),
                        pltpu.VMEM((1, 1), F32)],
        compiler_params=_params(("parallel", "parallel", "arbitrary")),
        name="mlstm",
    )(q, k, p, as_rows(log_i), as_rows(log_f), as_cols(log_i), as_cols(log_f))
    h = (h_dir[0] + h_dir[1]).reshape(R, ML_HEADS, ML_DV)
    h = h * lax.rsqrt(jnp.mean(h * h, axis=-1, keepdims=True) + RMS_EPS)
    return h.reshape(R, MIX_HALF) * norm_w * jax.nn.sigmoid(o)


def _pad_to(x, axis, mult):
    n = -x.shape[axis] % mult
    if n == 0:
        return x
    pad = [(0, 0)] * x.ndim
    pad[axis] = (0, n)
    return jnp.pad(x, pad)


def mm3p(a, b):
    M, N = a.shape[0], b.shape[1]
    a = _pad_to(_pad_to(a, 1, LANE), 0, 8)
    b = _pad_to(_pad_to(b, 0, LANE), 1, LANE)
    return mm3(a, b)[:M, :N]


def hyena_filters(T, ff1, ff1_b, ff2, ff2_b, ff3, ff3_b, ff_out, freq, decay):
    t = jnp.linspace(0.0, 1.0, T, dtype=F32)[:, None]
    bands = (HY_EMB - 1) // 2
    f = jnp.linspace(1e-4, bands - 1, bands, dtype=F32)[None, :]
    w = (2 * math.pi / T) * jnp.arange(T, dtype=F32)[:, None]
    z = jnp.concatenate([t, jnp.cos(f * w), -jnp.sin(f * w)], axis=-1)
    h = jnp.sin(freq * (mm3p(z, ff1) + ff1_b))
    h = jnp.sin(freq * (mm3p(h, ff2) + ff2_b))
    h = jnp.sin(freq * (mm3p(h, ff3) + ff3_b))
    h = mm3p(h, ff_out) * jnp.exp(-t * jnp.abs(decay))
    return h.reshape(T, HY_ORDER, 2, MIX_HALF)


FFT_N2 = LANE


@functools.lru_cache(maxsize=None)
def _dft_tables(n1):
    n2 = FFT_N2
    n = n1 * n2
    i1, i2 = np.arange(n1), np.arange(n2)
    a1 = 2 * np.pi * np.outer(i1, i1) / n1
    a2 = 2 * np.pi * np.outer(i2, i2) / n2
    c1, s1 = np.cos(a1), -np.sin(a1)
    c2, s2 = np.cos(a2), -np.sin(a2)
    atw = 2 * np.pi * np.outer(i2, i1) / n
    f32 = lambda x: np.asarray(x, np.float32)
    return dict(
        wa=f32(np.concatenate([c1, s1], axis=1)),
        wb=f32(np.block([[c2, s2], [-s2, c2]])),
        wb_inv=f32(np.block([[c2, -s2], [s2, c2]])),
        wa_inv=f32(np.concatenate([c1, s1], axis=0)[:, :n1 // 2] / n),
        twr=f32(np.cos(atw)), twi=f32(-np.sin(atw)))


def _fft_fwd(x_t, n1):
    C = x_t.shape[0]
    n2 = FFT_N2
    tb = _dft_tables(n1)
    xa = jnp.transpose(x_t.reshape(C, n1, n2), (0, 2, 1)).reshape(C * n2, n1)
    a = mm3p(xa, tb["wa"]).reshape(C, n2, 2, n1)
    ar, ai = a[:, :, 0], a[:, :, 1]
    br = ar * tb["twr"] - ai * tb["twi"]
    bi = ar * tb["twi"] + ai * tb["twr"]
    b = jnp.concatenate([jnp.transpose(br, (0, 2, 1)), jnp.transpose(bi, (0, 2, 1))], axis=-1)
    return mm3(b.reshape(C * n1, 2 * n2), tb["wb"])


def _fft_inv(y, C, n1):
    n2 = FFT_N2
    tb = _dft_tables(n1)
    g = mm3(y, tb["wb_inv"]).reshape(C, n1, 2, n2)
    gr, gi = g[:, :, 0], g[:, :, 1]
    twr, twi = tb["twr"].T, tb["twi"].T
    hr = gr * twr + gi * twi
    hi = gi * twr - gr * twi
    h = jnp.concatenate([jnp.transpose(hr, (0, 2, 1)), jnp.transpose(hi, (0, 2, 1))], axis=-1)
    out = mm3p(h.reshape(C * n2, 2 * n1), tb["wa_inv"])
    return jnp.transpose(out.reshape(C, n2, n1 // 2), (0, 2, 1)).reshape(C, n1 // 2 * n2)


def _long_conv(z, h_fwd, h_bwd):
    T, C = z.shape
    n1 = 2 * T // FFT_N2
    f = jnp.concatenate([h_fwd, jnp.zeros((1, C), F32), h_bwd[:0:-1]], axis=0)
    zs = _fft_fwd(jnp.pad(z.T, ((0, 0), (0, T))), n1).reshape(C * n1, 2, FFT_N2)
    fs = _fft_fwd(f.T, n1).reshape(C * n1, 2, FFT_N2)
    yr = zs[:, 0] * fs[:, 0] - zs[:, 1] * fs[:, 1]
    yi = zs[:, 0] * fs[:, 1] + zs[:, 1] * fs[:, 0]
    return _fft_inv(jnp.concatenate([yr, yi], axis=-1), C, n1).T


HY_Q = 4
HY_F1Q = FFT_N2 // HY_Q
HY_QR = HY_F1Q * FFT_N2
HY_NN = FFT_N2 * FFT_N2


@functools.lru_cache(maxsize=None)
def _hy_tables():
    n = FFT_N2
    i = np.arange(n)
    w = np.exp(-2j * np.pi * np.outer(i, i) / n)
    tw = np.exp(-2j * np.pi * np.outer(i, i) / (n * n))
    ca = np.conj(w)[: n // 2] / (n * n)
    f32 = lambda x: np.ascontiguousarray(x, dtype=np.float32)
    return dict(
        wa=f32(np.concatenate([w.real, w.imag], axis=0)),
        wbr=f32(w.real), wbi=f32(w.imag),
        twr=f32(tw.real).reshape(n, 1, n), twi=f32(tw.imag).reshape(n, 1, n),
        ctr=f32(tw.real.T).reshape(n, 1, n), cti=f32(-tw.imag.T).reshape(n, 1, n),
        car=f32(ca.real), cai=f32(ca.imag))


def _dot3(a_hi, a_lo, b):
    b_hi, b_lo = _split(b)
    acc = jnp.dot(a_hi, b_hi, preferred_element_type=F32)
    acc += jnp.dot(a_hi, b_lo, preferred_element_type=F32)
    return acc + jnp.dot(a_lo, b_hi, preferred_element_type=F32)


def _hy_fwd_kernel(x_ref, wa_ref, wbr_ref, wbi_ref, twr_ref, twi_ref, o_ref, a_ref, *, k1):
    n = FFT_N2
    q = pl.program_id(1)

    @pl.when(q == 0)
    def _():
        wa_hi, wa_lo = _split(wa_ref[...])
        for t2 in range(n):
            y = _dot3(wa_hi, wa_lo, x_ref[pl.ds(t2, k1, stride=n), :])
            a_ref[pl.ds(t2, n, stride=n), :] = y[:n]
            a_ref[pl.ds(HY_NN + t2, n, stride=n), :] = y[n:]

    cr, ci = wbr_ref[...], wbi_ref[...]

    def body(j, carry):
        f1 = q * HY_F1Q + j
        base = pl.multiple_of(f1 * n, n)
        rhs = jnp.concatenate([a_ref[pl.ds(base, n), :], a_ref[pl.ds(HY_NN + base, n), :]], axis=0)
        twr, twi = twr_ref[f1], twi_ref[f1]
        pr = cr * twr - ci * twi
        pi = cr * twi + ci * twr
        lhs = jnp.concatenate([jnp.concatenate([pr, -pi], axis=1), jnp.concatenate([pi, pr], axis=1)], axis=0)
        out = _dot3(*_split(lhs), rhs)
        ob = pl.multiple_of(j * n, n)
        o_ref[pl.ds(ob, n), :] = out[:n]
        o_ref[pl.ds(HY_QR + ob, n), :] = out[n:]
        return carry

    lax.fori_loop(0, HY_F1Q, body, 0)


def hy_fwd(x):
    T_in, C = x.shape
    k1 = T_in // FFT_N2
    tb = _hy_tables()
    const = lambda a: pl.BlockSpec(a.shape, lambda c, q: (0,) * a.ndim)
    wa = tb["wa"][:, :k1]
    args = (wa, tb["wbr"], tb["wbi"], tb["twr"], tb["twi"])
    return pl.pallas_call(
        functools.partial(_hy_fwd_kernel, k1=k1),
        grid=(C // LANE, HY_Q),
        in_specs=[pl.BlockSpec((T_in, LANE), lambda c, q: (0, c))] + [const(a) for a in args],
        out_specs=pl.BlockSpec((None, 2 * HY_QR, LANE), lambda c, q: (q, 0, c)),
        out_shape=jax.ShapeDtypeStruct((HY_Q, 2 * HY_QR, C), F32),
        scratch_shapes=[pltpu.VMEM((2 * HY_NN, LANE), F32)],
        compiler_params=_params(("parallel", "arbitrary")),
        name="hy_fwd",
    )(x, *args)


def _hy_conv_kernel(z_ref, f_ref, wbr_ref, wbi_ref, ctr_ref, cti_ref, car_ref, cai_ref, y_ref, g_ref):
    n = FFT_N2
    q = pl.program_id(1)
    cr, ci = wbr_ref[...], wbi_ref[...]
    l_hi, l_lo = _split(jnp.concatenate([jnp.concatenate([cr, ci], axis=1),
                                         jnp.concatenate([-ci, cr], axis=1)], axis=0))

    def body(j, carry):
        ob = pl.multiple_of(j * n, n)
        zr, zi = z_ref[pl.ds(ob, n), :], z_ref[pl.ds(HY_QR + ob, n), :]
        fr, fi = f_ref[pl.ds(ob, n), :], f_ref[pl.ds(HY_QR + ob, n), :]
        prod = jnp.concatenate([zr * fr - zi * fi, zr * fi + zi * fr], axis=0)
        out = _dot3(l_hi, l_lo, prod)
        base = pl.multiple_of((q * HY_F1Q + j) * n, n)
        g_ref[pl.ds(base, n), :] = out[:n]
        g_ref[pl.ds(HY_NN + base, n), :] = out[n:]
        return carry

    lax.fori_loop(0, HY_F1Q, body, 0)

    @pl.when(q == HY_Q - 1)
    def _():
        car, cai = car_ref[...], cai_ref[...]
        for t2 in range(n):
            ctr, cti = ctr_ref[t2], cti_ref[t2]
            er = car * ctr - cai * cti
            ei = car * cti + cai * ctr
            rhs = jnp.concatenate([g_ref[pl.ds(t2, n, stride=n), :],
                                   g_ref[pl.ds(HY_NN + t2, n, stride=n), :]], axis=0)
            y_ref[pl.ds(t2, n // 2, stride=n), :] = _dot3(*_split(jnp.concatenate([er, -ei], axis=1)), rhs)


def hy_conv(zs, fs, f_block=0):
    C = zs.shape[-1]
    T = HY_NN // 2
    tb = _hy_tables()
    const = lambda a: pl.BlockSpec(a.shape, lambda c, q: (0,) * a.ndim)
    spec = pl.BlockSpec((None, 2 * HY_QR, LANE), lambda c, q: (q, 0, c))
    fspec = pl.BlockSpec((None, 2 * HY_QR, LANE), lambda c, q: (q, 0, c + f_block))
    args = (tb["wbr"], tb["wbi"], tb["ctr"], tb["cti"], tb["car"], tb["cai"])
    return pl.pallas_call(
        _hy_conv_kernel,
        grid=(C // LANE, HY_Q),
        in_specs=[spec, fspec] + [const(a) for a in args],
        out_specs=pl.BlockSpec((T, LANE), lambda c, q: (0, c)),
        out_shape=jax.ShapeDtypeStruct((T, C), F32),
        scratch_shapes=[pltpu.VMEM((2 * HY_NN, LANE), F32)],
        compiler_params=_params(("parallel", "arbitrary")),
        name="hy_conv",
    )(zs, fs, *args)


def hyena_filter_taps(T, ff1, ff1_b, ff2, ff2_b, ff3, ff3_b, ff_out, freq, decay):
    t = jnp.linspace(0.0, 1.0, T, dtype=F32)[:, None]
    bands = (HY_EMB - 1) // 2
    f = jnp.linspace(1e-4, bands - 1, bands, dtype=F32)[None, :]
    w = (2 * math.pi / T) * jnp.arange(T, dtype=F32)[:, None]
    z = jnp.concatenate([t, jnp.cos(f * w), -jnp.sin(f * w)], axis=-1)
    h = jnp.sin(freq * (mm3p(z, ff1) + ff1_b))
    h = jnp.sin(freq * (mm3p(h, ff2) + ff2_b))
    h = jnp.sin(freq * (mm3p(h, ff3) + ff3_b))
    w_out = ff_out.reshape(-1, HY_ORDER, 2, MIX_HALF)
    dec = jnp.abs(decay).reshape(HY_ORDER, 2, MIX_HALF)
    side = lambda s: (w_out[:, :, s].reshape(-1, HY_ORDER * MIX_HALF), dec[:, s].reshape(HY_ORDER * MIX_HALF))
    (w_f, d_f), (w_b, d_b) = side(0), side(1)
    fwd = mm3p(h, w_f) * jnp.exp(-t * d_f)
    bwd = mm3p(h[::-1], w_b) * jnp.exp(-t[::-1] * d_b)
    return jnp.concatenate([fwd, jnp.zeros((1, fwd.shape[1]), F32), bwd[:T - 1]], axis=0)


def hyena_seq(p, short_w, short_b, conv, bias):
    u = short_w[0] * _shift_prev(p) + short_w[1] * p + short_w[2] * _shift_next(p) + short_b
    z, x1, x2 = jnp.split(u, 3, axis=-1)
    for n, gate in enumerate((x1, x2)):
        z = gate * (conv(z, n) + bias[n] * z)
    return z


def hyena_mix(p, short_w, short_b, filt_params, bias):
    T = p.shape[0] - CTX_LEN
    filt_c = hyena_filters(CTX_LEN, *filt_params)
    out_c = hyena_seq(p[:CTX_LEN], short_w, short_b,
                      lambda z, n: _long_conv(z, filt_c[:, n, 0], filt_c[:, n, 1]), bias)
    if 2 * T == HY_NN:
        spectra = hy_fwd(hyena_filter_taps(T, *filt_params))
        conv = lambda z, n: hy_conv(hy_fwd(z), spectra, n * MIX_HALF // LANE)
    else:
        filt = hyena_filters(T, *filt_params)
        conv = lambda z, n: _long_conv(z, filt[:, n, 0], filt[:, n, 1])
    out = hyena_seq(p[CTX_LEN:], short_w, short_b, conv, bias)
    return jnp.concatenate([out_c, out], axis=0)


def ffn_half(x, mod, g_pre, g_post, w1, w3, w2):
    h = norm_mod(x, g_pre, mod[:, 0], mod[:, 1])
    a = swiglu(h, w1.astype(BF16), w3.astype(BF16))
    return post_res(x, mm(a, w2), g_post, mod[:, 2], FFN_RES)


def kernel(x, c, ctx, c_ctx, ada_down, ada_up, ada_bias, norm_g, ffn_w1, ffn_w3, ffn_w2, ab_w_in, ab_w_out, rwkv_mu, rwkv_w0, rwkv_w2, rwkv_a0, rwkv_a2, rwkv_g2, rwkv_kk, rwkv_ka, rwkv_rk, rwkv_gn, na_rpb, cd_w_in, cd_w_out, ml_gate_b, ml_norm, hy_short_w, hy_short_b, hy_ff1, hy_ff1_b, hy_ff2, hy_ff2_b, hy_ff3, hy_ff3_b, hy_ff_out, hy_freq, hy_decay, hy_bias):
    assert x.shape[0] == 1 and ctx.shape[1] == CTX_LEN
    X = jnp.concatenate([ctx[0], x[0]], axis=0)
    s2 = _pad_to(jnp.stack([jax.nn.silu(c_ctx), jax.nn.silu(c[0])]), 0, 16)
    for l in range(DEPTH):
        i = l // 2
        mod = (mm(mm(s2, ada_down[l]), ada_up[l])[:2] + ada_bias[l]).reshape(2, N_SUB, 3, D_MODEL)
        g = norm_g[l]
        X = ffn_half(X, mod[:, 0], g[0], g[1], ffn_w1[l, 0], ffn_w3[l, 0], ffn_w2[l, 0])
        xn = norm_mod(X, g[2], mod[:, 1, 0], mod[:, 1, 1])
        if l % 2 == 0:
            w_ab = ab_w_in[i].astype(BF16)
            w_rkv = [_kmajor(t) for t in jnp.split(w_ab[:, :AB_WD], 3, axis=-1)]
            p = mm(xn, _pack_cols(jnp.concatenate(w_rkv + [w_ab[:, AB_WD:]], axis=-1), _AB_CUTS, _AB_WIDTHS))
            ya = rwkv_mix(p, rwkv_mu[i], rwkv_w0[i], rwkv_w2[i], rwkv_a0[i], rwkv_a2[i],
                          rwkv_g2[i], rwkv_kk[i], rwkv_ka[i], rwkv_rk[i], rwkv_gn[i])
            y = jnp.concatenate([ya, na_mix(p, na_rpb[i])], axis=-1)
            w_out = ab_w_out[i].astype(BF16)
            w_rw = jnp.swapaxes(w_out[:MIX_HALF].reshape(RWKV_HEADS, RWKV_HEAD, D_MODEL), 0, 1)
            w_out = jnp.concatenate([w_rw.reshape(MIX_HALF, D_MODEL), w_out[MIX_HALF:]], axis=0)
        else:
            w_in = _pack_cols(cd_w_in[i], (CD_GT, ML_COLS), (CD_GT, LANE, P_COLS - CD_HY))
            p = mm(xn, w_in)
            filt_params = (hy_ff1[i], hy_ff1_b[i], hy_ff2[i], hy_ff2_b[i], hy_ff3[i], hy_ff3_b[i],
                           hy_ff_out[i], hy_freq[i], hy_decay[i])
            yd = hyena_mix(p[:, CD_HY:CD_HY + 3 * MIX_HALF], hy_short_w[i], hy_short_b[i], filt_params,
                           hy_bias[i])
            y = jnp.concatenate([mlstm_mix(p, ml_gate_b[i], ml_norm[i]), yd], axis=-1)
            w_out = cd_w_out[i]
        X = post_res(X, mm(y, w_out), g[3], mod[:, 1, 2], 1.0)
        X = ffn_half(X, mod[:, 2], g[4], g[5], ffn_w1[l, 1], ffn_w3[l, 1], ffn_w2[l, 1])
    return X[CTX_LEN:][None]
```

```python
import functools
import math

import jax
import jax.numpy as jnp
import numpy as np
from jax import lax
from jax.experimental import pallas as pl
from jax.experimental.pallas import tpu as pltpu

F32 = jnp.float32
BF16 = jnp.bfloat16

D_MODEL = 4096
DEPTH = 4
GRID_W = 64
CTX_LEN = 256
N_SUB = 3
FFN_RES = 0.5
D_FF = 5632
RMS_EPS = 1e-6
MIX_HALF = D_MODEL // 2

RWKV_HEAD = 64
RWKV_HEADS = MIX_HALF // RWKV_HEAD
RWKV_DECAY_LORA = 96
RWKV_A_LORA = 96
RWKV_GATE_LORA = 256
RWKV_COLS = 3 * MIX_HALF + RWKV_DECAY_LORA + RWKV_A_LORA + RWKV_GATE_LORA
RWKV_GN_EPS = 64e-5

NA_HEAD_DIM = 128
NA_HEADS = MIX_HALF // NA_HEAD_DIM
NA_KH = 8
NA_KW = 16

ML_HEADS = 4
ML_DV = MIX_HALF // ML_HEADS
ML_DQK = ML_DV // 2
ML_COLS = 2 * ML_HEADS * ML_DQK + 2 * MIX_HALF + 4 * ML_HEADS
ROPE_BASE = 10000.0

HY_ORDER = 2
HY_EMB = 33

LANE = 128
ROW_TILE = CTX_LEN
VMEM_LIMIT = 56 * 1024 * 1024
P_COLS = 12800
NEG = -1e30


def _params(sem):
    return pltpu.CompilerParams(dimension_semantics=sem, vmem_limit_bytes=VMEM_LIMIT)


def _pick(n, cands):
    for c in cands:
        if n % c == 0:
            return c
    return n


def _mm_kernel(a_ref, b_ref, o_ref):
    o_ref[...] = jnp.dot(a_ref[...], b_ref[...], preferred_element_type=F32).astype(o_ref.dtype)


def _split(x):
    hi = x.astype(BF16)
    lo = (x - hi.astype(F32)).astype(BF16)
    return hi, lo


def _mm3_kernel(a_ref, b_ref, o_ref):
    a_hi, a_lo = _split(a_ref[...])
    b_hi, b_lo = _split(b_ref[...])
    acc = jnp.dot(a_hi, b_hi, preferred_element_type=F32)
    acc += jnp.dot(a_hi, b_lo, preferred_element_type=F32)
    acc += jnp.dot(a_lo, b_hi, preferred_element_type=F32)
    o_ref[...] = acc


def _mm_call(kern, a, b, out_dtype, name):
    M, K = a.shape
    N = b.shape[1]
    tm = _pick(M, (1024, 768, 512, 256, 128, 64, 32, 16, 8))
    tn = _pick(N, (512, 256, 128))
    return pl.pallas_call(
        kern,
        grid=(M // tm, N // tn),
        in_specs=[pl.BlockSpec((tm, K), lambda i, j: (i, 0)),
                  pl.BlockSpec((K, tn), lambda i, j: (0, j))],
        out_specs=pl.BlockSpec((tm, tn), lambda i, j: (i, j)),
        out_shape=jax.ShapeDtypeStruct((M, N), out_dtype),
        compiler_params=_params(("parallel", "parallel")),
        name=name,
    )(a, b)


def mm(a, b, out_dtype=F32):
    return _mm_call(_mm_kernel, a.astype(BF16), b.astype(BF16), out_dtype, "mm")


def mm3(a, b):
    return _mm_call(_mm3_kernel, a.astype(F32), b.astype(F32), F32, "mm3")


def _swiglu_kernel(a_ref, w1_ref, w3_ref, o_ref):
    a = a_ref[...]
    h1 = jnp.dot(a, w1_ref[...], preferred_element_type=F32)
    h3 = jnp.dot(a, w3_ref[...], preferred_element_type=F32)
    o_ref[...] = (h1 * jax.nn.sigmoid(h1) * h3).astype(o_ref.dtype)


def swiglu(a, w1, w3):
    M, K = a.shape
    N = w1.shape[1]
    tm = _pick(M, (768, 512, 256))
    tn = _pick(N, (512, 256, 128))
    wspec = pl.BlockSpec((K, tn), lambda i, j: (0, j))
    return pl.pallas_call(
        _swiglu_kernel,
        grid=(M // tm, N // tn),
        in_specs=[pl.BlockSpec((tm, K), lambda i, j: (i, 0)), wspec, wspec],
        out_specs=pl.BlockSpec((tm, tn), lambda i, j: (i, j)),
        out_shape=jax.ShapeDtypeStruct((M, N), BF16),
        compiler_params=_params(("parallel", "parallel")),
        name="swiglu",
    )(a, w1, w3)


def _group_spec(d):
    return pl.BlockSpec((1, 1, d), lambda i: (jnp.minimum(i, 1), 0, 0))


def _norm_mod_kernel(x_ref, g_ref, sh_ref, sc_ref, o_ref):
    x = x_ref[...]
    y = x * lax.rsqrt(jnp.mean(x * x, axis=-1, keepdims=True) + RMS_EPS) * g_ref[...]
    o_ref[...] = (y * (1 + sc_ref[0]) + sh_ref[0]).astype(o_ref.dtype)


def norm_mod(x, g, shift, scale):
    R, D = x.shape
    row = pl.BlockSpec((ROW_TILE, D), lambda i: (i, 0))
    return pl.pallas_call(
        _norm_mod_kernel,
        grid=(R // ROW_TILE,),
        in_specs=[row, pl.BlockSpec((1, D), lambda i: (0, 0)), _group_spec(D), _group_spec(D)],
        out_specs=row,
        out_shape=jax.ShapeDtypeStruct((R, D), BF16),
        compiler_params=_params(("parallel",)),
        name="norm_mod",
    )(x, g.reshape(1, D), shift.reshape(2, 1, D), scale.reshape(2, 1, D))


def _post_res_kernel(x_ref, y_ref, g_ref, gate_ref, o_ref, *, coef):
    y = y_ref[...]
    yn = y * lax.rsqrt(jnp.mean(y * y, axis=-1, keepdims=True) + RMS_EPS) * g_ref[...]
    o_ref[...] = x_ref[...] + coef * gate_ref[0] * yn


def post_res(x, y, g, gate, coef):
    R, D = x.shape
    row = pl.BlockSpec((ROW_TILE, D), lambda i: (i, 0))
    return pl.pallas_call(
        functools.partial(_post_res_kernel, coef=coef),
        grid=(R // ROW_TILE,),
        in_specs=[row, row, pl.BlockSpec((1, D), lambda i: (0, 0)), _group_spec(D)],
        out_specs=row,
        out_shape=jax.ShapeDtypeStruct((R, D), F32),
        compiler_params=_params(("parallel",)),
        name="post_res",
    )(x, y, g.reshape(1, D), gate.reshape(2, 1, D))


WKV_TB = 8
WKV_KG = MIX_HALF // LANE
WKV_VC = RWKV_HEAD // 8
PREP_TB = 128


def _kmajor(w):
    lead = w.shape[:-1]
    return jnp.swapaxes(w.reshape(*lead, RWKV_HEADS, RWKV_HEAD), -1, -2).reshape(*lead, MIX_HALF)


def _fold_lanes(s):
    s = s + pltpu.roll(s, 64, 1)
    return s + pltpu.roll(s, 32, 1)


def _rwkv_prep_kernel(p_ref, hp_ref, hn_ref, mu_ref, w2_ref, a2_ref, g2_ref, vec_ref,
                      a_out, b0_out, b1_out, k0_out, k1_out, w0_out, w1_out, r_out, v_out, g_out, bonus_out):
    W = MIX_HALF
    i = pl.program_id(0)
    cb = CTX_LEN // PREP_TB
    has_prev = jnp.logical_and(i != 0, i != cb).astype(F32)
    has_next = jnp.logical_and(i != cb - 1, i != pl.num_programs(0) - 1).astype(F32)
    row = lax.broadcasted_iota(jnp.int32, (PREP_TB, 1), 0)

    def shifted(c0, width):
        x = p_ref[:, c0:c0 + width]
        up = jnp.where(row == 0, hp_ref[7:8, c0:c0 + width] * has_prev, pltpu.roll(x, 1, 0))
        dn = jnp.where(row == PREP_TB - 1, hn_ref[0:1, c0:c0 + width] * has_next,
                       pltpu.roll(x, PREP_TB - 1, 0))
        return x + mu_ref[0:1, c0:c0 + width] * (up - x) + mu_ref[1:2, c0:c0 + width] * (dn - x)

    lo = shifted(AB_WD, AB_NA - AB_WD)
    wd, ad, gd = lo[:, :LANE], lo[:, LANE:2 * LANE], lo[:, 2 * LANE:]
    wlo = jnp.dot(jnp.tanh(wd).astype(BF16), w2_ref[...], preferred_element_type=F32)
    alo = jnp.dot(ad.astype(BF16), a2_ref[...], preferred_element_type=F32)
    g_out[...] = jnp.dot(jax.nn.sigmoid(gd).astype(BF16), g2_ref[...], preferred_element_type=F32)

    ss = None
    for g in range(WKV_KG):
        sl = slice(g * LANE, (g + 1) * LANE)
        kk = shifted(W + g * LANE, LANE) * vec_ref[4:5, sl]
        ss = kk * kk if ss is None else ss + kk * kk
    nrm = jnp.maximum(jnp.sqrt(_fold_lanes(ss)), 1e-12)
    bonus = None
    for g in range(WKV_KG):
        sl = slice(g * LANE, (g + 1) * LANE)
        r = shifted(g * LANE, LANE)
        k = shifted(W + g * LANE, LANE)
        r_out[:, sl] = r
        v_out[:, sl] = shifted(2 * W + g * LANE, LANE)
        kk = k * vec_ref[4:5, sl] / nrm
        a_out[:, sl] = -kk
        for d, (w_out, k_out, b_out) in enumerate(((w0_out, k0_out, b0_out), (w1_out, k1_out, b1_out))):
            dl = slice(d * W + g * LANE, d * W + (g + 1) * LANE)
            logw = -jax.nn.softplus(-(vec_ref[d:d + 1, sl] + wlo[:, dl])) - 0.5
            w_out[:, sl] = jnp.exp(-jnp.exp(logw))
            a = jax.nn.sigmoid(vec_ref[2 + d:3 + d, sl] + alo[:, dl])
            kd = k * (1 + (a - 1) * vec_ref[5:6, sl])
            k_out[:, sl] = kd
            b_out[:, sl] = kk * a
            term = r * kd * vec_ref[6:7, sl]
            bonus = term if bonus is None else bonus + term
    bonus_out[...] = _fold_lanes(bonus)


def rwkv_prep(p, mu, w2c, a2c, g2, vec):
    R = p.shape[0]
    W = MIX_HALF
    hb = PREP_TB // 8
    full = lambda a: pl.BlockSpec(a.shape, lambda i: (0,) * a.ndim)
    wide = pl.BlockSpec((PREP_TB, W), lambda i: (i, 0))
    return pl.pallas_call(
        _rwkv_prep_kernel,
        grid=(R // PREP_TB,),
        in_specs=[pl.BlockSpec((PREP_TB, AB_NA), lambda i: (i, 0)),
                  pl.BlockSpec((8, AB_NA), lambda i: (jnp.maximum(i * hb - 1, 0), 0)),
                  pl.BlockSpec((8, AB_NA), lambda i: (jnp.minimum((i + 1) * hb, R // 8 - 1), 0)),
                  full(mu), full(w2c), full(a2c), full(g2), full(vec)],
        out_specs=[wide] * 10 + [pl.BlockSpec((PREP_TB, LANE), lambda i: (i, 0))],
        out_shape=[jax.ShapeDtypeStruct((R, W), F32)] * 10 + [jax.ShapeDtypeStruct((R, LANE), F32)],
        compiler_params=_params(("parallel",)),
        name="rwkv_prep",
    )(p, p, p, mu, w2c, a2c, g2, vec)


def _wkv_kernel(a0_ref, b0_ref, k0_ref, w0_ref, r0_ref, v0_ref, a1_ref, b1_ref, k1_ref, w1_ref, r1_ref, v1_ref,
                y0_ref, y1_ref, s_ref):
    @pl.when(pl.program_id(0) == 0)
    def _():
        s_ref[...] = jnp.zeros_like(s_ref)

    dirs = ((a0_ref, b0_ref, k0_ref, w0_ref, r0_ref, v0_ref, y0_ref),
            (a1_ref, b1_ref, k1_ref, w1_ref, r1_ref, v1_ref, y1_ref))

    for t in range(WKV_TB):
        for d, (a_ref, b_ref, k_ref, w_ref, r_ref, v_ref, y_ref) in enumerate(dirs):
            tt = t if d == 0 else WKV_TB - 1 - t
            row = lambda ref, g: ref[tt:tt + 1, g * LANE:(g + 1) * LANE]
            acc = [None] * WKV_VC
            for g in range(WKV_KG):
                ar = row(a_ref, g)
                for c in range(WKV_VC):
                    term = s_ref[d, g, pl.ds(c * 8, 8), :] * ar
                    acc[c] = term if acc[c] is None else acc[c] + term
            sa = [_fold_lanes(x) for x in acc]
            vv = [v_ref[tt, pl.ds(c * 8, 8), :] for c in range(WKV_VC)]
            acc = [None] * WKV_VC
            for g in range(WKV_KG):
                wr, br, kr, rr = row(w_ref, g), row(b_ref, g), row(k_ref, g), row(r_ref, g)
                for c in range(WKV_VC):
                    s_new = s_ref[d, g, pl.ds(c * 8, 8), :] * wr + sa[c] * br + vv[c] * kr
                    s_ref[d, g, pl.ds(c * 8, 8), :] = s_new
                    term = s_new * rr
                    acc[c] = term if acc[c] is None else acc[c] + term
            for c in range(WKV_VC):
                y_ref[tt, pl.ds(c * 8, 8), :] = _fold_lanes(acc[c])


def wkv_scan(a, b0, b1, k0, k1, w0, w1, r, vp):
    R = a.shape[0]
    nb = R // WKV_TB
    cb = CTX_LEN // WKV_TB
    fwd = lambda i: i
    bwd = lambda i: jnp.where(i < cb, cb - 1 - i, nb + cb - 1 - i)
    kspec = lambda f: pl.BlockSpec((WKV_TB, MIX_HALF), lambda i: (f(i), 0))
    vspec = lambda f: pl.BlockSpec((WKV_TB, RWKV_HEAD, LANE), lambda i: (f(i), 0, 0))
    yshape = jax.ShapeDtypeStruct((R, RWKV_HEAD, LANE), F32)
    return pl.pallas_call(
        _wkv_kernel,
        grid=(nb,),
        in_specs=[kspec(fwd)] * 5 + [vspec(fwd)] + [kspec(bwd)] * 5 + [vspec(bwd)],
        out_specs=[vspec(fwd), vspec(bwd)],
        out_shape=[yshape, yshape],
        scratch_shapes=[pltpu.VMEM((2, WKV_KG, RWKV_HEAD, LANE), F32)],
        compiler_params=_params(("arbitrary",)),
        name="wkv_scan",
    )(a, b0, k0, w0, r, vp, a, b1, k1, w1, r, vp)


def _shift_prev(y):
    return jnp.pad(y, ((1, 0), (0, 0)))[:-1]


def _shift_next(y):
    return jnp.pad(y, ((0, 1), (0, 0)))[1:]


AB_WD = 3 * MIX_HALF
AB_AD = AB_WD + LANE
AB_GD = AB_AD + LANE
AB_NA = AB_GD + RWKV_GATE_LORA


def _pack_cols(w, cuts, widths):
    pieces = jnp.split(w, cuts, axis=-1)
    pad = [(0, 0)] * (w.ndim - 1)
    return jnp.concatenate([jnp.pad(p, pad + [(0, wd - p.shape[-1])]) for p, wd in zip(pieces, widths)],
                           axis=-1)


_AB_CUTS = (AB_WD, AB_WD + RWKV_DECAY_LORA, AB_WD + RWKV_DECAY_LORA + RWKV_A_LORA, RWKV_COLS)
_AB_WIDTHS = (AB_WD, LANE, LANE, RWKV_GATE_LORA, 3 * MIX_HALF)


def rwkv_mix(p, mu, w0, w2, a0, a2, g2, k_k, k_a, r_k, gn):
    R = p.shape[0]
    km3 = lambda t: jnp.concatenate([_kmajor(x) for x in jnp.split(t, 3, axis=-1)], axis=-1)
    mu_p = _pack_cols(mu, _AB_CUTS[:3], _AB_WIDTHS[:4])
    mu_p = jnp.concatenate([km3(mu_p[:, :AB_WD]), mu_p[:, AB_WD:]], axis=-1)
    lora = lambda w: jnp.pad(jnp.concatenate([_kmajor(w[0]), _kmajor(w[1])], axis=-1),
                             ((0, LANE - w.shape[1]), (0, 0))).astype(BF16)
    vec = jnp.stack([_kmajor(t) for t in (w0[0], w0[1], a0[0], a0[1], k_k, k_a, r_k, jnp.zeros_like(r_k))])
    a, b0, b1, k0, k1, d0, d1, r, v, g, bonus = rwkv_prep(p, mu_p, lora(w2), lora(a2),
                                                          _kmajor(g2).astype(BF16), vec)
    tile = lambda t: t.reshape(R, RWKV_HEAD, RWKV_HEADS)
    vp = jnp.broadcast_to(tile(v)[:, :, None, :], (R, RWKV_HEAD, 4, RWKV_HEADS)).reshape(R, RWKV_HEAD, LANE)
    y0, y1 = wkv_scan(a, b0, b1, k0, k1, d0, d1, r, vp)
    y = (y0 + y1)[:, :, :RWKV_HEADS]
    mean = jnp.mean(y, axis=1, keepdims=True)
    var = jnp.mean(jnp.square(y - mean), axis=1, keepdims=True)
    gn_t = _kmajor(gn).reshape(2, RWKV_HEAD, RWKV_HEADS)
    yn = (y - mean) * lax.rsqrt(var + RWKV_GN_EPS) * gn_t[0] + gn_t[1]
    out = (yn + bonus[:, None, :RWKV_HEADS] * tile(v)) * tile(g)
    return out.reshape(R, MIX_HALF)


NA_WIN = NA_KH * GRID_W


NA_RB = 4


def _na_kernel(q_ref, k_ref, v_ref, bias_ref, o_ref, *, rows):
    dn = (((1,), (1,)), ((), ()))
    k_ctx = k_ref[pl.ds(0, CTX_LEN), :].astype(BF16)
    v_ctx = v_ref[pl.ds(0, CTX_LEN), :].astype(BF16)
    for j in range(NA_RB):
        r = pl.program_id(1) * NA_RB + j
        r0 = jnp.clip(r - NA_KH // 2, 0, rows - NA_KH)
        start = pl.multiple_of(CTX_LEN + r0 * GRID_W, GRID_W)
        qs = slice(j * GRID_W, (j + 1) * GRID_W)
        q = (q_ref[qs, :] * NA_HEAD_DIM ** -0.5).astype(BF16)
        k_win = k_ref[pl.ds(start, NA_WIN), :].astype(BF16)
        s_loc = lax.dot_general(q, k_win, dn, preferred_element_type=F32) + bias_ref[0, r - r0]
        s_ctx = lax.dot_general(q, k_ctx, dn, preferred_element_type=F32)
        m = jnp.maximum(jnp.max(s_loc, axis=-1, keepdims=True), jnp.max(s_ctx, axis=-1, keepdims=True))
        p_loc = jnp.exp(s_loc - m)
        p_ctx = jnp.exp(s_ctx - m)
        den = jnp.sum(p_loc, axis=-1, keepdims=True) + jnp.sum(p_ctx, axis=-1, keepdims=True)
        num = jnp.dot(p_loc.astype(BF16), v_ref[pl.ds(start, NA_WIN), :].astype(BF16),
                      preferred_element_type=F32)
        num += jnp.dot(p_ctx.astype(BF16), v_ctx, preferred_element_type=F32)
        o_ref[qs, :] = num / den


def _na_ctx_kernel(q_ref, k_ref, v_ref, o_ref):
    q = (q_ref[...] * NA_HEAD_DIM ** -0.5).astype(BF16)
    s = lax.dot_general(q, k_ref[...].astype(BF16), (((1,), (1,)), ((), ())), preferred_element_type=F32)
    p = jnp.exp(s - jnp.max(s, axis=-1, keepdims=True))
    num = jnp.dot(p.astype(BF16), v_ref[...].astype(BF16), preferred_element_type=F32)
    o_ref[...] = num / jnp.sum(p, axis=-1, keepdims=True)


def _na_bias(rpb, rows):
    kh = min(NA_KH, rows)
    col = np.arange(GRID_W)
    c0 = np.clip(col - NA_KW // 2, 0, GRID_W - NA_KW)
    key = np.arange(GRID_W)
    inside = (key[None, :] >= c0[:, None]) & (key[None, :] < c0[:, None] + NA_KW)
    dc = np.clip(key[None, :] - col[:, None] + (NA_KW - 1), 0, 2 * NA_KW - 2)
    onehot = np.zeros((GRID_W, GRID_W, 2 * NA_KW - 1), np.float32)
    onehot[col[:, None], key[None, :], dc] = 1.0
    toep = jnp.einsum("qcd,hrd->hrqc", onehot, rpb, precision=lax.Precision.HIGHEST)
    toep = jnp.where(jnp.asarray(inside), toep, NEG)
    return jnp.stack([jnp.concatenate([toep[:, i - j + NA_KH - 1] for i in range(kh)], axis=-1)
                      for j in range(NA_KH)], axis=1)


def na_mix(p, rpb):
    R = p.shape[0]
    T = R - CTX_LEN
    rows = T // GRID_W
    qb, kb, vb = [(AB_NA + i * MIX_HALF) // NA_HEAD_DIM for i in range(3)]
    qrows = NA_RB * GRID_W
    cb = CTX_LEN // qrows
    bias = _na_bias(rpb, rows)
    strip = lambda b: pl.BlockSpec((R, NA_HEAD_DIM), lambda h, r: (0, b + h))
    lat = pl.pallas_call(
        functools.partial(_na_kernel, rows=rows),
        grid=(NA_HEADS, rows // NA_RB),
        in_specs=[pl.BlockSpec((qrows, NA_HEAD_DIM), lambda h, r: (cb + r, qb + h)),
                  strip(kb), strip(vb),
                  pl.BlockSpec((1, NA_KH, GRID_W, NA_WIN), lambda h, r: (h, 0, 0, 0))],
        out_specs=pl.BlockSpec((qrows, NA_HEAD_DIM), lambda h, r: (r, h)),
        out_shape=jax.ShapeDtypeStruct((T, MIX_HALF), F32),
        compiler_params=_params(("parallel", "arbitrary")),
        name="na_latent",
    )(p, p, p, bias)
    blk = lambda b: pl.BlockSpec((CTX_LEN, NA_HEAD_DIM), lambda h: (0, b + h))
    ctx = pl.pallas_call(
        _na_ctx_kernel,
        grid=(NA_HEADS,),
        in_specs=[blk(qb), blk(kb), blk(vb)],
        out_specs=pl.BlockSpec((CTX_LEN, NA_HEAD_DIM), lambda h: (0, h)),
        out_shape=jax.ShapeDtypeStruct((CTX_LEN, MIX_HALF), F32),
        compiler_params=_params(("parallel",)),
        name="na_context",
    )(p, p, p)
    return jnp.concatenate([ctx, lat], axis=0)


ML_CHUNK = CTX_LEN
CD_GT = 2 * ML_HEADS * ML_DQK + 2 * MIX_HALF
CD_HY = CD_GT + LANE


def _mlstm_kernel(q_ref, k_ref, v_ref, lir_ref, lfr_ref, lic_ref, lfc_ref, o_ref, c_ref, n_ref, m_ref):
    L = ML_CHUNK

    @pl.when(pl.program_id(2) == 0)
    def _():
        c_ref[...] = jnp.zeros_like(c_ref)
        n_ref[...] = jnp.zeros_like(n_ref)
        m_ref[...] = jnp.zeros_like(m_ref)

    sgn = 1 - 2 * pl.program_id(0)
    row = lax.broadcasted_iota(jnp.int32, (L, L), 0)
    col = lax.broadcasted_iota(jnp.int32, (L, L), 1)
    seen = (row - col) * sgn >= 0
    seen_t = (col - row) * sgn >= 0
    q = q_ref[...]
    k = k_ref[...]
    vb = v_ref[...].astype(BF16)
    li_r, lf_r, li_c, lf_c = lir_ref[0], lfr_ref[0], lic_ref[0], lfc_ref[0]
    b_c = jnp.sum(jnp.where(seen, lf_r, 0.0), axis=1, keepdims=True)
    b_r = jnp.sum(jnp.where(seen_t, lf_c, 0.0), axis=0, keepdims=True)
    m_prev = m_ref[...]
    dmat = jnp.where(seen, b_c - b_r + li_r, NEG)
    inter = b_c + m_prev
    m_t = jnp.maximum(jnp.max(dmat, axis=1, keepdims=True), inter)
    qb = q.astype(BF16)
    s = lax.dot_general(qb, k.astype(BF16), (((1,), (1,)), ((), ())), preferred_element_type=F32)
    s = s * jnp.exp(dmat - m_t)
    dec = jnp.exp(inter - m_t)
    num = jnp.dot(s.astype(BF16), vb, preferred_element_type=F32)
    num += dec * jnp.dot(qb, c_ref[...].astype(BF16), preferred_element_type=F32)
    den = jnp.sum(s, axis=1, keepdims=True) + dec * jnp.sum(q * n_ref[...], axis=1, keepdims=True)
    o_ref[0] = num / jnp.maximum(jnp.abs(den), jnp.exp(-m_t))
    total = jnp.sum(lf_r, axis=1, keepdims=True)
    gl = total - b_c + li_c
    m_new = jnp.maximum(total + m_prev, jnp.max(gl, axis=0, keepdims=True))
    kw = k * jnp.exp(gl - m_new)
    sc = jnp.exp(total + m_prev - m_new)
    c_ref[...] = sc * c_ref[...] + lax.dot_general(kw.astype(BF16), vb, (((0,), (0,)), ((), ())),
                                                   preferred_element_type=F32)
    n_ref[...] = sc * n_ref[...] + jnp.sum(kw, axis=0, keepdims=True)
    m_ref[...] = m_new


def _axial_rope(x):
    T, d = x.shape[0], x.shape[-1]
    half = d // 2
    nf = half // 2
    t = jnp.arange(T)
    row = (t // GRID_W).astype(F32)
    col = (t % GRID_W).astype(F32)
    inv = ROPE_BASE ** (-jnp.arange(nf, dtype=F32) / nf)
    ang = jnp.concatenate([row[:, None] * inv, col[:, None] * inv], axis=-1)[:, None, :]
    cos, sin = jnp.cos(ang), jnp.sin(ang)
    x1, x2 = x[..., :half], x[..., half:]
    return jnp.concatenate([x1 * cos - x2 * sin, x2 * cos + x1 * sin], axis=-1)


def mlstm_mix(p, gate_b, norm_w):
    R = p.shape[0]
    L = ML_CHUNK
    nc = R // L
    QK = ML_HEADS * ML_DQK

    def rope_lat(t):
        th = t.reshape(R, ML_HEADS, ML_DQK)
        return jnp.concatenate([th[:CTX_LEN], _axial_rope(th[CTX_LEN:])], axis=0).reshape(R, QK)

    q = rope_lat(p[:, :QK])
    k = rope_lat(p[:, QK:2 * QK]) * ML_DQK ** -0.5
    o = p[:, 2 * QK + MIX_HALF:CD_GT]
    gt = p[:, CD_GT:CD_GT + 4 * ML_HEADS].reshape(R, 2, 2, ML_HEADS) + gate_b
    log_i = jnp.transpose(gt[:, :, 0], (1, 2, 0))
    log_f = jax.nn.log_sigmoid(jnp.transpose(gt[:, :, 1], (1, 2, 0)))
    as_rows = lambda t: t.reshape(2 * ML_HEADS * nc, 1, L)
    as_cols = lambda t: t.reshape(2 * ML_HEADS * nc, L, 1)

    def chunk(d, c):
        return jnp.where(d == 0, c, jnp.where(c == 0, 0, nc - c))

    gidx = lambda d, h, c: ((d * ML_HEADS + h) * nc + chunk(d, c), 0, 0)
    vb = 2 * QK // ML_DV
    h_dir = pl.pallas_call(
        _mlstm_kernel,
        grid=(2, ML_HEADS, nc),
        in_specs=[pl.BlockSpec((L, ML_DQK), lambda d, h, c: (chunk(d, c), h)),
                  pl.BlockSpec((L, ML_DQK), lambda d, h, c: (chunk(d, c), h)),
                  pl.BlockSpec((L, ML_DV), lambda d, h, c: (chunk(d, c), vb + h)),
                  pl.BlockSpec((1, 1, L), gidx), pl.BlockSpec((1, 1, L), gidx),
                  pl.BlockSpec((1, L, 1), gidx), pl.BlockSpec((1, L, 1), gidx)],
        out_specs=pl.BlockSpec((1, L, ML_DV), lambda d, h, c: (d, chunk(d, c), h)),
        out_shape=jax.ShapeDtypeStruct((2, R, MIX_HALF), F32),
        scratch_shapes=[pltpu.VMEM((ML_DQK, ML_DV), F32), pltpu.VMEM((1, ML_DQK), F32),
                        pltpu.VMEM((1, 1), F32)],
        compiler_params=_params(("parallel", "parallel", "arbitrary")),
        name="mlstm",
    )(q, k, p, as_rows(log_i), as_rows(log_f), as_cols(log_i), as_cols(log_f))
    h = (h_dir[0] + h_dir[1]).reshape(R, ML_HEADS, ML_DV)
    h = h * lax.rsqrt(jnp.mean(h * h, axis=-1, keepdims=True) + RMS_EPS)
    return h.reshape(R, MIX_HALF) * norm_w * jax.nn.sigmoid(o)


def _pad_to(x, axis, mult):
    n = -x.shape[axis] % mult
    if n == 0:
        return x
    pad = [(0, 0)] * x.ndim
    pad[axis] = (0, n)
    return jnp.pad(x, pad)


def mm3p(a, b):
    M, N = a.shape[0], b.shape[1]
    a = _pad_to(_pad_to(a, 1, LANE), 0, 8)
    b = _pad_to(_pad_to(b, 0, LANE), 1, LANE)
    return mm3(a, b)[:M, :N]


def hyena_filters(T, ff1, ff1_b, ff2, ff2_b, ff3, ff3_b, ff_out, freq, decay):
    t = jnp.linspace(0.0, 1.0, T, dtype=F32)[:, None]
    bands = (HY_EMB - 1) // 2
    f = jnp.linspace(1e-4, bands - 1, bands, dtype=F32)[None, :]
    w = (2 * math.pi / T) * jnp.arange(T, dtype=F32)[:, None]
    z = jnp.concatenate([t, jnp.cos(f * w), -jnp.sin(f * w)], axis=-1)
    h = jnp.sin(freq * (mm3p(z, ff1) + ff1_b))
    h = jnp.sin(freq * (mm3p(h, ff2) + ff2_b))
    h = jnp.sin(freq * (mm3p(h, ff3) + ff3_b))
    h = mm3p(h, ff_out) * jnp.exp(-t * jnp.abs(decay))
    return h.reshape(T, HY_ORDER, 2, MIX_HALF)


FFT_N2 = LANE


@functools.lru_cache(maxsize=None)
def _dft_tables(n1):
    n2 = FFT_N2
    n = n1 * n2
    i1, i2 = np.arange(n1), np.arange(n2)
    a1 = 2 * np.pi * np.outer(i1, i1) / n1
    a2 = 2 * np.pi * np.outer(i2, i2) / n2
    c1, s1 = np.cos(a1), -np.sin(a1)
    c2, s2 = np.cos(a2), -np.sin(a2)
    atw = 2 * np.pi * np.outer(i2, i1) / n
    f32 = lambda x: np.asarray(x, np.float32)
    return dict(
        wa=f32(np.concatenate([c1, s1], axis=1)),
        wb=f32(np.block([[c2, s2], [-s2, c2]])),
        wb_inv=f32(np.block([[c2, -s2], [s2, c2]])),
        wa_inv=f32(np.concatenate([c1, s1], axis=0)[:, :n1 // 2] / n),
        twr=f32(np.cos(atw)), twi=f32(-np.sin(atw)))


def _fft_fwd(x_t, n1):
    C = x_t.shape[0]
    n2 = FFT_N2
    tb = _dft_tables(n1)
    xa = jnp.transpose(x_t.reshape(C, n1, n2), (0, 2, 1)).reshape(C * n2, n1)
    a = mm3p(xa, tb["wa"]).reshape(C, n2, 2, n1)
    ar, ai = a[:, :, 0], a[:, :, 1]
    br = ar * tb["twr"] - ai * tb["twi"]
    bi = ar * tb["twi"] + ai * tb["twr"]
    b = jnp.concatenate([jnp.transpose(br, (0, 2, 1)), jnp.transpose(bi, (0, 2, 1))], axis=-1)
    return mm3(b.reshape(C * n1, 2 * n2), tb["wb"])


def _fft_inv(y, C, n1):
    n2 = FFT_N2
    tb = _dft_tables(n1)
    g = mm3(y, tb["wb_inv"]).reshape(C, n1, 2, n2)
    gr, gi = g[:, :, 0], g[:, :, 1]
    twr, twi = tb["twr"].T, tb["twi"].T
    hr = gr * twr + gi * twi
    hi = gi * twr - gr * twi
    h = jnp.concatenate([jnp.transpose(hr, (0, 2, 1)), jnp.transpose(hi, (0, 2, 1))], axis=-1)
    out = mm3p(h.reshape(C * n2, 2 * n1), tb["wa_inv"])
    return jnp.transpose(out.reshape(C, n2, n1 // 2), (0, 2, 1)).reshape(C, n1 // 2 * n2)


def _long_conv(z, h_fwd, h_bwd):
    T, C = z.shape
    n1 = 2 * T // FFT_N2
    f = jnp.concatenate([h_fwd, jnp.zeros((1, C), F32), h_bwd[:0:-1]], axis=0)
    zs = _fft_fwd(jnp.pad(z.T, ((0, 0), (0, T))), n1).reshape(C * n1, 2, FFT_N2)
    fs = _fft_fwd(f.T, n1).reshape(C * n1, 2, FFT_N2)
    yr = zs[:, 0] * fs[:, 0] - zs[:, 1] * fs[:, 1]
    yi = zs[:, 0] * fs[:, 1] + zs[:, 1] * fs[:, 0]
    return _fft_inv(jnp.concatenate([yr, yi], axis=-1), C, n1).T


HY_Q = 4
HY_F1Q = FFT_N2 // HY_Q
HY_QR = HY_F1Q * FFT_N2
HY_NN = FFT_N2 * FFT_N2


@functools.lru_cache(maxsize=None)
def _hy_tables():
    n = FFT_N2
    i = np.arange(n)
    w = np.exp(-2j * np.pi * np.outer(i, i) / n)
    tw = np.exp(-2j * np.pi * np.outer(i, i) / (n * n))
    ca = np.conj(w)[: n // 2] / (n * n)
    f32 = lambda x: np.ascontiguousarray(x, dtype=np.float32)
    return dict(
        wa=f32(np.concatenate([w.real, w.imag], axis=0)),
        wbr=f32(w.real), wbi=f32(w.imag),
        twr=f32(tw.real).reshape(n, 1, n), twi=f32(tw.imag).reshape(n, 1, n),
        ctr=f32(tw.real.T).reshape(n, 1, n), cti=f32(-tw.imag.T).reshape(n, 1, n),
        car=f32(ca.real), cai=f32(ca.imag))


def _dot3(a_hi, a_lo, b):
    b_hi, b_lo = _split(b)
    acc = jnp.dot(a_hi, b_hi, preferred_element_type=F32)
    acc += jnp.dot(a_hi, b_lo, preferred_element_type=F32)
    return acc + jnp.dot(a_lo, b_hi, preferred_element_type=F32)


def _hy_fwd_kernel(x_ref, wa_ref, wbr_ref, wbi_ref, twr_ref, twi_ref, o_ref, a_ref, *, k1):
    n = FFT_N2
    q = pl.program_id(1)

    @pl.when(q == 0)
    def _():
        wa_hi, wa_lo = _split(wa_ref[...])
        for t2 in range(n):
            y = _dot3(wa_hi, wa_lo, x_ref[pl.ds(t2, k1, stride=n), :])
            a_ref[pl.ds(t2, n, stride=n), :] = y[:n]
            a_ref[pl.ds(HY_NN + t2, n, stride=n), :] = y[n:]

    cr, ci = wbr_ref[...], wbi_ref[...]

    def body(j, carry):
        f1 = q * HY_F1Q + j
        base = pl.multiple_of(f1 * n, n)
        rhs = jnp.concatenate([a_ref[pl.ds(base, n), :], a_ref[pl.ds(HY_NN + base, n), :]], axis=0)
        twr, twi = twr_ref[f1], twi_ref[f1]
        pr = cr * twr - ci * twi
        pi = cr * twi + ci * twr
        lhs = jnp.concatenate([jnp.concatenate([pr, -pi], axis=1), jnp.concatenate([pi, pr], axis=1)], axis=0)
        out = _dot3(*_split(lhs), rhs)
        ob = pl.multiple_of(j * n, n)
        o_ref[pl.ds(ob, n), :] = out[:n]
        o_ref[pl.ds(HY_QR + ob, n), :] = out[n:]
        return carry

    lax.fori_loop(0, HY_F1Q, body, 0, unroll=2)


def hy_fwd(x):
    T_in, C = x.shape
    k1 = T_in // FFT_N2
    tb = _hy_tables()
    const = lambda a: pl.BlockSpec(a.shape, lambda c, q: (0,) * a.ndim)
    wa = tb["wa"][:, :k1]
    args = (wa, tb["wbr"], tb["wbi"], tb["twr"], tb["twi"])
    return pl.pallas_call(
        functools.partial(_hy_fwd_kernel, k1=k1),
        grid=(C // LANE, HY_Q),
        in_specs=[pl.BlockSpec((T_in, LANE), lambda c, q: (0, c))] + [const(a) for a in args],
        out_specs=pl.BlockSpec((None, 2 * HY_QR, LANE), lambda c, q: (q, 0, c)),
        out_shape=jax.ShapeDtypeStruct((HY_Q, 2 * HY_QR, C), F32),
        scratch_shapes=[pltpu.VMEM((2 * HY_NN, LANE), F32)],
        compiler_params=_params(("parallel", "arbitrary")),
        name="hy_fwd",
    )(x, *args)


def _hy_conv_kernel(z_ref, f_ref, wbr_ref, wbi_ref, ctr_ref, cti_ref, car_ref, cai_ref, y_ref, g_ref):
    n = FFT_N2
    q = pl.program_id(1)
    cr, ci = wbr_ref[...], wbi_ref[...]
    l_hi, l_lo = _split(jnp.concatenate([jnp.concatenate([cr, ci], axis=1),
                                         jnp.concatenate([-ci, cr], axis=1)], axis=0))

    def body(j, carry):
        ob = pl.multiple_of(j * n, n)
        zr, zi = z_ref[pl.ds(ob, n), :], z_ref[pl.ds(HY_QR + ob, n), :]
        fr, fi = f_ref[pl.ds(ob, n), :], f_ref[pl.ds(HY_QR + ob, n), :]
        prod = jnp.concatenate([zr * fr - zi * fi, zr * fi + zi * fr], axis=0)
        out = _dot3(l_hi, l_lo, prod)
        base = pl.multiple_of((q * HY_F1Q + j) * n, n)
        g_ref[pl.ds(base, n), :] = out[:n]
        g_ref[pl.ds(HY_NN + base, n), :] = out[n:]
        return carry

    lax.fori_loop(0, HY_F1Q, body, 0, unroll=2)

    @pl.when(q == HY_Q - 1)
    def _():
        car, cai = car_ref[...], cai_ref[...]
        for t2 in range(n):
            ctr, cti = ctr_ref[t2], cti_ref[t2]
            er = car * ctr - cai * cti
            ei = car * cti + cai * ctr
            rhs = jnp.concatenate([g_ref[pl.ds(t2, n, stride=n), :],
                                   g_ref[pl.ds(HY_NN + t2, n, stride=n), :]], axis=0)
            y_ref[pl.ds(t2, n // 2, stride=n), :] = _dot3(*_split(jnp.concatenate([er, -ei], axis=1)), rhs)


def hy_conv(zs, fs, f_block=0):
    C = zs.shape[-1]
    T = HY_NN // 2
    tb = _hy_tables()
    const = lambda a: pl.BlockSpec(a.shape, lambda c, q: (0,) * a.ndim)
    spec = pl.BlockSpec((None, 2 * HY_QR, LANE), lambda c, q: (q, 0, c))
    fspec = pl.BlockSpec((None, 2 * HY_QR, LANE), lambda c, q: (q, 0, c + f_block))
    args = (tb["wbr"], tb["wbi"], tb["ctr"], tb["cti"], tb["car"], tb["cai"])
    return pl.pallas_call(
        _hy_conv_kernel,
        grid=(C // LANE, HY_Q),
        in_specs=[spec, fspec] + [const(a) for a in args],
        out_specs=pl.BlockSpec((T, LANE), lambda c, q: (0, c)),
        out_shape=jax.ShapeDtypeStruct((T, C), F32),
        scratch_shapes=[pltpu.VMEM((2 * HY_NN, LANE), F32)],
        compiler_params=_params(("parallel", "arbitrary")),
        name="hy_conv",
    )(zs, fs, *args)


def hyena_filter_taps(T, ff1, ff1_b, ff2, ff2_b, ff3, ff3_b, ff_out, freq, decay):
    t = jnp.linspace(0.0, 1.0, T, dtype=F32)[:, None]
    bands = (HY_EMB - 1) // 2
    f = jnp.linspace(1e-4, bands - 1, bands, dtype=F32)[None, :]
    w = (2 * math.pi / T) * jnp.arange(T, dtype=F32)[:, None]
    z = jnp.concatenate([t, jnp.cos(f * w), -jnp.sin(f * w)], axis=-1)
    h = jnp.sin(freq * (mm3p(z, ff1) + ff1_b))
    h = jnp.sin(freq * (mm3p(h, ff2) + ff2_b))
    h = jnp.sin(freq * (mm3p(h, ff3) + ff3_b))
    w_out = ff_out.reshape(-1, HY_ORDER, 2, MIX_HALF)
    dec = jnp.abs(decay).reshape(HY_ORDER, 2, MIX_HALF)
    side = lambda s: (w_out[:, :, s].reshape(-1, HY_ORDER * MIX_HALF), dec[:, s].reshape(HY_ORDER * MIX_HALF))
    (w_f, d_f), (w_b, d_b) = side(0), side(1)
    fwd = mm3p(h, w_f) * jnp.exp(-t * d_f)
    bwd = mm3p(h[::-1], w_b) * jnp.exp(-t[::-1] * d_b)
    return jnp.concatenate([fwd, jnp.zeros((1, fwd.shape[1]), F32), bwd[:T - 1]], axis=0)


@functools.lru_cache(maxsize=None)
def _dft_small(n):
    i = np.arange(n)
    w = np.exp(-2j * np.pi * np.outer(i, i) / n)
    fwd = np.concatenate([w.real, w.imag], axis=0)
    inv = np.concatenate([w.real, w.imag], axis=1)[: n // 2] / n
    return np.asarray(fwd, np.float32), np.asarray(inv, np.float32)


def _short_conv(z, h_fwd, h_bwd):
    T, C = z.shape
    n = 2 * T
    fwd, inv = _dft_small(n)
    f = jnp.concatenate([h_fwd, jnp.zeros((1, C), F32), h_bwd[:0:-1]], axis=0)
    zs = mm3(fwd[:, :T], z)
    fs = mm3(fwd, f)
    prod = jnp.concatenate([zs[:n] * fs[:n] - zs[n:] * fs[n:], zs[:n] * fs[n:] + zs[n:] * fs[:n]], axis=0)
    return mm3(inv, prod)


def hyena_seq(p, short_w, short_b, conv, bias):
    u = short_w[0] * _shift_prev(p) + short_w[1] * p + short_w[2] * _shift_next(p) + short_b
    z, x1, x2 = jnp.split(u, 3, axis=-1)
    for n, gate in enumerate((x1, x2)):
        z = gate * (conv(z, n) + bias[n] * z)
    return z


def hyena_mix(p, short_w, short_b, filt_params, bias):
    T = p.shape[0] - CTX_LEN
    filt_c = hyena_filters(CTX_LEN, *filt_params)
    out_c = hyena_seq(p[:CTX_LEN], short_w, short_b,
                      lambda z, n: _short_conv(z, filt_c[:, n, 0], filt_c[:, n, 1]), bias)
    if 2 * T == HY_NN:
        spectra = hy_fwd(hyena_filter_taps(T, *filt_params))
        conv = lambda z, n: hy_conv(hy_fwd(z), spectra, n * MIX_HALF // LANE)
    else:
        filt = hyena_filters(T, *filt_params)
        conv = lambda z, n: _long_conv(z, filt[:, n, 0], filt[:, n, 1])
    out = hyena_seq(p[CTX_LEN:], short_w, short_b, conv, bias)
    return jnp.concatenate([out_c, out], axis=0)


def ffn_half(x, mod, g_pre, g_post, w1, w3, w2):
    h = norm_mod(x, g_pre, mod[:, 0], mod[:, 1])
    a = swiglu(h, w1.astype(BF16), w3.astype(BF16))
    return post_res(x, mm(a, w2), g_post, mod[:, 2], FFN_RES)


def kernel(x, c, ctx, c_ctx, ada_down, ada_up, ada_bias, norm_g, ffn_w1, ffn_w3, ffn_w2, ab_w_in, ab_w_out, rwkv_mu, rwkv_w0, rwkv_w2, rwkv_a0, rwkv_a2, rwkv_g2, rwkv_kk, rwkv_ka, rwkv_rk, rwkv_gn, na_rpb, cd_w_in, cd_w_out, ml_gate_b, ml_norm, hy_short_w, hy_short_b, hy_ff1, hy_ff1_b, hy_ff2, hy_ff2_b, hy_ff3, hy_ff3_b, hy_ff_out, hy_freq, hy_decay, hy_bias):
    assert x.shape[0] == 1 and ctx.shape[1] == CTX_LEN
    X = jnp.concatenate([ctx[0], x[0]], axis=0)
    s2 = _pad_to(jnp.stack([jax.nn.silu(c_ctx), jax.nn.silu(c[0])]), 0, 16)
    for l in range(DEPTH):
        i = l // 2
        mod = (mm(mm(s2, ada_down[l]), ada_up[l])[:2] + ada_bias[l]).reshape(2, N_SUB, 3, D_MODEL)
        g = norm_g[l]
        X = ffn_half(X, mod[:, 0], g[0], g[1], ffn_w1[l, 0], ffn_w3[l, 0], ffn_w2[l, 0])
        xn = norm_mod(X, g[2], mod[:, 1, 0], mod[:, 1, 1])
        if l % 2 == 0:
            w_ab = ab_w_in[i].astype(BF16)
            w_rkv = [_kmajor(t) for t in jnp.split(w_ab[:, :AB_WD], 3, axis=-1)]
            p = mm(xn, _pack_cols(jnp.concatenate(w_rkv + [w_ab[:, AB_WD:]], axis=-1), _AB_CUTS, _AB_WIDTHS))
            ya = rwkv_mix(p, rwkv_mu[i], rwkv_w0[i], rwkv_w2[i], rwkv_a0[i], rwkv_a2[i],
                          rwkv_g2[i], rwkv_kk[i], rwkv_ka[i], rwkv_rk[i], rwkv_gn[i])
            y = jnp.concatenate([ya, na_mix(p, na_rpb[i])], axis=-1)
            w_out = ab_w_out[i].astype(BF16)
            w_rw = jnp.swapaxes(w_out[:MIX_HALF].reshape(RWKV_HEADS, RWKV_HEAD, D_MODEL), 0, 1)
            w_out = jnp.concatenate([w_rw.reshape(MIX_HALF, D_MODEL), w_out[MIX_HALF:]], axis=0)
        else:
            w_in = _pack_cols(cd_w_in[i], (CD_GT, ML_COLS), (CD_GT, LANE, P_COLS - CD_HY))
            p = mm(xn, w_in)
            filt_params = (hy_ff1[i], hy_ff1_b[i], hy_ff2[i], hy_ff2_b[i], hy_ff3[i], hy_ff3_b[i],
                           hy_ff_out[i], hy_freq[i], hy_decay[i])
            yd = hyena_mix(p[:, CD_HY:CD_HY + 3 * MIX_HALF], hy_short_w[i], hy_short_b[i], filt_params,
                           hy_bias[i])
            y = jnp.concatenate([mlstm_mix(p, ml_gate_b[i], ml_norm[i]), yd], axis=-1)
            w_out = cd_w_out[i]
        X = post_res(X, mm(y, w_out), g[3], mod[:, 1, 2], 1.0)
        X = ffn_half(X, mod[:, 2], g[4], g[5], ffn_w1[l, 1], ffn_w3[l, 1], ffn_w2[l, 1])
    return X[CTX_LEN:][None]
```

```python
import functools
import math

import jax
import jax.numpy as jnp
import numpy as np
from jax import lax
from jax.experimental import pallas as pl
from jax.experimental.pallas import tpu as pltpu

F32 = jnp.float32
BF16 = jnp.bfloat16

D_MODEL = 4096
DEPTH = 4
GRID_W = 64
CTX_LEN = 256
N_SUB = 3
FFN_RES = 0.5
D_FF = 5632
RMS_EPS = 1e-6
MIX_HALF = D_MODEL // 2

RWKV_HEAD = 64
RWKV_HEADS = MIX_HALF // RWKV_HEAD
RWKV_DECAY_LORA = 96
RWKV_A_LORA = 96
RWKV_GATE_LORA = 256
RWKV_COLS = 3 * MIX_HALF + RWKV_DECAY_LORA + RWKV_A_LORA + RWKV_GATE_LORA
RWKV_GN_EPS = 64e-5

NA_HEAD_DIM = 128
NA_HEADS = MIX_HALF // NA_HEAD_DIM
NA_KH = 8
NA_KW = 16

ML_HEADS = 4
ML_DV = MIX_HALF // ML_HEADS
ML_DQK = ML_DV // 2
ML_COLS = 2 * ML_HEADS * ML_DQK + 2 * MIX_HALF + 4 * ML_HEADS
ROPE_BASE = 10000.0

HY_ORDER = 2
HY_EMB = 33

LANE = 128
ROW_TILE = CTX_LEN
VMEM_LIMIT = 56 * 1024 * 1024
P_COLS = 12800
NEG = -1e30


def _params(sem):
    return pltpu.CompilerParams(dimension_semantics=sem, vmem_limit_bytes=VMEM_LIMIT)


def _pick(n, cands):
    for c in cands:
        if n % c == 0:
            return c
    return n


def _mm_kernel(a_ref, b_ref, o_ref):
    o_ref[...] = jnp.dot(a_ref[...], b_ref[...], preferred_element_type=F32).astype(o_ref.dtype)


def _split(x):
    hi = x.astype(BF16)
    lo = (x - hi.astype(F32)).astype(BF16)
    return hi, lo


def _mm3_kernel(a_ref, b_ref, o_ref):
    a_hi, a_lo = _split(a_ref[...])
    b_hi, b_lo = _split(b_ref[...])
    acc = jnp.dot(a_hi, b_hi, preferred_element_type=F32)
    acc += jnp.dot(a_hi, b_lo, preferred_element_type=F32)
    acc += jnp.dot(a_lo, b_hi, preferred_element_type=F32)
    o_ref[...] = acc


def _mm_call(kern, a, b, out_dtype, name):
    M, K = a.shape
    N = b.shape[1]
    tm = _pick(M, (1024, 768, 512, 256, 128, 64, 32, 16, 8))
    tn = _pick(N, (512, 256, 128))
    return pl.pallas_call(
        kern,
        grid=(M // tm, N // tn),
        in_specs=[pl.BlockSpec((tm, K), lambda i, j: (i, 0)),
                  pl.BlockSpec((K, tn), lambda i, j: (0, j))],
        out_specs=pl.BlockSpec((tm, tn), lambda i, j: (i, j)),
        out_shape=jax.ShapeDtypeStruct((M, N), out_dtype),
        compiler_params=_params(("parallel", "parallel")),
        name=name,
    )(a, b)


def mm(a, b, out_dtype=F32):
    return _mm_call(_mm_kernel, a.astype(BF16), b.astype(BF16), out_dtype, "mm")


def mm3(a, b):
    return _mm_call(_mm3_kernel, a.astype(F32), b.astype(F32), F32, "mm3")


def _mm_w32_kernel(a_ref, w_ref, o_ref, wb_ref):
    @pl.when(pl.program_id(1) == 0)
    def _():
        wb_ref[...] = w_ref[...].astype(BF16)

    o_ref[...] = jnp.dot(a_ref[...], wb_ref[...], preferred_element_type=F32).astype(o_ref.dtype)


def _swiglu_kernel(a_ref, w1_ref, w3_ref, o_ref, w1b_ref, w3b_ref):
    @pl.when(pl.program_id(1) == 0)
    def _():
        w1b_ref[...] = w1_ref[...].astype(BF16)
        w3b_ref[...] = w3_ref[...].astype(BF16)

    a = a_ref[...]
    h1 = jnp.dot(a, w1b_ref[...], preferred_element_type=F32)
    h3 = jnp.dot(a, w3b_ref[...], preferred_element_type=F32)
    o_ref[...] = (h1 * jax.nn.sigmoid(h1) * h3).astype(o_ref.dtype)


def _w32_call(kern, a, ws, tn, out_dtype, name):
    M, K = a.shape
    N = ws[0].shape[1]
    tm = _pick(M, (768, 512, 256))
    wspec = pl.BlockSpec((K, tn), lambda j, i: (0, j))
    return pl.pallas_call(
        kern,
        grid=(N // tn, M // tm),
        in_specs=[pl.BlockSpec((tm, K), lambda j, i: (i, 0))] + [wspec] * len(ws),
        out_specs=pl.BlockSpec((tm, tn), lambda j, i: (i, j)),
        out_shape=jax.ShapeDtypeStruct((M, N), out_dtype),
        scratch_shapes=[pltpu.VMEM((K, tn), BF16)] * len(ws),
        compiler_params=_params(("parallel", "arbitrary")),
        name=name,
    )(a, *ws)


def swiglu(a, w1, w3):
    return _w32_call(_swiglu_kernel, a, (w1, w3), 256, BF16, "swiglu")


def mm_w32(a, w):
    return _w32_call(_mm_w32_kernel, a, (w,), 512, F32, "mm_w32")


def _group_spec(d):
    return pl.BlockSpec((1, 1, d), lambda i: (jnp.minimum(i, 1), 0, 0))


def _norm_mod_kernel(x_ref, g_ref, sh_ref, sc_ref, o_ref):
    x = x_ref[...]
    y = x * lax.rsqrt(jnp.mean(x * x, axis=-1, keepdims=True) + RMS_EPS) * g_ref[...]
    o_ref[...] = (y * (1 + sc_ref[0]) + sh_ref[0]).astype(o_ref.dtype)


def norm_mod(x, g, shift, scale):
    R, D = x.shape
    row = pl.BlockSpec((ROW_TILE, D), lambda i: (i, 0))
    return pl.pallas_call(
        _norm_mod_kernel,
        grid=(R // ROW_TILE,),
        in_specs=[row, pl.BlockSpec((1, D), lambda i: (0, 0)), _group_spec(D), _group_spec(D)],
        out_specs=row,
        out_shape=jax.ShapeDtypeStruct((R, D), BF16),
        compiler_params=_params(("parallel",)),
        name="norm_mod",
    )(x, g.reshape(1, D), shift.reshape(2, 1, D), scale.reshape(2, 1, D))


def _post_res_kernel(x_ref, y_ref, g_ref, gate_ref, o_ref, *, coef):
    y = y_ref[...]
    yn = y * lax.rsqrt(jnp.mean(y * y, axis=-1, keepdims=True) + RMS_EPS) * g_ref[...]
    o_ref[...] = x_ref[...] + coef * gate_ref[0] * yn


def _post_res_norm_kernel(x_ref, y_ref, g_ref, gate_ref, gn_ref, sh_ref, sc_ref, o_ref, h_ref, *, coef):
    y = y_ref[...]
    yn = y * lax.rsqrt(jnp.mean(y * y, axis=-1, keepdims=True) + RMS_EPS) * g_ref[...]
    x = x_ref[...] + coef * gate_ref[0] * yn
    o_ref[...] = x
    h = x * lax.rsqrt(jnp.mean(x * x, axis=-1, keepdims=True) + RMS_EPS) * gn_ref[...]
    h_ref[...] = (h * (1 + sc_ref[0]) + sh_ref[0]).astype(h_ref.dtype)


def post_res_norm(x, y, g, gate, coef, g_next, shift_next, scale_next):
    R, D = x.shape
    row = pl.BlockSpec((ROW_TILE, D), lambda i: (i, 0))
    vec = pl.BlockSpec((1, D), lambda i: (0, 0))
    grp = lambda t: t.reshape(2, 1, D)
    return pl.pallas_call(
        functools.partial(_post_res_norm_kernel, coef=coef),
        grid=(R // ROW_TILE,),
        in_specs=[row, row, vec, _group_spec(D), vec, _group_spec(D), _group_spec(D)],
        out_specs=[row, row],
        out_shape=[jax.ShapeDtypeStruct((R, D), F32), jax.ShapeDtypeStruct((R, D), BF16)],
        compiler_params=_params(("parallel",)),
        name="post_res_norm",
    )(x, y, g.reshape(1, D), grp(gate), g_next.reshape(1, D), grp(shift_next), grp(scale_next))


def post_res(x, y, g, gate, coef):
    R, D = x.shape
    row = pl.BlockSpec((ROW_TILE, D), lambda i: (i, 0))
    return pl.pallas_call(
        functools.partial(_post_res_kernel, coef=coef),
        grid=(R // ROW_TILE,),
        in_specs=[row, row, pl.BlockSpec((1, D), lambda i: (0, 0)), _group_spec(D)],
        out_specs=row,
        out_shape=jax.ShapeDtypeStruct((R, D), F32),
        compiler_params=_params(("parallel",)),
        name="post_res",
    )(x, y, g.reshape(1, D), gate.reshape(2, 1, D))


WKV_TB = 8
WKV_KG = MIX_HALF // LANE
WKV_VC = RWKV_HEAD // 8
WKV_CG = 2
PREP_TB = 128


def _kmajor(w):
    lead = w.shape[:-1]
    return jnp.swapaxes(w.reshape(*lead, RWKV_HEADS, RWKV_HEAD), -1, -2).reshape(*lead, MIX_HALF)


def _fold_lanes(s):
    s = s + pltpu.roll(s, 64, 1)
    return s + pltpu.roll(s, 32, 1)


def _rwkv_prep_kernel(p_ref, hp_ref, hn_ref, mu_ref, w2_ref, a2_ref, g2_ref, vec_ref,
                      a_out, b0_out, b1_out, k0_out, k1_out, w0_out, w1_out, r_out, v_out, g_out, bonus_out):
    W = MIX_HALF
    i = pl.program_id(0)
    cb = CTX_LEN // PREP_TB
    has_prev = jnp.logical_and(i != 0, i != cb).astype(F32)
    has_next = jnp.logical_and(i != cb - 1, i != pl.num_programs(0) - 1).astype(F32)
    row = lax.broadcasted_iota(jnp.int32, (PREP_TB, 1), 0)

    def shifted(c0, width):
        x = p_ref[:, c0:c0 + width]
        up = jnp.where(row == 0, hp_ref[7:8, c0:c0 + width] * has_prev, pltpu.roll(x, 1, 0))
        dn = jnp.where(row == PREP_TB - 1, hn_ref[0:1, c0:c0 + width] * has_next,
                       pltpu.roll(x, PREP_TB - 1, 0))
        return x + mu_ref[0:1, c0:c0 + width] * (up - x) + mu_ref[1:2, c0:c0 + width] * (dn - x)

    lo = shifted(AB_WD, AB_NA - AB_WD)
    wd, ad, gd = lo[:, :LANE], lo[:, LANE:2 * LANE], lo[:, 2 * LANE:]
    wlo = jnp.dot(jnp.tanh(wd).astype(BF16), w2_ref[...], preferred_element_type=F32)
    alo = jnp.dot(ad.astype(BF16), a2_ref[...], preferred_element_type=F32)
    g_out[...] = jnp.dot(jax.nn.sigmoid(gd).astype(BF16), g2_ref[...], preferred_element_type=F32)

    ss = None
    for g in range(WKV_KG):
        sl = slice(g * LANE, (g + 1) * LANE)
        kk = shifted(W + g * LANE, LANE) * vec_ref[4:5, sl]
        ss = kk * kk if ss is None else ss + kk * kk
    nrm = jnp.maximum(jnp.sqrt(_fold_lanes(ss)), 1e-12)
    bonus = None
    for g in range(WKV_KG):
        sl = slice(g * LANE, (g + 1) * LANE)
        r = shifted(g * LANE, LANE)
        k = shifted(W + g * LANE, LANE)
        r_out[:, sl] = r
        v_out[:, sl] = shifted(2 * W + g * LANE, LANE)
        kk = k * vec_ref[4:5, sl] / nrm
        a_out[:, sl] = -kk
        for d, (w_out, k_out, b_out) in enumerate(((w0_out, k0_out, b0_out), (w1_out, k1_out, b1_out))):
            dl = slice(d * W + g * LANE, d * W + (g + 1) * LANE)
            logw = -jax.nn.softplus(-(vec_ref[d:d + 1, sl] + wlo[:, dl])) - 0.5
            w_out[:, sl] = jnp.exp(-jnp.exp(logw))
            a = jax.nn.sigmoid(vec_ref[2 + d:3 + d, sl] + alo[:, dl])
            kd = k * (1 + (a - 1) * vec_ref[5:6, sl])
            k_out[:, sl] = kd
            b_out[:, sl] = kk * a
            term = r * kd * vec_ref[6:7, sl]
            bonus = term if bonus is None else bonus + term
    bonus_out[...] = _fold_lanes(bonus)


def rwkv_prep(p, mu, w2c, a2c, g2, vec):
    R = p.shape[0]
    W = MIX_HALF
    hb = PREP_TB // 8
    full = lambda a: pl.BlockSpec(a.shape, lambda i: (0,) * a.ndim)
    wide = pl.BlockSpec((PREP_TB, W), lambda i: (i, 0))
    return pl.pallas_call(
        _rwkv_prep_kernel,
        grid=(R // PREP_TB,),
        in_specs=[pl.BlockSpec((PREP_TB, AB_NA), lambda i: (i, 0)),
                  pl.BlockSpec((8, AB_NA), lambda i: (jnp.maximum(i * hb - 1, 0), 0)),
                  pl.BlockSpec((8, AB_NA), lambda i: (jnp.minimum((i + 1) * hb, R // 8 - 1), 0)),
                  full(mu), full(w2c), full(a2c), full(g2), full(vec)],
        out_specs=[wide] * 10 + [pl.BlockSpec((PREP_TB, LANE), lambda i: (i, 0))],
        out_shape=[jax.ShapeDtypeStruct((R, W), F32)] * 10 + [jax.ShapeDtypeStruct((R, LANE), F32)],
        compiler_params=_params(("parallel",)),
        name="rwkv_prep",
    )(p, p, p, mu, w2c, a2c, g2, vec)


def _wkv_kernel(a0_ref, b0_ref, k0_ref, w0_ref, r0_ref, v0_ref, a1_ref, b1_ref, k1_ref, w1_ref, r1_ref, v1_ref,
                y0_ref, y1_ref, s_ref):
    @pl.when(pl.program_id(0) == 0)
    def _():
        s_ref[...] = jnp.zeros_like(s_ref)

    dirs = ((a0_ref, b0_ref, k0_ref, w0_ref, r0_ref, v0_ref, y0_ref),
            (a1_ref, b1_ref, k1_ref, w1_ref, r1_ref, v1_ref, y1_ref))

    for t in range(WKV_TB):
        for d, (a_ref, b_ref, k_ref, w_ref, r_ref, v_ref, y_ref) in enumerate(dirs):
            tt = t if d == 0 else WKV_TB - 1 - t
            row = lambda ref, g: ref[tt:tt + 1, g * LANE:(g + 1) * LANE]
            for c0 in range(0, WKV_VC, WKV_CG):
                cs = range(c0, c0 + WKV_CG)
                acc = {c: None for c in cs}
                for g in range(WKV_KG):
                    ar = row(a_ref, g)
                    for c in cs:
                        term = s_ref[d, g, pl.ds(c * 8, 8), :] * ar
                        acc[c] = term if acc[c] is None else acc[c] + term
                sa = {c: _fold_lanes(acc[c]) for c in cs}
                vv = {c: v_ref[tt, pl.ds(c * 8, 8), :] for c in cs}
                acc = {c: None for c in cs}
                for g in range(WKV_KG):
                    wr, br, kr, rr = row(w_ref, g), row(b_ref, g), row(k_ref, g), row(r_ref, g)
                    for c in cs:
                        s_new = s_ref[d, g, pl.ds(c * 8, 8), :] * wr + sa[c] * br + vv[c] * kr
                        s_ref[d, g, pl.ds(c * 8, 8), :] = s_new
                        term = s_new * rr
                        acc[c] = term if acc[c] is None else acc[c] + term
                for c in cs:
                    y_ref[tt, pl.ds(c * 8, 8), :] = _fold_lanes(acc[c])


def wkv_scan(a, b0, b1, k0, k1, w0, w1, r, vp):
    R = a.shape[0]
    nb = R // WKV_TB
    cb = CTX_LEN // WKV_TB
    fwd = lambda i: i
    bwd = lambda i: jnp.where(i < cb, cb - 1 - i, nb + cb - 1 - i)
    kspec = lambda f: pl.BlockSpec((WKV_TB, MIX_HALF), lambda i: (f(i), 0))
    vspec = lambda f: pl.BlockSpec((WKV_TB, RWKV_HEAD, LANE), lambda i: (f(i), 0, 0))
    yshape = jax.ShapeDtypeStruct((R, RWKV_HEAD, LANE), F32)
    return pl.pallas_call(
        _wkv_kernel,
        grid=(nb,),
        in_specs=[kspec(fwd)] * 5 + [vspec(fwd)] + [kspec(bwd)] * 5 + [vspec(bwd)],
        out_specs=[vspec(fwd), vspec(bwd)],
        out_shape=[yshape, yshape],
        scratch_shapes=[pltpu.VMEM((2, WKV_KG, RWKV_HEAD, LANE), F32)],
        compiler_params=_params(("arbitrary",)),
        name="wkv_scan",
    )(a, b0, k0, w0, r, vp, a, b1, k1, w1, r, vp)


def _shift_prev(y):
    return jnp.pad(y, ((1, 0), (0, 0)))[:-1]


def _shift_next(y):
    return jnp.pad(y, ((0, 1), (0, 0)))[1:]


AB_WD = 3 * MIX_HALF
AB_AD = AB_WD + LANE
AB_GD = AB_AD + LANE
AB_NA = AB_GD + RWKV_GATE_LORA


def _pack_cols(w, cuts, widths):
    pieces = jnp.split(w, cuts, axis=-1)
    pad = [(0, 0)] * (w.ndim - 1)
    return jnp.concatenate([jnp.pad(p, pad + [(0, wd - p.shape[-1])]) for p, wd in zip(pieces, widths)],
                           axis=-1)


_AB_CUTS = (AB_WD, AB_WD + RWKV_DECAY_LORA, AB_WD + RWKV_DECAY_LORA + RWKV_A_LORA, RWKV_COLS)
_AB_WIDTHS = (AB_WD, LANE, LANE, RWKV_GATE_LORA, 3 * MIX_HALF)


def rwkv_mix(p, mu, w0, w2, a0, a2, g2, k_k, k_a, r_k, gn):
    R = p.shape[0]
    km3 = lambda t: jnp.concatenate([_kmajor(x) for x in jnp.split(t, 3, axis=-1)], axis=-1)
    mu_p = _pack_cols(mu, _AB_CUTS[:3], _AB_WIDTHS[:4])
    mu_p = jnp.concatenate([km3(mu_p[:, :AB_WD]), mu_p[:, AB_WD:]], axis=-1)
    lora = lambda w: jnp.pad(jnp.concatenate([_kmajor(w[0]), _kmajor(w[1])], axis=-1),
                             ((0, LANE - w.shape[1]), (0, 0))).astype(BF16)
    vec = jnp.stack([_kmajor(t) for t in (w0[0], w0[1], a0[0], a0[1], k_k, k_a, r_k, jnp.zeros_like(r_k))])
    a, b0, b1, k0, k1, d0, d1, r, v, g, bonus = rwkv_prep(p, mu_p, lora(w2), lora(a2),
                                                          _kmajor(g2).astype(BF16), vec)
    tile = lambda t: t.reshape(R, RWKV_HEAD, RWKV_HEADS)
    vp = jnp.broadcast_to(tile(v)[:, :, None, :], (R, RWKV_HEAD, 4, RWKV_HEADS)).reshape(R, RWKV_HEAD, LANE)
    y0, y1 = wkv_scan(a, b0, b1, k0, k1, d0, d1, r, vp)
    y = (y0 + y1)[:, :, :RWKV_HEADS]
    mean = jnp.mean(y, axis=1, keepdims=True)
    var = jnp.mean(jnp.square(y - mean), axis=1, keepdims=True)
    gn_t = _kmajor(gn).reshape(2, RWKV_HEAD, RWKV_HEADS)
    yn = (y - mean) * lax.rsqrt(var + RWKV_GN_EPS) * gn_t[0] + gn_t[1]
    out = (yn + bonus[:, None, :RWKV_HEADS] * tile(v)) * tile(g)
    return out.reshape(R, MIX_HALF)


NA_WIN = NA_KH * GRID_W


NA_RB = 4


def _na_kernel(q_ref, k_ref, v_ref, bias_ref, o_ref, *, rows):
    dn = (((1,), (1,)), ((), ()))
    k_ctx = k_ref[pl.ds(0, CTX_LEN), :].astype(BF16)
    v_ctx = v_ref[pl.ds(0, CTX_LEN), :].astype(BF16)
    for j in range(NA_RB):
        r = pl.program_id(1) * NA_RB + j
        r0 = jnp.clip(r - NA_KH // 2, 0, rows - NA_KH)
        start = pl.multiple_of(CTX_LEN + r0 * GRID_W, GRID_W)
        qs = slice(j * GRID_W, (j + 1) * GRID_W)
        q = (q_ref[qs, :] * NA_HEAD_DIM ** -0.5).astype(BF16)
        k_win = k_ref[pl.ds(start, NA_WIN), :].astype(BF16)
        s_loc = lax.dot_general(q, k_win, dn, preferred_element_type=F32) + bias_ref[0, r - r0]
        s_ctx = lax.dot_general(q, k_ctx, dn, preferred_element_type=F32)
        m = jnp.maximum(jnp.max(s_loc, axis=-1, keepdims=True), jnp.max(s_ctx, axis=-1, keepdims=True))
        p_loc = jnp.exp(s_loc - m)
        p_ctx = jnp.exp(s_ctx - m)
        den = jnp.sum(p_loc, axis=-1, keepdims=True) + jnp.sum(p_ctx, axis=-1, keepdims=True)
        num = jnp.dot(p_loc.astype(BF16), v_ref[pl.ds(start, NA_WIN), :].astype(BF16),
                      preferred_element_type=F32)
        num += jnp.dot(p_ctx.astype(BF16), v_ctx, preferred_element_type=F32)
        o_ref[qs, :] = num / den


def _na_ctx_kernel(q_ref, k_ref, v_ref, o_ref):
    q = (q_ref[...] * NA_HEAD_DIM ** -0.5).astype(BF16)
    s = lax.dot_general(q, k_ref[...].astype(BF16), (((1,), (1,)), ((), ())), preferred_element_type=F32)
    p = jnp.exp(s - jnp.max(s, axis=-1, keepdims=True))
    num = jnp.dot(p.astype(BF16), v_ref[...].astype(BF16), preferred_element_type=F32)
    o_ref[...] = num / jnp.sum(p, axis=-1, keepdims=True)


def _na_bias(rpb, rows):
    kh = min(NA_KH, rows)
    col = np.arange(GRID_W)
    c0 = np.clip(col - NA_KW // 2, 0, GRID_W - NA_KW)
    key = np.arange(GRID_W)
    inside = (key[None, :] >= c0[:, None]) & (key[None, :] < c0[:, None] + NA_KW)
    dc = np.clip(key[None, :] - col[:, None] + (NA_KW - 1), 0, 2 * NA_KW - 2)
    onehot = np.zeros((GRID_W, GRID_W, 2 * NA_KW - 1), np.float32)
    onehot[col[:, None], key[None, :], dc] = 1.0
    toep = jnp.einsum("qcd,hrd->hrqc", onehot, rpb, precision=lax.Precision.HIGHEST)
    toep = jnp.where(jnp.asarray(inside), toep, NEG)
    return jnp.stack([jnp.concatenate([toep[:, i - j + NA_KH - 1] for i in range(kh)], axis=-1)
                      for j in range(NA_KH)], axis=1)


def na_mix(p, rpb):
    R = p.shape[0]
    T = R - CTX_LEN
    rows = T // GRID_W
    qb, kb, vb = [(AB_NA + i * MIX_HALF) // NA_HEAD_DIM for i in range(3)]
    qrows = NA_RB * GRID_W
    cb = CTX_LEN // qrows
    bias = _na_bias(rpb, rows)
    strip = lambda b: pl.BlockSpec((R, NA_HEAD_DIM), lambda h, r: (0, b + h))
    lat = pl.pallas_call(
        functools.partial(_na_kernel, rows=rows),
        grid=(NA_HEADS, rows // NA_RB),
        in_specs=[pl.BlockSpec((qrows, NA_HEAD_DIM), lambda h, r: (cb + r, qb + h)),
                  strip(kb), strip(vb),
                  pl.BlockSpec((1, NA_KH, GRID_W, NA_WIN), lambda h, r: (h, 0, 0, 0))],
        out_specs=pl.BlockSpec((qrows, NA_HEAD_DIM), lambda h, r: (r, h)),
        out_shape=jax.ShapeDtypeStruct((T, MIX_HALF), F32),
        compiler_params=_params(("parallel", "arbitrary")),
        name="na_latent",
    )(p, p, p, bias)
    blk = lambda b: pl.BlockSpec((CTX_LEN, NA_HEAD_DIM), lambda h: (0, b + h))
    ctx = pl.pallas_call(
        _na_ctx_kernel,
        grid=(NA_HEADS,),
        in_specs=[blk(qb), blk(kb), blk(vb)],
        out_specs=pl.BlockSpec((CTX_LEN, NA_HEAD_DIM), lambda h: (0, h)),
        out_shape=jax.ShapeDtypeStruct((CTX_LEN, MIX_HALF), F32),
        compiler_params=_params(("parallel",)),
        name="na_context",
    )(p, p, p)
    return jnp.concatenate([ctx, lat], axis=0)


ML_CHUNK = CTX_LEN
CD_GT = 2 * ML_HEADS * ML_DQK + 2 * MIX_HALF
CD_HY = CD_GT + LANE


def _mlstm_kernel(q_ref, k_ref, v_ref, lir_ref, lfr_ref, lic_ref, lfc_ref, o_ref, c_ref, n_ref, m_ref):
    L = ML_CHUNK

    @pl.when(pl.program_id(2) == 0)
    def _():
        c_ref[...] = jnp.zeros_like(c_ref)
        n_ref[...] = jnp.zeros_like(n_ref)
        m_ref[...] = jnp.zeros_like(m_ref)

    sgn = 1 - 2 * pl.program_id(0)
    row = lax.broadcasted_iota(jnp.int32, (L, L), 0)
    col = lax.broadcasted_iota(jnp.int32, (L, L), 1)
    seen = (row - col) * sgn >= 0
    seen_t = (col - row) * sgn >= 0
    q = q_ref[...]
    k = k_ref[...]
    vb = v_ref[...].astype(BF16)
    li_r, lf_r, li_c, lf_c = lir_ref[0], lfr_ref[0], lic_ref[0], lfc_ref[0]
    b_c = jnp.sum(jnp.where(seen, lf_r, 0.0), axis=1, keepdims=True)
    b_r = jnp.sum(jnp.where(seen_t, lf_c, 0.0), axis=0, keepdims=True)
    m_prev = m_ref[...]
    dmat = jnp.where(seen, b_c - b_r + li_r, NEG)
    inter = b_c + m_prev
    m_t = jnp.maximum(jnp.max(dmat, axis=1, keepdims=True), inter)
    qb = q.astype(BF16)
    s = lax.dot_general(qb, k.astype(BF16), (((1,), (1,)), ((), ())), preferred_element_type=F32)
    s = s * jnp.exp(dmat - m_t)
    dec = jnp.exp(inter - m_t)
    num = jnp.dot(s.astype(BF16), vb, preferred_element_type=F32)
    num += dec * jnp.dot(qb, c_ref[...].astype(BF16), preferred_element_type=F32)
    den = jnp.sum(s, axis=1, keepdims=True) + dec * jnp.sum(q * n_ref[...], axis=1, keepdims=True)
    o_ref[0] = num / jnp.maximum(jnp.abs(den), jnp.exp(-m_t))
    total = jnp.sum(lf_r, axis=1, keepdims=True)
    gl = total - b_c + li_c
    m_new = jnp.maximum(total + m_prev, jnp.max(gl, axis=0, keepdims=True))
    kw = k * jnp.exp(gl - m_new)
    sc = jnp.exp(total + m_prev - m_new)
    c_ref[...] = sc * c_ref[...] + lax.dot_general(kw.astype(BF16), vb, (((0,), (0,)), ((), ())),
                                                   preferred_element_type=F32)
    n_ref[...] = sc * n_ref[...] + jnp.sum(kw, axis=0, keepdims=True)
    m_ref[...] = m_new


def _axial_rope(x):
    T, d = x.shape[0], x.shape[-1]
    half = d // 2
    nf = half // 2
    t = jnp.arange(T)
    row = (t // GRID_W).astype(F32)
    col = (t % GRID_W).astype(F32)
    inv = ROPE_BASE ** (-jnp.arange(nf, dtype=F32) / nf)
    ang = jnp.concatenate([row[:, None] * inv, col[:, None] * inv], axis=-1)[:, None, :]
    cos, sin = jnp.cos(ang), jnp.sin(ang)
    x1, x2 = x[..., :half], x[..., half:]
    return jnp.concatenate([x1 * cos - x2 * sin, x2 * cos + x1 * sin], axis=-1)


def mlstm_mix(p, gate_b, norm_w):
    R = p.shape[0]
    L = ML_CHUNK
    nc = R // L
    QK = ML_HEADS * ML_DQK

    def rope_lat(t):
        th = t.reshape(R, ML_HEADS, ML_DQK)
        return jnp.concatenate([th[:CTX_LEN], _axial_rope(th[CTX_LEN:])], axis=0).reshape(R, QK)

    q = rope_lat(p[:, :QK])
    k = rope_lat(p[:, QK:2 * QK]) * ML_DQK ** -0.5
    o = p[:, 2 * QK + MIX_HALF:CD_GT]
    gt = p[:, CD_GT:CD_GT + 4 * ML_HEADS].reshape(R, 2, 2, ML_HEADS) + gate_b
    log_i = jnp.transpose(gt[:, :, 0], (1, 2, 0))
    log_f = jax.nn.log_sigmoid(jnp.transpose(gt[:, :, 1], (1, 2, 0)))
    as_rows = lambda t: t.reshape(2 * ML_HEADS * nc, 1, L)
    as_cols = lambda t: t.reshape(2 * ML_HEADS * nc, L, 1)

    def chunk(d, c):
        return jnp.where(d == 0, c, jnp.where(c == 0, 0, nc - c))

    gidx = lambda d, h, c: ((d * ML_HEADS + h) * nc + chunk(d, c), 0, 0)
    vb = 2 * QK // ML_DV
    h_dir = pl.pallas_call(
        _mlstm_kernel,
        grid=(2, ML_HEADS, nc),
        in_specs=[pl.BlockSpec((L, ML_DQK), lambda d, h, c: (chunk(d, c), h)),
                  pl.BlockSpec((L, ML_DQK), lambda d, h, c: (chunk(d, c), h)),
                  pl.BlockSpec((L, ML_DV), lambda d, h, c: (chunk(d, c), vb + h)),
                  pl.BlockSpec((1, 1, L), gidx), pl.BlockSpec((1, 1, L), gidx),
                  pl.BlockSpec((1, L, 1), gidx), pl.BlockSpec((1, L, 1), gidx)],
        out_specs=pl.BlockSpec((1, L, ML_DV), lambda d, h, c: (d, chunk(d, c), h)),
        out_shape=jax.ShapeDtypeStruct((2, R, MIX_HALF), F32),
        scratch_shapes=[pltpu.VMEM((ML_DQK, ML_DV), F32), pltpu.VMEM((1, ML_DQK), F32),
                        pltpu.VMEM((1, 1), F32)],
        compiler_params=_params(("parallel", "parallel", "arbitrary")),
        name="mlstm",
    )(q, k, p, as_rows(log_i), as_rows(log_f), as_cols(log_i), as_cols(log_f))
    h = (h_dir[0] + h_dir[1]).reshape(R, ML_HEADS, ML_DV)
    h = h * lax.rsqrt(jnp.mean(h * h, axis=-1, keepdims=True) + RMS_EPS)
    return h.reshape(R, MIX_HALF) * norm_w * jax.nn.sigmoid(o)


def _pad_to(x, axis, mult):
    n = -x.shape[axis] % mult
    if n == 0:
        return x
    pad = [(0, 0)] * x.ndim
    pad[axis] = (0, n)
    return jnp.pad(x, pad)


def mm3p(a, b):
    M, N = a.shape[0], b.shape[1]
    a = _pad_to(_pad_to(a, 1, LANE), 0, 8)
    b = _pad_to(_pad_to(b, 0, LANE), 1, LANE)
    return mm3(a, b)[:M, :N]


def hyena_filters(T, ff1, ff1_b, ff2, ff2_b, ff3, ff3_b, ff_out, freq, decay):
    t = jnp.linspace(0.0, 1.0, T, dtype=F32)[:, None]
    bands = (HY_EMB - 1) // 2
    f = jnp.linspace(1e-4, bands - 1, bands, dtype=F32)[None, :]
    w = (2 * math.pi / T) * jnp.arange(T, dtype=F32)[:, None]
    z = jnp.concatenate([t, jnp.cos(f * w), -jnp.sin(f * w)], axis=-1)
    h = jnp.sin(freq * (mm3p(z, ff1) + ff1_b))
    h = jnp.sin(freq * (mm3p(h, ff2) + ff2_b))
    h = jnp.sin(freq * (mm3p(h, ff3) + ff3_b))
    h = mm3p(h, ff_out) * jnp.exp(-t * jnp.abs(decay))
    return h.reshape(T, HY_ORDER, 2, MIX_HALF)


FFT_N2 = LANE


@functools.lru_cache(maxsize=None)
def _dft_tables(n1):
    n2 = FFT_N2
    n = n1 * n2
    i1, i2 = np.arange(n1), np.arange(n2)
    a1 = 2 * np.pi * np.outer(i1, i1) / n1
    a2 = 2 * np.pi * np.outer(i2, i2) / n2
    c1, s1 = np.cos(a1), -np.sin(a1)
    c2, s2 = np.cos(a2), -np.sin(a2)
    atw = 2 * np.pi * np.outer(i2, i1) / n
    f32 = lambda x: np.asarray(x, np.float32)
    return dict(
        wa=f32(np.concatenate([c1, s1], axis=1)),
        wb=f32(np.block([[c2, s2], [-s2, c2]])),
        wb_inv=f32(np.block([[c2, -s2], [s2, c2]])),
        wa_inv=f32(np.concatenate([c1, s1], axis=0)[:, :n1 // 2] / n),
        twr=f32(np.cos(atw)), twi=f32(-np.sin(atw)))


def _fft_fwd(x_t, n1):
    C = x_t.shape[0]
    n2 = FFT_N2
    tb = _dft_tables(n1)
    xa = jnp.transpose(x_t.reshape(C, n1, n2), (0, 2, 1)).reshape(C * n2, n1)
    a = mm3p(xa, tb["wa"]).reshape(C, n2, 2, n1)
    ar, ai = a[:, :, 0], a[:, :, 1]
    br = ar * tb["twr"] - ai * tb["twi"]
    bi = ar * tb["twi"] + ai * tb["twr"]
    b = jnp.concatenate([jnp.transpose(br, (0, 2, 1)), jnp.transpose(bi, (0, 2, 1))], axis=-1)
    return mm3(b.reshape(C * n1, 2 * n2), tb["wb"])


def _fft_inv(y, C, n1):
    n2 = FFT_N2
    tb = _dft_tables(n1)
    g = mm3(y, tb["wb_inv"]).reshape(C, n1, 2, n2)
    gr, gi = g[:, :, 0], g[:, :, 1]
    twr, twi = tb["twr"].T, tb["twi"].T
    hr = gr * twr + gi * twi
    hi = gi * twr - gr * twi
    h = jnp.concatenate([jnp.transpose(hr, (0, 2, 1)), jnp.transpose(hi, (0, 2, 1))], axis=-1)
    out = mm3p(h.reshape(C * n2, 2 * n1), tb["wa_inv"])
    return jnp.transpose(out.reshape(C, n2, n1 // 2), (0, 2, 1)).reshape(C, n1 // 2 * n2)


def _long_conv(z, h_fwd, h_bwd):
    T, C = z.shape
    n1 = 2 * T // FFT_N2
    f = jnp.concatenate([h_fwd, jnp.zeros((1, C), F32), h_bwd[:0:-1]], axis=0)
    zs = _fft_fwd(jnp.pad(z.T, ((0, 0), (0, T))), n1).reshape(C * n1, 2, FFT_N2)
    fs = _fft_fwd(f.T, n1).reshape(C * n1, 2, FFT_N2)
    yr = zs[:, 0] * fs[:, 0] - zs[:, 1] * fs[:, 1]
    yi = zs[:, 0] * fs[:, 1] + zs[:, 1] * fs[:, 0]
    return _fft_inv(jnp.concatenate([yr, yi], axis=-1), C, n1).T


HY_Q = 4
HY_F1Q = FFT_N2 // HY_Q
HY_QR = HY_F1Q * FFT_N2
HY_NN = FFT_N2 * FFT_N2


@functools.lru_cache(maxsize=None)
def _hy_tables():
    n = FFT_N2
    i = np.arange(n)
    w = np.exp(-2j * np.pi * np.outer(i, i) / n)
    tw = np.exp(-2j * np.pi * np.outer(i, i) / (n * n))
    ca = np.conj(w)[: n // 2] / (n * n)
    f32 = lambda x: np.ascontiguousarray(x, dtype=np.float32)
    return dict(
        wa=f32(np.concatenate([w.real, w.imag], axis=0)),
        wbr=f32(w.real), wbi=f32(w.imag),
        twr=f32(tw.real).reshape(n, 1, n), twi=f32(tw.imag).reshape(n, 1, n),
        ctr=f32(tw.real.T).reshape(n, 1, n), cti=f32(-tw.imag.T).reshape(n, 1, n),
        car=f32(ca.real), cai=f32(ca.imag))


def _dot3(a_hi, a_lo, b):
    b_hi, b_lo = _split(b)
    acc = jnp.dot(a_hi, b_hi, preferred_element_type=F32)
    acc += jnp.dot(a_hi, b_lo, preferred_element_type=F32)
    return acc + jnp.dot(a_lo, b_hi, preferred_element_type=F32)


def _hy_fwd_kernel(x_ref, wa_ref, wbr_ref, wbi_ref, twr_ref, twi_ref, o_ref, a_ref, *, k1):
    n = FFT_N2
    q = pl.program_id(1)

    @pl.when(q == 0)
    def _():
        wa_hi, wa_lo = _split(wa_ref[...])
        for t2 in range(n):
            y = _dot3(wa_hi, wa_lo, x_ref[pl.ds(t2, k1, stride=n), :])
            a_ref[pl.ds(t2, n, stride=n), :] = y[:n]
            a_ref[pl.ds(HY_NN + t2, n, stride=n), :] = y[n:]

    cr, ci = wbr_ref[...], wbi_ref[...]

    def body(j, carry):
        f1 = q * HY_F1Q + j
        base = pl.multiple_of(f1 * n, n)
        rhs = jnp.concatenate([a_ref[pl.ds(base, n), :], a_ref[pl.ds(HY_NN + base, n), :]], axis=0)
        twr, twi = twr_ref[f1], twi_ref[f1]
        pr = cr * twr - ci * twi
        pi = cr * twi + ci * twr
        lhs = jnp.concatenate([jnp.concatenate([pr, -pi], axis=1), jnp.concatenate([pi, pr], axis=1)], axis=0)
        out = _dot3(*_split(lhs), rhs)
        ob = pl.multiple_of(j * n, n)
        o_ref[pl.ds(ob, n), :] = out[:n]
        o_ref[pl.ds(HY_QR + ob, n), :] = out[n:]
        return carry

    lax.fori_loop(0, HY_F1Q, body, 0, unroll=2)


def hy_fwd(x):
    T_in, C = x.shape
    k1 = T_in // FFT_N2
    tb = _hy_tables()
    const = lambda a: pl.BlockSpec(a.shape, lambda c, q: (0,) * a.ndim)
    wa = tb["wa"][:, :k1]
    args = (wa, tb["wbr"], tb["wbi"], tb["twr"], tb["twi"])
    return pl.pallas_call(
        functools.partial(_hy_fwd_kernel, k1=k1),
        grid=(C // LANE, HY_Q),
        in_specs=[pl.BlockSpec((T_in, LANE), lambda c, q: (0, c))] + [const(a) for a in args],
        out_specs=pl.BlockSpec((None, 2 * HY_QR, LANE), lambda c, q: (q, 0, c)),
        out_shape=jax.ShapeDtypeStruct((HY_Q, 2 * HY_QR, C), F32),
        scratch_shapes=[pltpu.VMEM((2 * HY_NN, LANE), F32)],
        compiler_params=_params(("parallel", "arbitrary")),
        name="hy_fwd",
    )(x, *args)


def _hy_conv_kernel(z_ref, f_ref, wbr_ref, wbi_ref, ctr_ref, cti_ref, car_ref, cai_ref, y_ref, g_ref):
    n = FFT_N2
    q = pl.program_id(1)
    cr, ci = wbr_ref[...], wbi_ref[...]
    l_hi, l_lo = _split(jnp.concatenate([jnp.concatenate([cr, ci], axis=1),
                                         jnp.concatenate([-ci, cr], axis=1)], axis=0))

    def body(j, carry):
        ob = pl.multiple_of(j * n, n)
        zr, zi = z_ref[pl.ds(ob, n), :], z_ref[pl.ds(HY_QR + ob, n), :]
        fr, fi = f_ref[pl.ds(ob, n), :], f_ref[pl.ds(HY_QR + ob, n), :]
        prod = jnp.concatenate([zr * fr - zi * fi, zr * fi + zi * fr], axis=0)
        out = _dot3(l_hi, l_lo, prod)
        base = pl.multiple_of((q * HY_F1Q + j) * n, n)
        g_ref[pl.ds(base, n), :] = out[:n]
        g_ref[pl.ds(HY_NN + base, n), :] = out[n:]
        return carry

    lax.fori_loop(0, HY_F1Q, body, 0, unroll=2)

    @pl.when(q == HY_Q - 1)
    def _():
        car, cai = car_ref[...], cai_ref[...]
        for t2 in range(n):
            ctr, cti = ctr_ref[t2], cti_ref[t2]
            er = car * ctr - cai * cti
            ei = car * cti + cai * ctr
            rhs = jnp.concatenate([g_ref[pl.ds(t2, n, stride=n), :],
                                   g_ref[pl.ds(HY_NN + t2, n, stride=n), :]], axis=0)
            y_ref[pl.ds(t2, n // 2, stride=n), :] = _dot3(*_split(jnp.concatenate([er, -ei], axis=1)), rhs)


def hy_conv(zs, fs, f_block=0):
    C = zs.shape[-1]
    T = HY_NN // 2
    tb = _hy_tables()
    const = lambda a: pl.BlockSpec(a.shape, lambda c, q: (0,) * a.ndim)
    spec = pl.BlockSpec((None, 2 * HY_QR, LANE), lambda c, q: (q, 0, c))
    fspec = pl.BlockSpec((None, 2 * HY_QR, LANE), lambda c, q: (q, 0, c + f_block))
    args = (tb["wbr"], tb["wbi"], tb["ctr"], tb["cti"], tb["car"], tb["cai"])
    return pl.pallas_call(
        _hy_conv_kernel,
        grid=(C // LANE, HY_Q),
        in_specs=[spec, fspec] + [const(a) for a in args],
        out_specs=pl.BlockSpec((T, LANE), lambda c, q: (0, c)),
        out_shape=jax.ShapeDtypeStruct((T, C), F32),
        scratch_shapes=[pltpu.VMEM((2 * HY_NN, LANE), F32)],
        compiler_params=_params(("parallel", "arbitrary")),
        name="hy_conv",
    )(zs, fs, *args)


def hyena_filter_taps(T, ff1, ff1_b, ff2, ff2_b, ff3, ff3_b, ff_out, freq, decay):
    t = jnp.linspace(0.0, 1.0, T, dtype=F32)[:, None]
    bands = (HY_EMB - 1) // 2
    f = jnp.linspace(1e-4, bands - 1, bands, dtype=F32)[None, :]
    w = (2 * math.pi / T) * jnp.arange(T, dtype=F32)[:, None]
    z = jnp.concatenate([t, jnp.cos(f * w), -jnp.sin(f * w)], axis=-1)
    h = jnp.sin(freq * (mm3p(z, ff1) + ff1_b))
    h = jnp.sin(freq * (mm3p(h, ff2) + ff2_b))
    h = jnp.sin(freq * (mm3p(h, ff3) + ff3_b))
    w_out = ff_out.reshape(-1, HY_ORDER, 2, MIX_HALF)
    dec = jnp.abs(decay).reshape(HY_ORDER, 2, MIX_HALF)
    side = lambda s: (w_out[:, :, s].reshape(-1, HY_ORDER * MIX_HALF), dec[:, s].reshape(HY_ORDER * MIX_HALF))
    (w_f, d_f), (w_b, d_b) = side(0), side(1)
    fwd = mm3p(h, w_f) * jnp.exp(-t * d_f)
    bwd = mm3p(h[::-1], w_b) * jnp.exp(-t[::-1] * d_b)
    return jnp.concatenate([fwd, jnp.zeros((1, fwd.shape[1]), F32), bwd[:T - 1]], axis=0)


@functools.lru_cache(maxsize=None)
def _dft_small(n):
    i = np.arange(n)
    w = np.exp(-2j * np.pi * np.outer(i, i) / n)
    fwd = np.concatenate([w.real, w.imag], axis=0)
    inv = np.concatenate([w.real, w.imag], axis=1)[: n // 2] / n
    return np.asarray(fwd, np.float32), np.asarray(inv, np.float32)


def _short_conv(z, h_fwd, h_bwd):
    T, C = z.shape
    n = 2 * T
    fwd, inv = _dft_small(n)
    f = jnp.concatenate([h_fwd, jnp.zeros((1, C), F32), h_bwd[:0:-1]], axis=0)
    zs = mm3(fwd[:, :T], z)
    fs = mm3(fwd, f)
    prod = jnp.concatenate([zs[:n] * fs[:n] - zs[n:] * fs[n:], zs[:n] * fs[n:] + zs[n:] * fs[:n]], axis=0)
    return mm3(inv, prod)


def hyena_seq(p, short_w, short_b, conv, bias):
    u = short_w[0] * _shift_prev(p) + short_w[1] * p + short_w[2] * _shift_next(p) + short_b
    z, x1, x2 = jnp.split(u, 3, axis=-1)
    for n, gate in enumerate((x1, x2)):
        z = gate * (conv(z, n) + bias[n] * z)
    return z


def hyena_mix(p, short_w, short_b, filt_params, bias):
    T = p.shape[0] - CTX_LEN
    filt_c = hyena_filters(CTX_LEN, *filt_params)
    out_c = hyena_seq(p[:CTX_LEN], short_w, short_b,
                      lambda z, n: _short_conv(z, filt_c[:, n, 0], filt_c[:, n, 1]), bias)
    if 2 * T == HY_NN:
        spectra = hy_fwd(hyena_filter_taps(T, *filt_params))
        conv = lambda z, n: hy_conv(hy_fwd(z), spectra, n * MIX_HALF // LANE)
    else:
        filt = hyena_filters(T, *filt_params)
        conv = lambda z, n: _long_conv(z, filt[:, n, 0], filt[:, n, 1])
    out = hyena_seq(p[CTX_LEN:], short_w, short_b, conv, bias)
    return jnp.concatenate([out_c, out], axis=0)


def kernel(x, c, ctx, c_ctx, ada_down, ada_up, ada_bias, norm_g, ffn_w1, ffn_w3, ffn_w2, ab_w_in, ab_w_out, rwkv_mu, rwkv_w0, rwkv_w2, rwkv_a0, rwkv_a2, rwkv_g2, rwkv_kk, rwkv_ka, rwkv_rk, rwkv_gn, na_rpb, cd_w_in, cd_w_out, ml_gate_b, ml_norm, hy_short_w, hy_short_b, hy_ff1, hy_ff1_b, hy_ff2, hy_ff2_b, hy_ff3, hy_ff3_b, hy_ff_out, hy_freq, hy_decay, hy_bias):
    assert x.shape[0] == 1 and ctx.shape[1] == CTX_LEN
    X = jnp.concatenate([ctx[0], x[0]], axis=0)
    s2 = _pad_to(jnp.stack([jax.nn.silu(c_ctx), jax.nn.silu(c[0])]), 0, 16)
    mods = [(mm(mm(s2, ada_down[l]), ada_up[l])[:2] + ada_bias[l]).reshape(2, N_SUB, 3, D_MODEL)
            for l in range(DEPTH)]
    hn = norm_mod(X, norm_g[0, 0], mods[0][:, 0, 0], mods[0][:, 0, 1])
    for l in range(DEPTH):
        i = l // 2
        mod = mods[l]
        g = norm_g[l]
        y = mm_w32(swiglu(hn, ffn_w1[l, 0], ffn_w3[l, 0]), ffn_w2[l, 0])
        X, xn = post_res_norm(X, y, g[1], mod[:, 0, 2], FFN_RES, g[2], mod[:, 1, 0], mod[:, 1, 1])
        if l % 2 == 0:
            w_ab = ab_w_in[i].astype(BF16)
            w_rkv = [_kmajor(t) for t in jnp.split(w_ab[:, :AB_WD], 3, axis=-1)]
            p = mm(xn, _pack_cols(jnp.concatenate(w_rkv + [w_ab[:, AB_WD:]], axis=-1), _AB_CUTS, _AB_WIDTHS))
            ya = rwkv_mix(p, rwkv_mu[i], rwkv_w0[i], rwkv_w2[i], rwkv_a0[i], rwkv_a2[i],
                          rwkv_g2[i], rwkv_kk[i], rwkv_ka[i], rwkv_rk[i], rwkv_gn[i])
            y = jnp.concatenate([ya, na_mix(p, na_rpb[i])], axis=-1)
            w_out = ab_w_out[i].astype(BF16)
            w_rw = jnp.swapaxes(w_out[:MIX_HALF].reshape(RWKV_HEADS, RWKV_HEAD, D_MODEL), 0, 1)
            yo = mm(y, jnp.concatenate([w_rw.reshape(MIX_HALF, D_MODEL), w_out[MIX_HALF:]], axis=0))
        else:
            w_in = _pack_cols(cd_w_in[i].astype(BF16), (CD_GT, ML_COLS), (CD_GT, LANE, P_COLS - CD_HY))
            p = mm(xn, w_in)
            filt_params = (hy_ff1[i], hy_ff1_b[i], hy_ff2[i], hy_ff2_b[i], hy_ff3[i], hy_ff3_b[i],
                           hy_ff_out[i], hy_freq[i], hy_decay[i])
            yd = hyena_mix(p[:, CD_HY:CD_HY + 3 * MIX_HALF], hy_short_w[i], hy_short_b[i], filt_params,
                           hy_bias[i])
            y = jnp.concatenate([mlstm_mix(p, ml_gate_b[i], ml_norm[i]), yd], axis=-1)
            yo = mm_w32(y.astype(BF16), cd_w_out[i])
        X, hn = post_res_norm(X, yo, g[3], mod[:, 1, 2], 1.0, g[4], mod[:, 2, 0], mod[:, 2, 1])
        y = mm_w32(swiglu(hn, ffn_w1[l, 1], ffn_w3[l, 1]), ffn_w2[l, 1])
        if l + 1 < DEPTH:
            nxt = mods[l + 1]
            X, hn = post_res_norm(X, y, g[5], mod[:, 2, 2], FFN_RES, norm_g[l + 1, 0], nxt[:, 0, 0], nxt[:, 0, 1])
        else:
            X = post_res(X, y, g[5], mod[:, 2, 2], FFN_RES)
    return X[CTX_LEN:][None]
```

```python
import functools
import math

import jax
import jax.numpy as jnp
import numpy as np
from jax import lax
from jax.experimental import pallas as pl
from jax.experimental.pallas import tpu as pltpu

F32 = jnp.float32
BF16 = jnp.bfloat16

D_MODEL = 4096
DEPTH = 4
GRID_W = 64
CTX_LEN = 256
N_SUB = 3
FFN_RES = 0.5
D_FF = 5632
RMS_EPS = 1e-6
MIX_HALF = D_MODEL // 2

RWKV_HEAD = 64
RWKV_HEADS = MIX_HALF // RWKV_HEAD
RWKV_DECAY_LORA = 96
RWKV_A_LORA = 96
RWKV_GATE_LORA = 256
RWKV_COLS = 3 * MIX_HALF + RWKV_DECAY_LORA + RWKV_A_LORA + RWKV_GATE_LORA
RWKV_GN_EPS = 64e-5

NA_HEAD_DIM = 128
NA_HEADS = MIX_HALF // NA_HEAD_DIM
NA_KH = 8
NA_KW = 16

ML_HEADS = 4
ML_DV = MIX_HALF // ML_HEADS
ML_DQK = ML_DV // 2
ML_COLS = 2 * ML_HEADS * ML_DQK + 2 * MIX_HALF + 4 * ML_HEADS
ROPE_BASE = 10000.0

HY_ORDER = 2
HY_EMB = 33

LANE = 128
ROW_TILE = CTX_LEN
VMEM_LIMIT = 56 * 1024 * 1024
P_COLS = 12800
NEG = -1e30


def _params(sem):
    return pltpu.CompilerParams(dimension_semantics=sem, vmem_limit_bytes=VMEM_LIMIT)


def _pick(n, cands):
    for c in cands:
        if n % c == 0:
            return c
    return n


def _mm_kernel(a_ref, b_ref, o_ref):
    o_ref[...] = jnp.dot(a_ref[...], b_ref[...], preferred_element_type=F32).astype(o_ref.dtype)


def _split(x):
    hi = x.astype(BF16)
    lo = (x - hi.astype(F32)).astype(BF16)
    return hi, lo


def _mm3_kernel(a_ref, b_ref, o_ref):
    a_hi, a_lo = _split(a_ref[...])
    b_hi, b_lo = _split(b_ref[...])
    acc = jnp.dot(a_hi, b_hi, preferred_element_type=F32)
    acc += jnp.dot(a_hi, b_lo, preferred_element_type=F32)
    acc += jnp.dot(a_lo, b_hi, preferred_element_type=F32)
    o_ref[...] = acc


def _mm_call(kern, a, b, out_dtype, name):
    M, K = a.shape
    N = b.shape[1]
    tm = _pick(M, (1024, 768, 512, 256, 128, 64, 32, 16, 8))
    tn = _pick(N, (512, 256, 128))
    return pl.pallas_call(
        kern,
        grid=(M // tm, N // tn),
        in_specs=[pl.BlockSpec((tm, K), lambda i, j: (i, 0)),
                  pl.BlockSpec((K, tn), lambda i, j: (0, j))],
        out_specs=pl.BlockSpec((tm, tn), lambda i, j: (i, j)),
        out_shape=jax.ShapeDtypeStruct((M, N), out_dtype),
        compiler_params=_params(("parallel", "parallel")),
        name=name,
    )(a, b)


def mm(a, b, out_dtype=F32):
    return _mm_call(_mm_kernel, a.astype(BF16), b.astype(BF16), out_dtype, "mm")


def mm3(a, b):
    return _mm_call(_mm3_kernel, a.astype(F32), b.astype(F32), F32, "mm3")


def _mm_w32_kernel(a_ref, w_ref, o_ref, wb_ref):
    @pl.when(pl.program_id(1) == 0)
    def _():
        wb_ref[...] = w_ref[...].astype(BF16)

    o_ref[...] = jnp.dot(a_ref[...], wb_ref[...], preferred_element_type=F32).astype(o_ref.dtype)


def _swiglu_kernel(a_ref, w1_ref, w3_ref, o_ref, w1b_ref, w3b_ref):
    @pl.when(pl.program_id(1) == 0)
    def _():
        w1b_ref[...] = w1_ref[...].astype(BF16)
        w3b_ref[...] = w3_ref[...].astype(BF16)

    a = a_ref[...]
    h1 = jnp.dot(a, w1b_ref[...], preferred_element_type=F32)
    h3 = jnp.dot(a, w3b_ref[...], preferred_element_type=F32)
    o_ref[...] = (h1 * jax.nn.sigmoid(h1) * h3).astype(o_ref.dtype)


def _w32_call(kern, a, ws, tn, out_dtype, name, lead=()):
    M, K = a.shape
    N = ws[0].shape[-1]
    tm = _pick(M, (768, 512, 256))
    wspec = pl.BlockSpec((None,) * len(lead) + (K, tn), lambda j, i: tuple(lead) + (0, j))
    return pl.pallas_call(
        kern,
        grid=(N // tn, M // tm),
        in_specs=[pl.BlockSpec((tm, K), lambda j, i: (i, 0))] + [wspec] * len(ws),
        out_specs=pl.BlockSpec((tm, tn), lambda j, i: (i, j)),
        out_shape=jax.ShapeDtypeStruct((M, N), out_dtype),
        scratch_shapes=[pltpu.VMEM((K, tn), BF16)] * len(ws),
        compiler_params=_params(("parallel", "arbitrary")),
        name=name,
    )(a, *ws)


def swiglu(a, w1, w3, lead=()):
    return _w32_call(_swiglu_kernel, a, (w1, w3), 256, BF16, "swiglu", lead)


def mm_w32(a, w, lead=()):
    return _w32_call(_mm_w32_kernel, a, (w,), 512, F32, "mm_w32", lead)


def _group_spec(d):
    return pl.BlockSpec((1, 1, d), lambda i: (jnp.minimum(i, 1), 0, 0))


def _norm_mod_kernel(x_ref, g_ref, sh_ref, sc_ref, o_ref):
    x = x_ref[...]
    y = x * lax.rsqrt(jnp.mean(x * x, axis=-1, keepdims=True) + RMS_EPS) * g_ref[...]
    o_ref[...] = (y * (1 + sc_ref[0]) + sh_ref[0]).astype(o_ref.dtype)


def norm_mod(x, g, shift, scale):
    R, D = x.shape
    row = pl.BlockSpec((ROW_TILE, D), lambda i: (i, 0))
    return pl.pallas_call(
        _norm_mod_kernel,
        grid=(R // ROW_TILE,),
        in_specs=[row, pl.BlockSpec((1, D), lambda i: (0, 0)), _group_spec(D), _group_spec(D)],
        out_specs=row,
        out_shape=jax.ShapeDtypeStruct((R, D), BF16),
        compiler_params=_params(("parallel",)),
        name="norm_mod",
    )(x, g.reshape(1, D), shift.reshape(2, 1, D), scale.reshape(2, 1, D))


def _post_res_kernel(x_ref, y_ref, g_ref, gate_ref, o_ref, *, coef):
    y = y_ref[...]
    yn = y * lax.rsqrt(jnp.mean(y * y, axis=-1, keepdims=True) + RMS_EPS) * g_ref[...]
    o_ref[...] = x_ref[...] + coef * gate_ref[0] * yn


def _post_res_norm_kernel(x_ref, y_ref, g_ref, gate_ref, gn_ref, sh_ref, sc_ref, o_ref, h_ref, *, coef):
    y = y_ref[...]
    yn = y * lax.rsqrt(jnp.mean(y * y, axis=-1, keepdims=True) + RMS_EPS) * g_ref[...]
    x = x_ref[...] + coef * gate_ref[0] * yn
    o_ref[...] = x
    h = x * lax.rsqrt(jnp.mean(x * x, axis=-1, keepdims=True) + RMS_EPS) * gn_ref[...]
    h_ref[...] = (h * (1 + sc_ref[0]) + sh_ref[0]).astype(h_ref.dtype)


def post_res_norm(x, y, g, gate, coef, g_next, shift_next, scale_next):
    R, D = x.shape
    row = pl.BlockSpec((ROW_TILE, D), lambda i: (i, 0))
    vec = pl.BlockSpec((1, D), lambda i: (0, 0))
    grp = lambda t: t.reshape(2, 1, D)
    return pl.pallas_call(
        functools.partial(_post_res_norm_kernel, coef=coef),
        grid=(R // ROW_TILE,),
        in_specs=[row, row, vec, _group_spec(D), vec, _group_spec(D), _group_spec(D)],
        out_specs=[row, row],
        out_shape=[jax.ShapeDtypeStruct((R, D), F32), jax.ShapeDtypeStruct((R, D), BF16)],
        compiler_params=_params(("parallel",)),
        name="post_res_norm",
    )(x, y, g.reshape(1, D), grp(gate), g_next.reshape(1, D), grp(shift_next), grp(scale_next))


def post_res(x, y, g, gate, coef):
    R, D = x.shape
    row = pl.BlockSpec((ROW_TILE, D), lambda i: (i, 0))
    return pl.pallas_call(
        functools.partial(_post_res_kernel, coef=coef),
        grid=(R // ROW_TILE,),
        in_specs=[row, row, pl.BlockSpec((1, D), lambda i: (0, 0)), _group_spec(D)],
        out_specs=row,
        out_shape=jax.ShapeDtypeStruct((R, D), F32),
        compiler_params=_params(("parallel",)),
        name="post_res",
    )(x, y, g.reshape(1, D), gate.reshape(2, 1, D))


WKV_TB = 8
WKV_KG = MIX_HALF // LANE
WKV_VC = RWKV_HEAD // 8
WKV_CG = 2
PREP_TB = 128


def _kmajor(w):
    lead = w.shape[:-1]
    return jnp.swapaxes(w.reshape(*lead, RWKV_HEADS, RWKV_HEAD), -1, -2).reshape(*lead, MIX_HALF)


def _fold_lanes(s):
    s = s + pltpu.roll(s, 64, 1)
    return s + pltpu.roll(s, 32, 1)


def _rwkv_prep_kernel(p_ref, hp_ref, hn_ref, mu_ref, w2_ref, a2_ref, g2_ref, vec_ref,
                      a_out, b0_out, b1_out, k0_out, k1_out, w0_out, w1_out, r_out, v_out, g_out, bonus_out):
    W = MIX_HALF
    i = pl.program_id(0)
    cb = CTX_LEN // PREP_TB
    has_prev = jnp.logical_and(i != 0, i != cb).astype(F32)
    has_next = jnp.logical_and(i != cb - 1, i != pl.num_programs(0) - 1).astype(F32)
    row = lax.broadcasted_iota(jnp.int32, (PREP_TB, 1), 0)

    def shifted(c0, width):
        x = p_ref[:, c0:c0 + width]
        up = jnp.where(row == 0, hp_ref[7:8, c0:c0 + width] * has_prev, pltpu.roll(x, 1, 0))
        dn = jnp.where(row == PREP_TB - 1, hn_ref[0:1, c0:c0 + width] * has_next,
                       pltpu.roll(x, PREP_TB - 1, 0))
        return x + mu_ref[0:1, c0:c0 + width] * (up - x) + mu_ref[1:2, c0:c0 + width] * (dn - x)

    lo = shifted(AB_WD, AB_NA - AB_WD)
    wd, ad, gd = lo[:, :LANE], lo[:, LANE:2 * LANE], lo[:, 2 * LANE:]
    wlo = jnp.dot(jnp.tanh(wd).astype(BF16), w2_ref[...], preferred_element_type=F32)
    alo = jnp.dot(ad.astype(BF16), a2_ref[...], preferred_element_type=F32)
    g_out[...] = jnp.dot(jax.nn.sigmoid(gd).astype(BF16), g2_ref[...], preferred_element_type=F32)

    ss = None
    for g in range(WKV_KG):
        sl = slice(g * LANE, (g + 1) * LANE)
        kk = shifted(W + g * LANE, LANE) * vec_ref[4:5, sl]
        ss = kk * kk if ss is None else ss + kk * kk
    nrm = jnp.maximum(jnp.sqrt(_fold_lanes(ss)), 1e-12)
    bonus = None
    for g in range(WKV_KG):
        sl = slice(g * LANE, (g + 1) * LANE)
        r = shifted(g * LANE, LANE)
        k = shifted(W + g * LANE, LANE)
        r_out[:, sl] = r
        v_out[:, sl] = shifted(2 * W + g * LANE, LANE)
        kk = k * vec_ref[4:5, sl] / nrm
        a_out[:, sl] = -kk
        for d, (w_out, k_out, b_out) in enumerate(((w0_out, k0_out, b0_out), (w1_out, k1_out, b1_out))):
            dl = slice(d * W + g * LANE, d * W + (g + 1) * LANE)
            logw = -jax.nn.softplus(-(vec_ref[d:d + 1, sl] + wlo[:, dl])) - 0.5
            w_out[:, sl] = jnp.exp(-jnp.exp(logw))
            a = jax.nn.sigmoid(vec_ref[2 + d:3 + d, sl] + alo[:, dl])
            kd = k * (1 + (a - 1) * vec_ref[5:6, sl])
            k_out[:, sl] = kd
            b_out[:, sl] = kk * a
            term = r * kd * vec_ref[6:7, sl]
            bonus = term if bonus is None else bonus + term
    bonus_out[...] = _fold_lanes(bonus)


def rwkv_prep(p, mu, w2c, a2c, g2, vec):
    R = p.shape[0]
    W = MIX_HALF
    hb = PREP_TB // 8
    full = lambda a: pl.BlockSpec(a.shape, lambda i: (0,) * a.ndim)
    wide = pl.BlockSpec((PREP_TB, W), lambda i: (i, 0))
    return pl.pallas_call(
        _rwkv_prep_kernel,
        grid=(R // PREP_TB,),
        in_specs=[pl.BlockSpec((PREP_TB, AB_NA), lambda i: (i, 0)),
                  pl.BlockSpec((8, AB_NA), lambda i: (jnp.maximum(i * hb - 1, 0), 0)),
                  pl.BlockSpec((8, AB_NA), lambda i: (jnp.minimum((i + 1) * hb, R // 8 - 1), 0)),
                  full(mu), full(w2c), full(a2c), full(g2), full(vec)],
        out_specs=[wide] * 10 + [pl.BlockSpec((PREP_TB, LANE), lambda i: (i, 0))],
        out_shape=[jax.ShapeDtypeStruct((R, W), F32)] * 10 + [jax.ShapeDtypeStruct((R, LANE), F32)],
        compiler_params=_params(("parallel",)),
        name="rwkv_prep",
    )(p, p, p, mu, w2c, a2c, g2, vec)


def _wkv_kernel(a0_ref, b0_ref, k0_ref, w0_ref, r0_ref, v0_ref, a1_ref, b1_ref, k1_ref, w1_ref, r1_ref, v1_ref,
                y0_ref, y1_ref, s_ref):
    @pl.when(pl.program_id(0) == 0)
    def _():
        s_ref[...] = jnp.zeros_like(s_ref)

    dirs = ((a0_ref, b0_ref, k0_ref, w0_ref, r0_ref, v0_ref, y0_ref),
            (a1_ref, b1_ref, k1_ref, w1_ref, r1_ref, v1_ref, y1_ref))

    for t in range(WKV_TB):
        for d, (a_ref, b_ref, k_ref, w_ref, r_ref, v_ref, y_ref) in enumerate(dirs):
            tt = t if d == 0 else WKV_TB - 1 - t
            row = lambda ref, g: ref[tt:tt + 1, g * LANE:(g + 1) * LANE]
            for c0 in range(0, WKV_VC, WKV_CG):
                cs = range(c0, c0 + WKV_CG)
                acc = {c: None for c in cs}
                for g in range(WKV_KG):
                    ar = row(a_ref, g)
                    for c in cs:
                        term = s_ref[d, g, pl.ds(c * 8, 8), :] * ar
                        acc[c] = term if acc[c] is None else acc[c] + term
                sa = {c: _fold_lanes(acc[c]) for c in cs}
                vv = {c: v_ref[tt, pl.ds(c * 8, 8), :] for c in cs}
                acc = {c: None for c in cs}
                for g in range(WKV_KG):
                    wr, br, kr, rr = row(w_ref, g), row(b_ref, g), row(k_ref, g), row(r_ref, g)
                    for c in cs:
                        s_new = s_ref[d, g, pl.ds(c * 8, 8), :] * wr + sa[c] * br + vv[c] * kr
                        s_ref[d, g, pl.ds(c * 8, 8), :] = s_new
                        term = s_new * rr
                        acc[c] = term if acc[c] is None else acc[c] + term
                for c in cs:
                    y_ref[tt, pl.ds(c * 8, 8), :] = _fold_lanes(acc[c])


def wkv_scan(a, b0, b1, k0, k1, w0, w1, r, vp):
    R = a.shape[0]
    nb = R // WKV_TB
    cb = CTX_LEN // WKV_TB
    fwd = lambda i: i
    bwd = lambda i: jnp.where(i < cb, cb - 1 - i, nb + cb - 1 - i)
    kspec = lambda f: pl.BlockSpec((WKV_TB, MIX_HALF), lambda i: (f(i), 0))
    vspec = lambda f: pl.BlockSpec((WKV_TB, RWKV_HEAD, LANE), lambda i: (f(i), 0, 0))
    yshape = jax.ShapeDtypeStruct((R, RWKV_HEAD, LANE), F32)
    return pl.pallas_call(
        _wkv_kernel,
        grid=(nb,),
        in_specs=[kspec(fwd)] * 5 + [vspec(fwd)] + [kspec(bwd)] * 5 + [vspec(bwd)],
        out_specs=[vspec(fwd), vspec(bwd)],
        out_shape=[yshape, yshape],
        scratch_shapes=[pltpu.VMEM((2, WKV_KG, RWKV_HEAD, LANE), F32)],
        compiler_params=_params(("arbitrary",)),
        name="wkv_scan",
    )(a, b0, k0, w0, r, vp, a, b1, k1, w1, r, vp)


def _shift_prev(y):
    return jnp.pad(y, ((1, 0), (0, 0)))[:-1]


def _shift_next(y):
    return jnp.pad(y, ((0, 1), (0, 0)))[1:]


AB_WD = 3 * MIX_HALF
AB_AD = AB_WD + LANE
AB_GD = AB_AD + LANE
AB_NA = AB_GD + RWKV_GATE_LORA


def _pack_cols(w, cuts, widths):
    pieces = jnp.split(w, cuts, axis=-1)
    pad = [(0, 0)] * (w.ndim - 1)
    return jnp.concatenate([jnp.pad(p, pad + [(0, wd - p.shape[-1])]) for p, wd in zip(pieces, widths)],
                           axis=-1)


_AB_CUTS = (AB_WD, AB_WD + RWKV_DECAY_LORA, AB_WD + RWKV_DECAY_LORA + RWKV_A_LORA, RWKV_COLS)
_AB_WIDTHS = (AB_WD, LANE, LANE, RWKV_GATE_LORA, 3 * MIX_HALF)


def rwkv_mix(p, mu, w0, w2, a0, a2, g2, k_k, k_a, r_k, gn):
    R = p.shape[0]
    km3 = lambda t: jnp.concatenate([_kmajor(x) for x in jnp.split(t, 3, axis=-1)], axis=-1)
    mu_p = _pack_cols(mu, _AB_CUTS[:3], _AB_WIDTHS[:4])
    mu_p = jnp.concatenate([km3(mu_p[:, :AB_WD]), mu_p[:, AB_WD:]], axis=-1)
    lora = lambda w: jnp.pad(jnp.concatenate([_kmajor(w[0]), _kmajor(w[1])], axis=-1),
                             ((0, LANE - w.shape[1]), (0, 0))).astype(BF16)
    vec = jnp.stack([_kmajor(t) for t in (w0[0], w0[1], a0[0], a0[1], k_k, k_a, r_k, jnp.zeros_like(r_k))])
    a, b0, b1, k0, k1, d0, d1, r, v, g, bonus = rwkv_prep(p, mu_p, lora(w2), lora(a2),
                                                          _kmajor(g2).astype(BF16), vec)
    tile = lambda t: t.reshape(R, RWKV_HEAD, RWKV_HEADS)
    vp = jnp.broadcast_to(tile(v)[:, :, None, :], (R, RWKV_HEAD, 4, RWKV_HEADS)).reshape(R, RWKV_HEAD, LANE)
    y0, y1 = wkv_scan(a, b0, b1, k0, k1, d0, d1, r, vp)
    y = (y0 + y1)[:, :, :RWKV_HEADS]
    mean = jnp.mean(y, axis=1, keepdims=True)
    var = jnp.mean(jnp.square(y - mean), axis=1, keepdims=True)
    gn_t = _kmajor(gn).reshape(2, RWKV_HEAD, RWKV_HEADS)
    yn = (y - mean) * lax.rsqrt(var + RWKV_GN_EPS) * gn_t[0] + gn_t[1]
    out = (yn + bonus[:, None, :RWKV_HEADS] * tile(v)) * tile(g)
    return out.reshape(R, MIX_HALF)


NA_WIN = NA_KH * GRID_W


NA_RB = 4


def _na_kernel(q_ref, k_ref, v_ref, bias_ref, o_ref, *, rows):
    dn = (((1,), (1,)), ((), ()))
    k_ctx = k_ref[pl.ds(0, CTX_LEN), :].astype(BF16)
    v_ctx = v_ref[pl.ds(0, CTX_LEN), :].astype(BF16)
    for j in range(NA_RB):
        r = pl.program_id(1) * NA_RB + j
        r0 = jnp.clip(r - NA_KH // 2, 0, rows - NA_KH)
        start = pl.multiple_of(CTX_LEN + r0 * GRID_W, GRID_W)
        qs = slice(j * GRID_W, (j + 1) * GRID_W)
        q = (q_ref[qs, :] * NA_HEAD_DIM ** -0.5).astype(BF16)
        k_win = k_ref[pl.ds(start, NA_WIN), :].astype(BF16)
        s_loc = lax.dot_general(q, k_win, dn, preferred_element_type=F32) + bias_ref[0, r - r0]
        s_ctx = lax.dot_general(q, k_ctx, dn, preferred_element_type=F32)
        m = jnp.maximum(jnp.max(s_loc, axis=-1, keepdims=True), jnp.max(s_ctx, axis=-1, keepdims=True))
        p_loc = jnp.exp(s_loc - m)
        p_ctx = jnp.exp(s_ctx - m)
        den = jnp.sum(p_loc, axis=-1, keepdims=True) + jnp.sum(p_ctx, axis=-1, keepdims=True)
        num = jnp.dot(p_loc.astype(BF16), v_ref[pl.ds(start, NA_WIN), :].astype(BF16),
                      preferred_element_type=F32)
        num += jnp.dot(p_ctx.astype(BF16), v_ctx, preferred_element_type=F32)
        o_ref[qs, :] = num / den


def _na_ctx_kernel(q_ref, k_ref, v_ref, o_ref):
    q = (q_ref[...] * NA_HEAD_DIM ** -0.5).astype(BF16)
    s = lax.dot_general(q, k_ref[...].astype(BF16), (((1,), (1,)), ((), ())), preferred_element_type=F32)
    p = jnp.exp(s - jnp.max(s, axis=-1, keepdims=True))
    num = jnp.dot(p.astype(BF16), v_ref[...].astype(BF16), preferred_element_type=F32)
    o_ref[...] = num / jnp.sum(p, axis=-1, keepdims=True)


def _na_bias(rpb, rows):
    kh = min(NA_KH, rows)
    col = np.arange(GRID_W)
    c0 = np.clip(col - NA_KW // 2, 0, GRID_W - NA_KW)
    key = np.arange(GRID_W)
    inside = (key[None, :] >= c0[:, None]) & (key[None, :] < c0[:, None] + NA_KW)
    dc = np.clip(key[None, :] - col[:, None] + (NA_KW - 1), 0, 2 * NA_KW - 2)
    onehot = np.zeros((GRID_W, GRID_W, 2 * NA_KW - 1), np.float32)
    onehot[col[:, None], key[None, :], dc] = 1.0
    toep = jnp.einsum("qcd,hrd->hrqc", onehot, rpb, precision=lax.Precision.HIGHEST)
    toep = jnp.where(jnp.asarray(inside), toep, NEG)
    return jnp.stack([jnp.concatenate([toep[:, i - j + NA_KH - 1] for i in range(kh)], axis=-1)
                      for j in range(NA_KH)], axis=1)


def na_mix(p, rpb):
    R = p.shape[0]
    T = R - CTX_LEN
    rows = T // GRID_W
    qb, kb, vb = [(AB_NA + i * MIX_HALF) // NA_HEAD_DIM for i in range(3)]
    qrows = NA_RB * GRID_W
    cb = CTX_LEN // qrows
    bias = _na_bias(rpb, rows)
    strip = lambda b: pl.BlockSpec((R, NA_HEAD_DIM), lambda h, r: (0, b + h))
    lat = pl.pallas_call(
        functools.partial(_na_kernel, rows=rows),
        grid=(NA_HEADS, rows // NA_RB),
        in_specs=[pl.BlockSpec((qrows, NA_HEAD_DIM), lambda h, r: (cb + r, qb + h)),
                  strip(kb), strip(vb),
                  pl.BlockSpec((1, NA_KH, GRID_W, NA_WIN), lambda h, r: (h, 0, 0, 0))],
        out_specs=pl.BlockSpec((qrows, NA_HEAD_DIM), lambda h, r: (r, h)),
        out_shape=jax.ShapeDtypeStruct((T, MIX_HALF), F32),
        compiler_params=_params(("parallel", "arbitrary")),
        name="na_latent",
    )(p, p, p, bias)
    blk = lambda b: pl.BlockSpec((CTX_LEN, NA_HEAD_DIM), lambda h: (0, b + h))
    ctx = pl.pallas_call(
        _na_ctx_kernel,
        grid=(NA_HEADS,),
        in_specs=[blk(qb), blk(kb), blk(vb)],
        out_specs=pl.BlockSpec((CTX_LEN, NA_HEAD_DIM), lambda h: (0, h)),
        out_shape=jax.ShapeDtypeStruct((CTX_LEN, MIX_HALF), F32),
        compiler_params=_params(("parallel",)),
        name="na_context",
    )(p, p, p)
    return jnp.concatenate([ctx, lat], axis=0)


ML_CHUNK = CTX_LEN
CD_GT = 2 * ML_HEADS * ML_DQK + 2 * MIX_HALF
CD_HY = CD_GT + LANE


def _mlstm_kernel(q_ref, k_ref, v_ref, lir_ref, lfr_ref, lic_ref, lfc_ref, o_ref, c_ref, n_ref, m_ref):
    L = ML_CHUNK

    @pl.when(pl.program_id(2) == 0)
    def _():
        c_ref[...] = jnp.zeros_like(c_ref)
        n_ref[...] = jnp.zeros_like(n_ref)
        m_ref[...] = jnp.zeros_like(m_ref)

    sgn = 1 - 2 * pl.program_id(0)
    row = lax.broadcasted_iota(jnp.int32, (L, L), 0)
    col = lax.broadcasted_iota(jnp.int32, (L, L), 1)
    seen = (row - col) * sgn >= 0
    seen_t = (col - row) * sgn >= 0
    q = q_ref[...]
    k = k_ref[...]
    vb = v_ref[...].astype(BF16)
    li_r, lf_r, li_c, lf_c = lir_ref[0], lfr_ref[0], lic_ref[0], lfc_ref[0]
    b_c = jnp.sum(jnp.where(seen, lf_r, 0.0), axis=1, keepdims=True)
    b_r = jnp.sum(jnp.where(seen_t, lf_c, 0.0), axis=0, keepdims=True)
    m_prev = m_ref[...]
    dmat = jnp.where(seen, b_c - b_r + li_r, NEG)
    inter = b_c + m_prev
    m_t = jnp.maximum(jnp.max(dmat, axis=1, keepdims=True), inter)
    qb = q.astype(BF16)
    s = lax.dot_general(qb, k.astype(BF16), (((1,), (1,)), ((), ())), preferred_element_type=F32)
    s = s * jnp.exp(dmat - m_t)
    dec = jnp.exp(inter - m_t)
    num = jnp.dot(s.astype(BF16), vb, preferred_element_type=F32)
    num += dec * jnp.dot(qb, c_ref[...].astype(BF16), preferred_element_type=F32)
    den = jnp.sum(s, axis=1, keepdims=True) + dec * jnp.sum(q * n_ref[...], axis=1, keepdims=True)
    o_ref[0] = num / jnp.maximum(jnp.abs(den), jnp.exp(-m_t))
    total = jnp.sum(lf_r, axis=1, keepdims=True)
    gl = total - b_c + li_c
    m_new = jnp.maximum(total + m_prev, jnp.max(gl, axis=0, keepdims=True))
    kw = k * jnp.exp(gl - m_new)
    sc = jnp.exp(total + m_prev - m_new)
    c_ref[...] = sc * c_ref[...] + lax.dot_general(kw.astype(BF16), vb, (((0,), (0,)), ((), ())),
                                                   preferred_element_type=F32)
    n_ref[...] = sc * n_ref[...] + jnp.sum(kw, axis=0, keepdims=True)
    m_ref[...] = m_new


def _axial_rope(x):
    T, d = x.shape[0], x.shape[-1]
    half = d // 2
    nf = half // 2
    t = jnp.arange(T)
    row = (t // GRID_W).astype(F32)
    col = (t % GRID_W).astype(F32)
    inv = ROPE_BASE ** (-jnp.arange(nf, dtype=F32) / nf)
    ang = jnp.concatenate([row[:, None] * inv, col[:, None] * inv], axis=-1)[:, None, :]
    cos, sin = jnp.cos(ang), jnp.sin(ang)
    x1, x2 = x[..., :half], x[..., half:]
    return jnp.concatenate([x1 * cos - x2 * sin, x2 * cos + x1 * sin], axis=-1)


def mlstm_mix(p, gate_b, norm_w):
    R = p.shape[0]
    L = ML_CHUNK
    nc = R // L
    QK = ML_HEADS * ML_DQK

    def rope_lat(t):
        th = t.reshape(R, ML_HEADS, ML_DQK)
        return jnp.concatenate([th[:CTX_LEN], _axial_rope(th[CTX_LEN:])], axis=0).reshape(R, QK)

    q = rope_lat(p[:, :QK])
    k = rope_lat(p[:, QK:2 * QK]) * ML_DQK ** -0.5
    o = p[:, 2 * QK + MIX_HALF:CD_GT]
    gt = p[:, CD_GT:CD_GT + 4 * ML_HEADS].reshape(R, 2, 2, ML_HEADS) + gate_b
    log_i = jnp.transpose(gt[:, :, 0], (1, 2, 0))
    log_f = jax.nn.log_sigmoid(jnp.transpose(gt[:, :, 1], (1, 2, 0)))
    as_rows = lambda t: t.reshape(2 * ML_HEADS * nc, 1, L)
    as_cols = lambda t: t.reshape(2 * ML_HEADS * nc, L, 1)

    def chunk(d, c):
        return jnp.where(d == 0, c, jnp.where(c == 0, 0, nc - c))

    gidx = lambda d, h, c: ((d * ML_HEADS + h) * nc + chunk(d, c), 0, 0)
    vb = 2 * QK // ML_DV
    h_dir = pl.pallas_call(
        _mlstm_kernel,
        grid=(2, ML_HEADS, nc),
        in_specs=[pl.BlockSpec((L, ML_DQK), lambda d, h, c: (chunk(d, c), h)),
                  pl.BlockSpec((L, ML_DQK), lambda d, h, c: (chunk(d, c), h)),
                  pl.BlockSpec((L, ML_DV), lambda d, h, c: (chunk(d, c), vb + h)),
                  pl.BlockSpec((1, 1, L), gidx), pl.BlockSpec((1, 1, L), gidx),
                  pl.BlockSpec((1, L, 1), gidx), pl.BlockSpec((1, L, 1), gidx)],
        out_specs=pl.BlockSpec((1, L, ML_DV), lambda d, h, c: (d, chunk(d, c), h)),
        out_shape=jax.ShapeDtypeStruct((2, R, MIX_HALF), F32),
        scratch_shapes=[pltpu.VMEM((ML_DQK, ML_DV), F32), pltpu.VMEM((1, ML_DQK), F32),
                        pltpu.VMEM((1, 1), F32)],
        compiler_params=_params(("parallel", "parallel", "arbitrary")),
        name="mlstm",
    )(q, k, p, as_rows(log_i), as_rows(log_f), as_cols(log_i), as_cols(log_f))
    h = (h_dir[0] + h_dir[1]).reshape(R, ML_HEADS, ML_DV)
    h = h * lax.rsqrt(jnp.mean(h * h, axis=-1, keepdims=True) + RMS_EPS)
    return h.reshape(R, MIX_HALF) * norm_w * jax.nn.sigmoid(o)


def _pad_to(x, axis, mult):
    n = -x.shape[axis] % mult
    if n == 0:
        return x
    pad = [(0, 0)] * x.ndim
    pad[axis] = (0, n)
    return jnp.pad(x, pad)


def mm3p(a, b):
    M, N = a.shape[0], b.shape[1]
    a = _pad_to(_pad_to(a, 1, LANE), 0, 8)
    b = _pad_to(_pad_to(b, 0, LANE), 1, LANE)
    return mm3(a, b)[:M, :N]


def hyena_filters(T, ff1, ff1_b, ff2, ff2_b, ff3, ff3_b, ff_out, freq, decay):
    t = jnp.linspace(0.0, 1.0, T, dtype=F32)[:, None]
    bands = (HY_EMB - 1) // 2
    f = jnp.linspace(1e-4, bands - 1, bands, dtype=F32)[None, :]
    w = (2 * math.pi / T) * jnp.arange(T, dtype=F32)[:, None]
    z = jnp.concatenate([t, jnp.cos(f * w), -jnp.sin(f * w)], axis=-1)
    h = jnp.sin(freq * (mm3p(z, ff1) + ff1_b))
    h = jnp.sin(freq * (mm3p(h, ff2) + ff2_b))
    h = jnp.sin(freq * (mm3p(h, ff3) + ff3_b))
    h = mm3p(h, ff_out) * jnp.exp(-t * jnp.abs(decay))
    return h.reshape(T, HY_ORDER, 2, MIX_HALF)


FFT_N2 = LANE


@functools.lru_cache(maxsize=None)
def _dft_tables(n1):
    n2 = FFT_N2
    n = n1 * n2
    i1, i2 = np.arange(n1), np.arange(n2)
    a1 = 2 * np.pi * np.outer(i1, i1) / n1
    a2 = 2 * np.pi * np.outer(i2, i2) / n2
    c1, s1 = np.cos(a1), -np.sin(a1)
    c2, s2 = np.cos(a2), -np.sin(a2)
    atw = 2 * np.pi * np.outer(i2, i1) / n
    f32 = lambda x: np.asarray(x, np.float32)
    return dict(
        wa=f32(np.concatenate([c1, s1], axis=1)),
        wb=f32(np.block([[c2, s2], [-s2, c2]])),
        wb_inv=f32(np.block([[c2, -s2], [s2, c2]])),
        wa_inv=f32(np.concatenate([c1, s1], axis=0)[:, :n1 // 2] / n),
        twr=f32(np.cos(atw)), twi=f32(-np.sin(atw)))


def _fft_fwd(x_t, n1):
    C = x_t.shape[0]
    n2 = FFT_N2
    tb = _dft_tables(n1)
    xa = jnp.transpose(x_t.reshape(C, n1, n2), (0, 2, 1)).reshape(C * n2, n1)
    a = mm3p(xa, tb["wa"]).reshape(C, n2, 2, n1)
    ar, ai = a[:, :, 0], a[:, :, 1]
    br = ar * tb["twr"] - ai * tb["twi"]
    bi = ar * tb["twi"] + ai * tb["twr"]
    b = jnp.concatenate([jnp.transpose(br, (0, 2, 1)), jnp.transpose(bi, (0, 2, 1))], axis=-1)
    return mm3(b.reshape(C * n1, 2 * n2), tb["wb"])


def _fft_inv(y, C, n1):
    n2 = FFT_N2
    tb = _dft_tables(n1)
    g = mm3(y, tb["wb_inv"]).reshape(C, n1, 2, n2)
    gr, gi = g[:, :, 0], g[:, :, 1]
    twr, twi = tb["twr"].T, tb["twi"].T
    hr = gr * twr + gi * twi
    hi = gi * twr - gr * twi
    h = jnp.concatenate([jnp.transpose(hr, (0, 2, 1)), jnp.transpose(hi, (0, 2, 1))], axis=-1)
    out = mm3p(h.reshape(C * n2, 2 * n1), tb["wa_inv"])
    return jnp.transpose(out.reshape(C, n2, n1 // 2), (0, 2, 1)).reshape(C, n1 // 2 * n2)


def _long_conv(z, h_fwd, h_bwd):
    T, C = z.shape
    n1 = 2 * T // FFT_N2
    f = jnp.concatenate([h_fwd, jnp.zeros((1, C), F32), h_bwd[:0:-1]], axis=0)
    zs = _fft_fwd(jnp.pad(z.T, ((0, 0), (0, T))), n1).reshape(C * n1, 2, FFT_N2)
    fs = _fft_fwd(f.T, n1).reshape(C * n1, 2, FFT_N2)
    yr = zs[:, 0] * fs[:, 0] - zs[:, 1] * fs[:, 1]
    yi = zs[:, 0] * fs[:, 1] + zs[:, 1] * fs[:, 0]
    return _fft_inv(jnp.concatenate([yr, yi], axis=-1), C, n1).T


HY_Q = 4
HY_F1Q = FFT_N2 // HY_Q
HY_QR = HY_F1Q * FFT_N2
HY_NN = FFT_N2 * FFT_N2


@functools.lru_cache(maxsize=None)
def _hy_tables():
    n = FFT_N2
    i = np.arange(n)
    w = np.exp(-2j * np.pi * np.outer(i, i) / n)
    tw = np.exp(-2j * np.pi * np.outer(i, i) / (n * n))
    ca = np.conj(w)[: n // 2] / (n * n)
    f32 = lambda x: np.ascontiguousarray(x, dtype=np.float32)
    return dict(
        wa=f32(np.concatenate([w.real, w.imag], axis=0)),
        wbr=f32(w.real), wbi=f32(w.imag),
        twr=f32(tw.real).reshape(n, 1, n), twi=f32(tw.imag).reshape(n, 1, n),
        ctr=f32(tw.real.T).reshape(n, 1, n), cti=f32(-tw.imag.T).reshape(n, 1, n),
        car=f32(ca.real), cai=f32(ca.imag))


def _dot3(a_hi, a_lo, b):
    b_hi, b_lo = _split(b)
    acc = jnp.dot(a_hi, b_hi, preferred_element_type=F32)
    acc += jnp.dot(a_hi, b_lo, preferred_element_type=F32)
    return acc + jnp.dot(a_lo, b_hi, preferred_element_type=F32)


def _hy_fwd_kernel(x_ref, wa_ref, wbr_ref, wbi_ref, twr_ref, twi_ref, o_ref, a_ref, *, k1):
    n = FFT_N2
    q = pl.program_id(1)

    @pl.when(q == 0)
    def _():
        wa_hi, wa_lo = _split(wa_ref[...])
        for t2 in range(n):
            y = _dot3(wa_hi, wa_lo, x_ref[pl.ds(t2, k1, stride=n), :])
            a_ref[pl.ds(t2, n, stride=n), :] = y[:n]
            a_ref[pl.ds(HY_NN + t2, n, stride=n), :] = y[n:]

    cr, ci = wbr_ref[...], wbi_ref[...]

    def body(j, carry):
        f1 = q * HY_F1Q + j
        base = pl.multiple_of(f1 * n, n)
        rhs = jnp.concatenate([a_ref[pl.ds(base, n), :], a_ref[pl.ds(HY_NN + base, n), :]], axis=0)
        twr, twi = twr_ref[f1], twi_ref[f1]
        pr = cr * twr - ci * twi
        pi = cr * twi + ci * twr
        lhs = jnp.concatenate([jnp.concatenate([pr, -pi], axis=1), jnp.concatenate([pi, pr], axis=1)], axis=0)
        out = _dot3(*_split(lhs), rhs)
        ob = pl.multiple_of(j * n, n)
        o_ref[pl.ds(ob, n), :] = out[:n]
        o_ref[pl.ds(HY_QR + ob, n), :] = out[n:]
        return carry

    lax.fori_loop(0, HY_F1Q, body, 0, unroll=2)


def hy_fwd(x):
    T_in, C = x.shape
    k1 = T_in // FFT_N2
    tb = _hy_tables()
    const = lambda a: pl.BlockSpec(a.shape, lambda c, q: (0,) * a.ndim)
    wa = tb["wa"][:, :k1]
    args = (wa, tb["wbr"], tb["wbi"], tb["twr"], tb["twi"])
    return pl.pallas_call(
        functools.partial(_hy_fwd_kernel, k1=k1),
        grid=(C // LANE, HY_Q),
        in_specs=[pl.BlockSpec((T_in, LANE), lambda c, q: (0, c))] + [const(a) for a in args],
        out_specs=pl.BlockSpec((None, 2 * HY_QR, LANE), lambda c, q: (q, 0, c)),
        out_shape=jax.ShapeDtypeStruct((HY_Q, 2 * HY_QR, C), F32),
        scratch_shapes=[pltpu.VMEM((2 * HY_NN, LANE), F32)],
        compiler_params=_params(("parallel", "arbitrary")),
        name="hy_fwd",
    )(x, *args)


def _hy_conv_kernel(z_ref, f_ref, wbr_ref, wbi_ref, ctr_ref, cti_ref, car_ref, cai_ref, y_ref, g_ref):
    n = FFT_N2
    q = pl.program_id(1)
    cr, ci = wbr_ref[...], wbi_ref[...]
    l_hi, l_lo = _split(jnp.concatenate([jnp.concatenate([cr, ci], axis=1),
                                         jnp.concatenate([-ci, cr], axis=1)], axis=0))

    def body(j, carry):
        ob = pl.multiple_of(j * n, n)
        zr, zi = z_ref[pl.ds(ob, n), :], z_ref[pl.ds(HY_QR + ob, n), :]
        fr, fi = f_ref[pl.ds(ob, n), :], f_ref[pl.ds(HY_QR + ob, n), :]
        prod = jnp.concatenate([zr * fr - zi * fi, zr * fi + zi * fr], axis=0)
        out = _dot3(l_hi, l_lo, prod)
        base = pl.multiple_of((q * HY_F1Q + j) * n, n)
        g_ref[pl.ds(base, n), :] = out[:n]
        g_ref[pl.ds(HY_NN + base, n), :] = out[n:]
        return carry

    lax.fori_loop(0, HY_F1Q, body, 0, unroll=2)

    @pl.when(q == HY_Q - 1)
    def _():
        car, cai = car_ref[...], cai_ref[...]
        for t2 in range(n):
            ctr, cti = ctr_ref[t2], cti_ref[t2]
            er = car * ctr - cai * cti
            ei = car * cti + cai * ctr
            rhs = jnp.concatenate([g_ref[pl.ds(t2, n, stride=n), :],
                                   g_ref[pl.ds(HY_NN + t2, n, stride=n), :]], axis=0)
            y_ref[pl.ds(t2, n // 2, stride=n), :] = _dot3(*_split(jnp.concatenate([er, -ei], axis=1)), rhs)


def hy_conv(zs, fs, f_block=0):
    C = zs.shape[-1]
    T = HY_NN // 2
    tb = _hy_tables()
    const = lambda a: pl.BlockSpec(a.shape, lambda c, q: (0,) * a.ndim)
    spec = pl.BlockSpec((None, 2 * HY_QR, LANE), lambda c, q: (q, 0, c))
    fspec = pl.BlockSpec((None, 2 * HY_QR, LANE), lambda c, q: (q, 0, c + f_block))
    args = (tb["wbr"], tb["wbi"], tb["ctr"], tb["cti"], tb["car"], tb["cai"])
    return pl.pallas_call(
        _hy_conv_kernel,
        grid=(C // LANE, HY_Q),
        in_specs=[spec, fspec] + [const(a) for a in args],
        out_specs=pl.BlockSpec((T, LANE), lambda c, q: (0, c)),
        out_shape=jax.ShapeDtypeStruct((T, C), F32),
        scratch_shapes=[pltpu.VMEM((2 * HY_NN, LANE), F32)],
        compiler_params=_params(("parallel", "arbitrary")),
        name="hy_conv",
    )(zs, fs, *args)


def hyena_filter_taps(T, ff1, ff1_b, ff2, ff2_b, ff3, ff3_b, ff_out, freq, decay):
    t = jnp.linspace(0.0, 1.0, T, dtype=F32)[:, None]
    bands = (HY_EMB - 1) // 2
    f = jnp.linspace(1e-4, bands - 1, bands, dtype=F32)[None, :]
    w = (2 * math.pi / T) * jnp.arange(T, dtype=F32)[:, None]
    z = jnp.concatenate([t, jnp.cos(f * w), -jnp.sin(f * w)], axis=-1)
    h = jnp.sin(freq * (mm3p(z, ff1) + ff1_b))
    h = jnp.sin(freq * (mm3p(h, ff2) + ff2_b))
    h = jnp.sin(freq * (mm3p(h, ff3) + ff3_b))
    w_out = ff_out.reshape(-1, HY_ORDER, 2, MIX_HALF)
    dec = jnp.abs(decay).reshape(HY_ORDER, 2, MIX_HALF)
    side = lambda s: (w_out[:, :, s].reshape(-1, HY_ORDER * MIX_HALF), dec[:, s].reshape(HY_ORDER * MIX_HALF))
    (w_f, d_f), (w_b, d_b) = side(0), side(1)
    fwd = mm3p(h, w_f) * jnp.exp(-t * d_f)
    bwd = mm3p(h[::-1], w_b) * jnp.exp(-t[::-1] * d_b)
    return jnp.concatenate([fwd, jnp.zeros((1, fwd.shape[1]), F32), bwd[:T - 1]], axis=0)


@functools.lru_cache(maxsize=None)
def _dft_small(n):
    i = np.arange(n)
    w = np.exp(-2j * np.pi * np.outer(i, i) / n)
    fwd = np.concatenate([w.real, w.imag], axis=0)
    inv = np.concatenate([w.real, w.imag], axis=1)[: n // 2] / n
    return np.asarray(fwd, np.float32), np.asarray(inv, np.float32)


def _short_conv(z, h_fwd, h_bwd):
    T, C = z.shape
    n = 2 * T
    fwd, inv = _dft_small(n)
    f = jnp.concatenate([h_fwd, jnp.zeros((1, C), F32), h_bwd[:0:-1]], axis=0)
    zs = mm3(fwd[:, :T], z)
    fs = mm3(fwd, f)
    prod = jnp.concatenate([zs[:n] * fs[:n] - zs[n:] * fs[n:], zs[:n] * fs[n:] + zs[n:] * fs[:n]], axis=0)
    return mm3(inv, prod)


def hyena_seq(p, short_w, short_b, conv, bias):
    u = short_w[0] * _shift_prev(p) + short_w[1] * p + short_w[2] * _shift_next(p) + short_b
    z, x1, x2 = jnp.split(u, 3, axis=-1)
    for n, gate in enumerate((x1, x2)):
        z = gate * (conv(z, n) + bias[n] * z)
    return z


def hyena_mix(p, short_w, short_b, filt_params, bias):
    T = p.shape[0] - CTX_LEN
    filt_c = hyena_filters(CTX_LEN, *filt_params)
    out_c = hyena_seq(p[:CTX_LEN], short_w, short_b,
                      lambda z, n: _short_conv(z, filt_c[:, n, 0], filt_c[:, n, 1]), bias)
    if 2 * T == HY_NN:
        spectra = hy_fwd(hyena_filter_taps(T, *filt_params))
        conv = lambda z, n: hy_conv(hy_fwd(z), spectra, n * MIX_HALF // LANE)
    else:
        filt = hyena_filters(T, *filt_params)
        conv = lambda z, n: _long_conv(z, filt[:, n, 0], filt[:, n, 1])
    out = hyena_seq(p[CTX_LEN:], short_w, short_b, conv, bias)
    return jnp.concatenate([out_c, out], axis=0)


def kernel(x, c, ctx, c_ctx, ada_down, ada_up, ada_bias, norm_g, ffn_w1, ffn_w3, ffn_w2, ab_w_in, ab_w_out, rwkv_mu, rwkv_w0, rwkv_w2, rwkv_a0, rwkv_a2, rwkv_g2, rwkv_kk, rwkv_ka, rwkv_rk, rwkv_gn, na_rpb, cd_w_in, cd_w_out, ml_gate_b, ml_norm, hy_short_w, hy_short_b, hy_ff1, hy_ff1_b, hy_ff2, hy_ff2_b, hy_ff3, hy_ff3_b, hy_ff_out, hy_freq, hy_decay, hy_bias):
    assert x.shape[0] == 1 and ctx.shape[1] == CTX_LEN
    X = jnp.concatenate([ctx[0], x[0]], axis=0)
    s2 = _pad_to(jnp.stack([jax.nn.silu(c_ctx), jax.nn.silu(c[0])]), 0, 16)
    n = jnp.arange(MIX_HALF)
    perm = ((n % RWKV_HEADS) * RWKV_HEAD + n // RWKV_HEADS == n[:, None]).astype(BF16)
    mods = [(mm(mm(s2, ada_down[l]), ada_up[l])[:2] + ada_bias[l]).reshape(2, N_SUB, 3, D_MODEL)
            for l in range(DEPTH)]
    hn = norm_mod(X, norm_g[0, 0], mods[0][:, 0, 0], mods[0][:, 0, 1])
    for l in range(DEPTH):
        i = l // 2
        mod = mods[l]
        g = norm_g[l]
        y = mm_w32(swiglu(hn, ffn_w1, ffn_w3, (l, 0)), ffn_w2, (l, 0))
        X, xn = post_res_norm(X, y, g[1], mod[:, 0, 2], FFN_RES, g[2], mod[:, 1, 0], mod[:, 1, 1])
        if l % 2 == 0:
            w_ab = ab_w_in[i].astype(BF16)
            w_rkv = [mm(t, perm, BF16) for t in jnp.split(w_ab[:, :AB_WD], 3, axis=-1)]
            p = mm(xn, _pack_cols(jnp.concatenate(w_rkv + [w_ab[:, AB_WD:]], axis=-1), _AB_CUTS, _AB_WIDTHS))
            ya = rwkv_mix(p, rwkv_mu[i], rwkv_w0[i], rwkv_w2[i], rwkv_a0[i], rwkv_a2[i],
                          rwkv_g2[i], rwkv_kk[i], rwkv_ka[i], rwkv_rk[i], rwkv_gn[i])
            y = jnp.concatenate([ya, na_mix(p, na_rpb[i])], axis=-1)
            w_out = ab_w_out[i].astype(BF16)
            w_rw = mm(perm.T, w_out[:MIX_HALF], BF16)
            yo = mm(y, jnp.concatenate([w_rw, w_out[MIX_HALF:]], axis=0))
        else:
            w_in = _pack_cols(cd_w_in[i].astype(BF16), (CD_GT, ML_COLS), (CD_GT, LANE, P_COLS - CD_HY))
            p = mm(xn, w_in)
            filt_params = (hy_ff1[i], hy_ff1_b[i], hy_ff2[i], hy_ff2_b[i], hy_ff3[i], hy_ff3_b[i],
                           hy_ff_out[i], hy_freq[i], hy_decay[i])
            yd = hyena_mix(p[:, CD_HY:CD_HY + 3 * MIX_HALF], hy_short_w[i], hy_short_b[i], filt_params,
                           hy_bias[i])
            y = jnp.concatenate([mlstm_mix(p, ml_gate_b[i], ml_norm[i]), yd], axis=-1)
            yo = mm_w32(y.astype(BF16), cd_w_out, (i,))
        X, hn = post_res_norm(X, yo, g[3], mod[:, 1, 2], 1.0, g[4], mod[:, 2, 0], mod[:, 2, 1])
        y = mm_w32(swiglu(hn, ffn_w1, ffn_w3, (l, 1)), ffn_w2, (l, 1))
        if l + 1 < DEPTH:
            nxt = mods[l + 1]
            X, hn = post_res_norm(X, y, g[5], mod[:, 2, 2], FFN_RES, norm_g[l + 1, 0], nxt[:, 0, 0], nxt[:, 0, 1])
        else:
            X = post_res(X, y, g[5], mod[:, 2, 2], FFN_RES)
    return X[CTX_LEN:][None]
```

```python
import functools
import math

import jax
import jax.numpy as jnp
import numpy as np
from jax import lax
from jax.experimental import pallas as pl
from jax.experimental.pallas import tpu as pltpu

F32 = jnp.float32
BF16 = jnp.bfloat16

D_MODEL = 4096
DEPTH = 4
GRID_W = 64
CTX_LEN = 256
N_SUB = 3
FFN_RES = 0.5
D_FF = 5632
RMS_EPS = 1e-6
MIX_HALF = D_MODEL // 2

RWKV_HEAD = 64
RWKV_HEADS = MIX_HALF // RWKV_HEAD
RWKV_DECAY_LORA = 96
RWKV_A_LORA = 96
RWKV_GATE_LORA = 256
RWKV_COLS = 3 * MIX_HALF + RWKV_DECAY_LORA + RWKV_A_LORA + RWKV_GATE_LORA
RWKV_GN_EPS = 64e-5

NA_HEAD_DIM = 128
NA_HEADS = MIX_HALF // NA_HEAD_DIM
NA_KH = 8
NA_KW = 16

ML_HEADS = 4
ML_DV = MIX_HALF // ML_HEADS
ML_DQK = ML_DV // 2
ML_COLS = 2 * ML_HEADS * ML_DQK + 2 * MIX_HALF + 4 * ML_HEADS
ROPE_BASE = 10000.0

HY_ORDER = 2
HY_EMB = 33

LANE = 128
ROW_TILE = CTX_LEN
VMEM_LIMIT = 56 * 1024 * 1024
P_COLS = 12800
NEG = -1e30


def _params(sem):
    return pltpu.CompilerParams(dimension_semantics=sem, vmem_limit_bytes=VMEM_LIMIT)


def _pick(n, cands):
    for c in cands:
        if n % c == 0:
            return c
    return n


def _mm_kernel(a_ref, b_ref, o_ref):
    o_ref[...] = jnp.dot(a_ref[...], b_ref[...], preferred_element_type=F32).astype(o_ref.dtype)


def _split(x):
    hi = x.astype(BF16)
    lo = (x - hi.astype(F32)).astype(BF16)
    return hi, lo


def _mm3_kernel(a_ref, b_ref, o_ref):
    a_hi, a_lo = _split(a_ref[...])
    b_hi, b_lo = _split(b_ref[...])
    acc = jnp.dot(a_hi, b_hi, preferred_element_type=F32)
    acc += jnp.dot(a_hi, b_lo, preferred_element_type=F32)
    acc += jnp.dot(a_lo, b_hi, preferred_element_type=F32)
    o_ref[...] = acc


def _mm_call(kern, a, b, out_dtype, name):
    M, K = a.shape
    N = b.shape[1]
    tm = _pick(M, (1024, 768, 512, 256, 128, 64, 32, 16, 8))
    tn = _pick(N, (512, 256, 128))
    return pl.pallas_call(
        kern,
        grid=(M // tm, N // tn),
        in_specs=[pl.BlockSpec((tm, K), lambda i, j: (i, 0)),
                  pl.BlockSpec((K, tn), lambda i, j: (0, j))],
        out_specs=pl.BlockSpec((tm, tn), lambda i, j: (i, j)),
        out_shape=jax.ShapeDtypeStruct((M, N), out_dtype),
        compiler_params=_params(("parallel", "parallel")),
        name=name,
    )(a, b)


def mm(a, b, out_dtype=F32):
    return _mm_call(_mm_kernel, a.astype(BF16), b.astype(BF16), out_dtype, "mm")


def mm3(a, b):
    return _mm_call(_mm3_kernel, a.astype(F32), b.astype(F32), F32, "mm3")


def _mm_w32_kernel(a_ref, w_ref, o_ref, wb_ref):
    @pl.when(pl.program_id(1) == 0)
    def _():
        wb_ref[...] = w_ref[...].astype(BF16)

    o_ref[...] = jnp.dot(a_ref[...], wb_ref[...], preferred_element_type=F32).astype(o_ref.dtype)


def _swiglu_kernel(a_ref, w1_ref, w3_ref, o_ref, w1b_ref, w3b_ref):
    @pl.when(pl.program_id(1) == 0)
    def _():
        w1b_ref[...] = w1_ref[...].astype(BF16)
        w3b_ref[...] = w3_ref[...].astype(BF16)

    a = a_ref[...]
    h1 = jnp.dot(a, w1b_ref[...], preferred_element_type=F32)
    h3 = jnp.dot(a, w3b_ref[...], preferred_element_type=F32)
    o_ref[...] = (h1 * jax.nn.sigmoid(h1) * h3).astype(o_ref.dtype)


def _w32_call(kern, a, ws, tn, out_dtype, name, lead=()):
    M, K = a.shape
    N = ws[0].shape[-1]
    tm = _pick(M, (768, 512, 256))
    wspec = pl.BlockSpec((None,) * len(lead) + (K, tn), lambda j, i: tuple(lead) + (0, j))
    return pl.pallas_call(
        kern,
        grid=(N // tn, M // tm),
        in_specs=[pl.BlockSpec((tm, K), lambda j, i: (i, 0))] + [wspec] * len(ws),
        out_specs=pl.BlockSpec((tm, tn), lambda j, i: (i, j)),
        out_shape=jax.ShapeDtypeStruct((M, N), out_dtype),
        scratch_shapes=[pltpu.VMEM((K, tn), BF16)] * len(ws),
        compiler_params=_params(("parallel", "arbitrary")),
        name=name,
    )(a, *ws)


def swiglu(a, w1, w3, lead=()):
    return _w32_call(_swiglu_kernel, a, (w1, w3), 256, BF16, "swiglu", lead)


def mm_w32(a, w, lead=()):
    return _w32_call(_mm_w32_kernel, a, (w,), 512, F32, "mm_w32", lead)


def _group_spec(d):
    return pl.BlockSpec((1, 1, d), lambda i: (jnp.minimum(i, 1), 0, 0))


def _norm_mod_kernel(x_ref, g_ref, sh_ref, sc_ref, o_ref):
    x = x_ref[...]
    y = x * lax.rsqrt(jnp.mean(x * x, axis=-1, keepdims=True) + RMS_EPS) * g_ref[...]
    o_ref[...] = (y * (1 + sc_ref[0]) + sh_ref[0]).astype(o_ref.dtype)


def norm_mod(x, g, shift, scale):
    R, D = x.shape
    row = pl.BlockSpec((ROW_TILE, D), lambda i: (i, 0))
    return pl.pallas_call(
        _norm_mod_kernel,
        grid=(R // ROW_TILE,),
        in_specs=[row, pl.BlockSpec((1, D), lambda i: (0, 0)), _group_spec(D), _group_spec(D)],
        out_specs=row,
        out_shape=jax.ShapeDtypeStruct((R, D), BF16),
        compiler_params=_params(("parallel",)),
        name="norm_mod",
    )(x, g.reshape(1, D), shift.reshape(2, 1, D), scale.reshape(2, 1, D))


def _post_res_kernel(x_ref, y_ref, g_ref, gate_ref, o_ref, *, coef):
    y = y_ref[...]
    yn = y * lax.rsqrt(jnp.mean(y * y, axis=-1, keepdims=True) + RMS_EPS) * g_ref[...]
    o_ref[...] = x_ref[...] + coef * gate_ref[0] * yn


def _post_res_norm_kernel(x_ref, y_ref, g_ref, gate_ref, gn_ref, sh_ref, sc_ref, o_ref, h_ref, *, coef):
    y = y_ref[...]
    yn = y * lax.rsqrt(jnp.mean(y * y, axis=-1, keepdims=True) + RMS_EPS) * g_ref[...]
    x = x_ref[...] + coef * gate_ref[0] * yn
    o_ref[...] = x
    h = x * lax.rsqrt(jnp.mean(x * x, axis=-1, keepdims=True) + RMS_EPS) * gn_ref[...]
    h_ref[...] = (h * (1 + sc_ref[0]) + sh_ref[0]).astype(h_ref.dtype)


def post_res_norm(x, y, g, gate, coef, g_next, shift_next, scale_next):
    R, D = x.shape
    row = pl.BlockSpec((ROW_TILE, D), lambda i: (i, 0))
    vec = pl.BlockSpec((1, D), lambda i: (0, 0))
    grp = lambda t: t.reshape(2, 1, D)
    return pl.pallas_call(
        functools.partial(_post_res_norm_kernel, coef=coef),
        grid=(R // ROW_TILE,),
        in_specs=[row, row, vec, _group_spec(D), vec, _group_spec(D), _group_spec(D)],
        out_specs=[row, row],
        out_shape=[jax.ShapeDtypeStruct((R, D), F32), jax.ShapeDtypeStruct((R, D), BF16)],
        compiler_params=_params(("parallel",)),
        name="post_res_norm",
    )(x, y, g.reshape(1, D), grp(gate), g_next.reshape(1, D), grp(shift_next), grp(scale_next))


def post_res(x, y, g, gate, coef):
    R, D = x.shape
    row = pl.BlockSpec((ROW_TILE, D), lambda i: (i, 0))
    return pl.pallas_call(
        functools.partial(_post_res_kernel, coef=coef),
        grid=(R // ROW_TILE,),
        in_specs=[row, row, pl.BlockSpec((1, D), lambda i: (0, 0)), _group_spec(D)],
        out_specs=row,
        out_shape=jax.ShapeDtypeStruct((R, D), F32),
        compiler_params=_params(("parallel",)),
        name="post_res",
    )(x, y, g.reshape(1, D), gate.reshape(2, 1, D))


WKV_TB = 8
WKV_KG = MIX_HALF // LANE
WKV_VC = RWKV_HEAD // 8
WKV_CG = 2
PREP_TB = 128


def _kmajor(w):
    lead = w.shape[:-1]
    return jnp.swapaxes(w.reshape(*lead, RWKV_HEADS, RWKV_HEAD), -1, -2).reshape(*lead, MIX_HALF)


def _fold_lanes(s):
    s = s + pltpu.roll(s, 64, 1)
    return s + pltpu.roll(s, 32, 1)


def _rwkv_prep_kernel(p_ref, hp_ref, hn_ref, mu_ref, w2_ref, a2_ref, g2_ref, vec_ref,
                      a_out, b0_out, b1_out, k0_out, k1_out, w0_out, w1_out, r_out, v_out, g_out, bonus_out):
    W = MIX_HALF
    i = pl.program_id(0)
    cb = CTX_LEN // PREP_TB
    has_prev = jnp.logical_and(i != 0, i != cb).astype(F32)
    has_next = jnp.logical_and(i != cb - 1, i != pl.num_programs(0) - 1).astype(F32)
    row = lax.broadcasted_iota(jnp.int32, (PREP_TB, 1), 0)

    def shifted(c0, width):
        x = p_ref[:, c0:c0 + width]
        up = jnp.where(row == 0, hp_ref[7:8, c0:c0 + width] * has_prev, pltpu.roll(x, 1, 0))
        dn = jnp.where(row == PREP_TB - 1, hn_ref[0:1, c0:c0 + width] * has_next,
                       pltpu.roll(x, PREP_TB - 1, 0))
        return x + mu_ref[0:1, c0:c0 + width] * (up - x) + mu_ref[1:2, c0:c0 + width] * (dn - x)

    lo = shifted(AB_WD, AB_NA - AB_WD)
    wd, ad, gd = lo[:, :LANE], lo[:, LANE:2 * LANE], lo[:, 2 * LANE:]
    wlo = jnp.dot(jnp.tanh(wd).astype(BF16), w2_ref[...], preferred_element_type=F32)
    alo = jnp.dot(ad.astype(BF16), a2_ref[...], preferred_element_type=F32)
    g_out[...] = jnp.dot(jax.nn.sigmoid(gd).astype(BF16), g2_ref[...], preferred_element_type=F32)

    ss = None
    for g in range(WKV_KG):
        sl = slice(g * LANE, (g + 1) * LANE)
        kk = shifted(W + g * LANE, LANE) * vec_ref[4:5, sl]
        ss = kk * kk if ss is None else ss + kk * kk
    nrm = jnp.maximum(jnp.sqrt(_fold_lanes(ss)), 1e-12)
    bonus = None
    for g in range(WKV_KG):
        sl = slice(g * LANE, (g + 1) * LANE)
        r = shifted(g * LANE, LANE)
        k = shifted(W + g * LANE, LANE)
        r_out[:, sl] = r
        v_out[:, sl] = shifted(2 * W + g * LANE, LANE)
        kk = k * vec_ref[4:5, sl] / nrm
        a_out[:, sl] = -kk
        for d, (w_out, k_out, b_out) in enumerate(((w0_out, k0_out, b0_out), (w1_out, k1_out, b1_out))):
            dl = slice(d * W + g * LANE, d * W + (g + 1) * LANE)
            logw = -jax.nn.softplus(-(vec_ref[d:d + 1, sl] + wlo[:, dl])) - 0.5
            w_out[:, sl] = jnp.exp(-jnp.exp(logw))
            a = jax.nn.sigmoid(vec_ref[2 + d:3 + d, sl] + alo[:, dl])
            kd = k * (1 + (a - 1) * vec_ref[5:6, sl])
            k_out[:, sl] = kd
            b_out[:, sl] = kk * a
            term = r * kd * vec_ref[6:7, sl]
            bonus = term if bonus is None else bonus + term
    bonus_out[...] = _fold_lanes(bonus)


def rwkv_prep(p, mu, w2c, a2c, g2, vec):
    R = p.shape[0]
    W = MIX_HALF
    hb = PREP_TB // 8
    full = lambda a: pl.BlockSpec(a.shape, lambda i: (0,) * a.ndim)
    wide = pl.BlockSpec((PREP_TB, W), lambda i: (i, 0))
    return pl.pallas_call(
        _rwkv_prep_kernel,
        grid=(R // PREP_TB,),
        in_specs=[pl.BlockSpec((PREP_TB, AB_NA), lambda i: (i, 0)),
                  pl.BlockSpec((8, AB_NA), lambda i: (jnp.maximum(i * hb - 1, 0), 0)),
                  pl.BlockSpec((8, AB_NA), lambda i: (jnp.minimum((i + 1) * hb, R // 8 - 1), 0)),
                  full(mu), full(w2c), full(a2c), full(g2), full(vec)],
        out_specs=[wide] * 10 + [pl.BlockSpec((PREP_TB, LANE), lambda i: (i, 0))],
        out_shape=[jax.ShapeDtypeStruct((R, W), F32)] * 10 + [jax.ShapeDtypeStruct((R, LANE), F32)],
        compiler_params=_params(("parallel",)),
        name="rwkv_prep",
    )(p, p, p, mu, w2c, a2c, g2, vec)


def _wkv_kernel(a0_ref, b0_ref, k0_ref, w0_ref, r0_ref, v0_ref, a1_ref, b1_ref, k1_ref, w1_ref, r1_ref, v1_ref,
                y0_ref, y1_ref, s_ref):
    @pl.when(pl.program_id(0) == 0)
    def _():
        s_ref[...] = jnp.zeros_like(s_ref)

    dirs = ((a0_ref, b0_ref, k0_ref, w0_ref, r0_ref, v0_ref, y0_ref),
            (a1_ref, b1_ref, k1_ref, w1_ref, r1_ref, v1_ref, y1_ref))

    for t in range(WKV_TB):
        for d, (a_ref, b_ref, k_ref, w_ref, r_ref, v_ref, y_ref) in enumerate(dirs):
            tt = t if d == 0 else WKV_TB - 1 - t
            row = lambda ref, g: ref[tt:tt + 1, g * LANE:(g + 1) * LANE]
            for c0 in range(0, WKV_VC, WKV_CG):
                cs = range(c0, c0 + WKV_CG)
                acc = {c: None for c in cs}
                for g in range(WKV_KG):
                    ar = row(a_ref, g)
                    for c in cs:
                        term = s_ref[d, g, pl.ds(c * 8, 8), :] * ar
                        acc[c] = term if acc[c] is None else acc[c] + term
                sa = {c: _fold_lanes(acc[c]) for c in cs}
                vv = {c: v_ref[tt, pl.ds(c * 8, 8), :] for c in cs}
                acc = {c: None for c in cs}
                for g in range(WKV_KG):
                    wr, br, kr, rr = row(w_ref, g), row(b_ref, g), row(k_ref, g), row(r_ref, g)
                    for c in cs:
                        s_new = s_ref[d, g, pl.ds(c * 8, 8), :] * wr + sa[c] * br + vv[c] * kr
                        s_ref[d, g, pl.ds(c * 8, 8), :] = s_new
                        term = s_new * rr
                        acc[c] = term if acc[c] is None else acc[c] + term
                for c in cs:
                    y_ref[tt, pl.ds(c * 8, 8), :] = _fold_lanes(acc[c])


def wkv_scan(a, b0, b1, k0, k1, w0, w1, r, vp):
    R = a.shape[0]
    nb = R // WKV_TB
    cb = CTX_LEN // WKV_TB
    fwd = lambda i: i
    bwd = lambda i: jnp.where(i < cb, cb - 1 - i, nb + cb - 1 - i)
    kspec = lambda f: pl.BlockSpec((WKV_TB, MIX_HALF), lambda i: (f(i), 0))
    vspec = lambda f: pl.BlockSpec((WKV_TB, RWKV_HEAD, LANE), lambda i: (f(i), 0, 0))
    yshape = jax.ShapeDtypeStruct((R, RWKV_HEAD, LANE), F32)
    return pl.pallas_call(
        _wkv_kernel,
        grid=(nb,),
        in_specs=[kspec(fwd)] * 5 + [vspec(fwd)] + [kspec(bwd)] * 5 + [vspec(bwd)],
        out_specs=[vspec(fwd), vspec(bwd)],
        out_shape=[yshape, yshape],
        scratch_shapes=[pltpu.VMEM((2, WKV_KG, RWKV_HEAD, LANE), F32)],
        compiler_params=_params(("arbitrary",)),
        name="wkv_scan",
    )(a, b0, k0, w0, r, vp, a, b1, k1, w1, r, vp)


def _shift_prev(y):
    return jnp.pad(y, ((1, 0), (0, 0)))[:-1]


def _shift_next(y):
    return jnp.pad(y, ((0, 1), (0, 0)))[1:]


AB_WD = 3 * MIX_HALF
AB_AD = AB_WD + LANE
AB_GD = AB_AD + LANE
AB_NA = AB_GD + RWKV_GATE_LORA


def _pack_cols(w, cuts, widths):
    pieces = jnp.split(w, cuts, axis=-1)
    pad = [(0, 0)] * (w.ndim - 1)
    return jnp.concatenate([jnp.pad(p, pad + [(0, wd - p.shape[-1])]) for p, wd in zip(pieces, widths)],
                           axis=-1)


_AB_CUTS = (AB_WD, AB_WD + RWKV_DECAY_LORA, AB_WD + RWKV_DECAY_LORA + RWKV_A_LORA, RWKV_COLS)
_AB_WIDTHS = (AB_WD, LANE, LANE, RWKV_GATE_LORA, 3 * MIX_HALF)


def rwkv_mix(p, mu, w0, w2, a0, a2, g2, k_k, k_a, r_k, gn):
    R = p.shape[0]
    km3 = lambda t: jnp.concatenate([_kmajor(x) for x in jnp.split(t, 3, axis=-1)], axis=-1)
    mu_p = _pack_cols(mu, _AB_CUTS[:3], _AB_WIDTHS[:4])
    mu_p = jnp.concatenate([km3(mu_p[:, :AB_WD]), mu_p[:, AB_WD:]], axis=-1)
    lora = lambda w: jnp.pad(jnp.concatenate([_kmajor(w[0]), _kmajor(w[1])], axis=-1),
                             ((0, LANE - w.shape[1]), (0, 0))).astype(BF16)
    vec = jnp.stack([_kmajor(t) for t in (w0[0], w0[1], a0[0], a0[1], k_k, k_a, r_k, jnp.zeros_like(r_k))])
    a, b0, b1, k0, k1, d0, d1, r, v, g, bonus = rwkv_prep(p, mu_p, lora(w2), lora(a2),
                                                          _kmajor(g2).astype(BF16), vec)
    tile = lambda t: t.reshape(R, RWKV_HEAD, RWKV_HEADS)
    vp = jnp.broadcast_to(tile(v)[:, :, None, :], (R, RWKV_HEAD, 4, RWKV_HEADS)).reshape(R, RWKV_HEAD, LANE)
    y0, y1 = wkv_scan(a, b0, b1, k0, k1, d0, d1, r, vp)
    y = (y0 + y1)[:, :, :RWKV_HEADS]
    mean = jnp.mean(y, axis=1, keepdims=True)
    var = jnp.mean(jnp.square(y - mean), axis=1, keepdims=True)
    gn_t = _kmajor(gn).reshape(2, RWKV_HEAD, RWKV_HEADS)
    yn = (y - mean) * lax.rsqrt(var + RWKV_GN_EPS) * gn_t[0] + gn_t[1]
    out = (yn + bonus[:, None, :RWKV_HEADS] * tile(v)) * tile(g)
    return out.reshape(R, MIX_HALF)


NA_WIN = NA_KH * GRID_W


NA_RB = 4


def _na_kernel(q_ref, k_ref, v_ref, bias_ref, o_ref, *, rows):
    dn = (((1,), (1,)), ((), ()))
    k_ctx = k_ref[pl.ds(0, CTX_LEN), :].astype(BF16)
    v_ctx = v_ref[pl.ds(0, CTX_LEN), :].astype(BF16)
    for j in range(NA_RB):
        r = pl.program_id(1) * NA_RB + j
        r0 = jnp.clip(r - NA_KH // 2, 0, rows - NA_KH)
        start = pl.multiple_of(CTX_LEN + r0 * GRID_W, GRID_W)
        qs = slice(j * GRID_W, (j + 1) * GRID_W)
        q = (q_ref[qs, :] * NA_HEAD_DIM ** -0.5).astype(BF16)
        k_win = k_ref[pl.ds(start, NA_WIN), :].astype(BF16)
        s_loc = lax.dot_general(q, k_win, dn, preferred_element_type=F32) + bias_ref[0, r - r0]
        s_ctx = lax.dot_general(q, k_ctx, dn, preferred_element_type=F32)
        m = jnp.maximum(jnp.max(s_loc, axis=-1, keepdims=True), jnp.max(s_ctx, axis=-1, keepdims=True))
        p_loc = jnp.exp(s_loc - m)
        p_ctx = jnp.exp(s_ctx - m)
        den = jnp.sum(p_loc, axis=-1, keepdims=True) + jnp.sum(p_ctx, axis=-1, keepdims=True)
        num = jnp.dot(p_loc.astype(BF16), v_ref[pl.ds(start, NA_WIN), :].astype(BF16),
                      preferred_element_type=F32)
        num += jnp.dot(p_ctx.astype(BF16), v_ctx, preferred_element_type=F32)
        o_ref[qs, :] = num / den


def _na_ctx_kernel(q_ref, k_ref, v_ref, o_ref):
    q = (q_ref[...] * NA_HEAD_DIM ** -0.5).astype(BF16)
    s = lax.dot_general(q, k_ref[...].astype(BF16), (((1,), (1,)), ((), ())), preferred_element_type=F32)
    p = jnp.exp(s - jnp.max(s, axis=-1, keepdims=True))
    num = jnp.dot(p.astype(BF16), v_ref[...].astype(BF16), preferred_element_type=F32)
    o_ref[...] = num / jnp.sum(p, axis=-1, keepdims=True)


def _na_bias(rpb, rows):
    kh = min(NA_KH, rows)
    col = np.arange(GRID_W)
    c0 = np.clip(col - NA_KW // 2, 0, GRID_W - NA_KW)
    key = np.arange(GRID_W)
    inside = (key[None, :] >= c0[:, None]) & (key[None, :] < c0[:, None] + NA_KW)
    dc = np.clip(key[None, :] - col[:, None] + (NA_KW - 1), 0, 2 * NA_KW - 2)
    onehot = np.zeros((GRID_W, GRID_W, 2 * NA_KW - 1), np.float32)
    onehot[col[:, None], key[None, :], dc] = 1.0
    toep = jnp.einsum("qcd,hrd->hrqc", onehot, rpb, precision=lax.Precision.HIGHEST)
    toep = jnp.where(jnp.asarray(inside), toep, NEG)
    return jnp.stack([jnp.concatenate([toep[:, i - j + NA_KH - 1] for i in range(kh)], axis=-1)
                      for j in range(NA_KH)], axis=1)


def na_mix(p, rpb):
    R = p.shape[0]
    T = R - CTX_LEN
    rows = T // GRID_W
    qb, kb, vb = [(AB_NA + i * MIX_HALF) // NA_HEAD_DIM for i in range(3)]
    qrows = NA_RB * GRID_W
    cb = CTX_LEN // qrows
    bias = _na_bias(rpb, rows)
    strip = lambda b: pl.BlockSpec((R, NA_HEAD_DIM), lambda h, r: (0, b + h))
    lat = pl.pallas_call(
        functools.partial(_na_kernel, rows=rows),
        grid=(NA_HEADS, rows // NA_RB),
        in_specs=[pl.BlockSpec((qrows, NA_HEAD_DIM), lambda h, r: (cb + r, qb + h)),
                  strip(kb), strip(vb),
                  pl.BlockSpec((1, NA_KH, GRID_W, NA_WIN), lambda h, r: (h, 0, 0, 0))],
        out_specs=pl.BlockSpec((qrows, NA_HEAD_DIM), lambda h, r: (r, h)),
        out_shape=jax.ShapeDtypeStruct((T, MIX_HALF), F32),
        compiler_params=_params(("parallel", "arbitrary")),
        name="na_latent",
    )(p, p, p, bias)
    blk = lambda b: pl.BlockSpec((CTX_LEN, NA_HEAD_DIM), lambda h: (0, b + h))
    ctx = pl.pallas_call(
        _na_ctx_kernel,
        grid=(NA_HEADS,),
        in_specs=[blk(qb), blk(kb), blk(vb)],
        out_specs=pl.BlockSpec((CTX_LEN, NA_HEAD_DIM), lambda h: (0, h)),
        out_shape=jax.ShapeDtypeStruct((CTX_LEN, MIX_HALF), F32),
        compiler_params=_params(("parallel",)),
        name="na_context",
    )(p, p, p)
    return jnp.concatenate([ctx, lat], axis=0)


ML_CHUNK = CTX_LEN
CD_GT = 2 * ML_HEADS * ML_DQK + 2 * MIX_HALF
CD_HY = CD_GT + LANE


def _mlstm_kernel(q_ref, k_ref, v_ref, lir_ref, lfr_ref, lic_ref, lfc_ref, o_ref, c_ref, n_ref, m_ref):
    L = ML_CHUNK

    @pl.when(pl.program_id(2) == 0)
    def _():
        c_ref[...] = jnp.zeros_like(c_ref)
        n_ref[...] = jnp.zeros_like(n_ref)
        m_ref[...] = jnp.zeros_like(m_ref)

    sgn = 1 - 2 * pl.program_id(0)
    row = lax.broadcasted_iota(jnp.int32, (L, L), 0)
    col = lax.broadcasted_iota(jnp.int32, (L, L), 1)
    seen = (row - col) * sgn >= 0
    seen_t = (col - row) * sgn >= 0
    q = q_ref[...]
    k = k_ref[...]
    vb = v_ref[...].astype(BF16)
    li_r, lf_r, li_c, lf_c = lir_ref[0], lfr_ref[0], lic_ref[0], lfc_ref[0]
    b_c = jnp.sum(jnp.where(seen, lf_r, 0.0), axis=1, keepdims=True)
    b_r = jnp.sum(jnp.where(seen_t, lf_c, 0.0), axis=0, keepdims=True)
    m_prev = m_ref[...]
    dmat = jnp.where(seen, b_c - b_r + li_r, NEG)
    inter = b_c + m_prev
    m_t = jnp.maximum(jnp.max(dmat, axis=1, keepdims=True), inter)
    qb = q.astype(BF16)
    s = lax.dot_general(qb, k.astype(BF16), (((1,), (1,)), ((), ())), preferred_element_type=F32)
    s = s * jnp.exp(dmat - m_t)
    dec = jnp.exp(inter - m_t)
    num = jnp.dot(s.astype(BF16), vb, preferred_element_type=F32)
    num += dec * jnp.dot(qb, c_ref[...].astype(BF16), preferred_element_type=F32)
    den = jnp.sum(s, axis=1, keepdims=True) + dec * jnp.sum(q * n_ref[...], axis=1, keepdims=True)
    o_ref[0] = num / jnp.maximum(jnp.abs(den), jnp.exp(-m_t))
    total = jnp.sum(lf_r, axis=1, keepdims=True)
    gl = total - b_c + li_c
    m_new = jnp.maximum(total + m_prev, jnp.max(gl, axis=0, keepdims=True))
    kw = k * jnp.exp(gl - m_new)
    sc = jnp.exp(total + m_prev - m_new)
    c_ref[...] = sc * c_ref[...] + lax.dot_general(kw.astype(BF16), vb, (((0,), (0,)), ((), ())),
                                                   preferred_element_type=F32)
    n_ref[...] = sc * n_ref[...] + jnp.sum(kw, axis=0, keepdims=True)
    m_ref[...] = m_new


def _axial_rope(x):
    T, d = x.shape[0], x.shape[-1]
    half = d // 2
    nf = half // 2
    t = jnp.arange(T)
    row = (t // GRID_W).astype(F32)
    col = (t % GRID_W).astype(F32)
    inv = ROPE_BASE ** (-jnp.arange(nf, dtype=F32) / nf)
    ang = jnp.concatenate([row[:, None] * inv, col[:, None] * inv], axis=-1)[:, None, :]
    cos, sin = jnp.cos(ang), jnp.sin(ang)
    x1, x2 = x[..., :half], x[..., half:]
    return jnp.concatenate([x1 * cos - x2 * sin, x2 * cos + x1 * sin], axis=-1)


def mlstm_mix(p, gate_b, norm_w):
    R = p.shape[0]
    L = ML_CHUNK
    nc = R // L
    QK = ML_HEADS * ML_DQK

    def rope_lat(t):
        th = t.reshape(R, ML_HEADS, ML_DQK)
        return jnp.concatenate([th[:CTX_LEN], _axial_rope(th[CTX_LEN:])], axis=0).reshape(R, QK)

    q = rope_lat(p[:, :QK])
    k = rope_lat(p[:, QK:2 * QK]) * ML_DQK ** -0.5
    o = p[:, 2 * QK + MIX_HALF:CD_GT]
    gt = p[:, CD_GT:CD_GT + 4 * ML_HEADS].reshape(R, 2, 2, ML_HEADS) + gate_b
    log_i = jnp.transpose(gt[:, :, 0], (1, 2, 0))
    log_f = jax.nn.log_sigmoid(jnp.transpose(gt[:, :, 1], (1, 2, 0)))
    as_rows = lambda t: t.reshape(2 * ML_HEADS * nc, 1, L)
    as_cols = lambda t: t.reshape(2 * ML_HEADS * nc, L, 1)

    def chunk(d, c):
        return jnp.where(d == 0, c, jnp.where(c == 0, 0, nc - c))

    gidx = lambda d, h, c: ((d * ML_HEADS + h) * nc + chunk(d, c), 0, 0)
    vb = 2 * QK // ML_DV
    h_dir = pl.pallas_call(
        _mlstm_kernel,
        grid=(2, ML_HEADS, nc),
        in_specs=[pl.BlockSpec((L, ML_DQK), lambda d, h, c: (chunk(d, c), h)),
                  pl.BlockSpec((L, ML_DQK), lambda d, h, c: (chunk(d, c), h)),
                  pl.BlockSpec((L, ML_DV), lambda d, h, c: (chunk(d, c), vb + h)),
                  pl.BlockSpec((1, 1, L), gidx), pl.BlockSpec((1, 1, L), gidx),
                  pl.BlockSpec((1, L, 1), gidx), pl.BlockSpec((1, L, 1), gidx)],
        out_specs=pl.BlockSpec((1, L, ML_DV), lambda d, h, c: (d, chunk(d, c), h)),
        out_shape=jax.ShapeDtypeStruct((2, R, MIX_HALF), F32),
        scratch_shapes=[pltpu.VMEM((ML_DQK, ML_DV), F32), pltpu.VMEM((1, ML_DQK), F32),
                        pltpu.VMEM((1, 1), F32)],
        compiler_params=_params(("parallel", "parallel", "arbitrary")),
        name="mlstm",
    )(q, k, p, as_rows(log_i), as_rows(log_f), as_cols(log_i), as_cols(log_f))
    h = (h_dir[0] + h_dir[1]).reshape(R, ML_HEADS, ML_DV)
    h = h * lax.rsqrt(jnp.mean(h * h, axis=-1, keepdims=True) + RMS_EPS)
    return h.reshape(R, MIX_HALF) * norm_w * jax.nn.sigmoid(o)


def _pad_to(x, axis, mult):
    n = -x.shape[axis] % mult
    if n == 0:
        return x
    pad = [(0, 0)] * x.ndim
    pad[axis] = (0, n)
    return jnp.pad(x, pad)


def mm3p(a, b):
    M, N = a.shape[0], b.shape[1]
    a = _pad_to(_pad_to(a, 1, LANE), 0, 8)
    b = _pad_to(_pad_to(b, 0, LANE), 1, LANE)
    return mm3(a, b)[:M, :N]


def hyena_filters(T, ff1, ff1_b, ff2, ff2_b, ff3, ff3_b, ff_out, freq, decay):
    t = jnp.linspace(0.0, 1.0, T, dtype=F32)[:, None]
    bands = (HY_EMB - 1) // 2
    f = jnp.linspace(1e-4, bands - 1, bands, dtype=F32)[None, :]
    w = (2 * math.pi / T) * jnp.arange(T, dtype=F32)[:, None]
    z = jnp.concatenate([t, jnp.cos(f * w), -jnp.sin(f * w)], axis=-1)
    h = jnp.sin(freq * (mm3p(z, ff1) + ff1_b))
    h = jnp.sin(freq * (mm3p(h, ff2) + ff2_b))
    h = jnp.sin(freq * (mm3p(h, ff3) + ff3_b))
    h = mm3p(h, ff_out) * jnp.exp(-t * jnp.abs(decay))
    return h.reshape(T, HY_ORDER, 2, MIX_HALF)


FFT_N2 = LANE


@functools.lru_cache(maxsize=None)
def _dft_tables(n1):
    n2 = FFT_N2
    n = n1 * n2
    i1, i2 = np.arange(n1), np.arange(n2)
    a1 = 2 * np.pi * np.outer(i1, i1) / n1
    a2 = 2 * np.pi * np.outer(i2, i2) / n2
    c1, s1 = np.cos(a1), -np.sin(a1)
    c2, s2 = np.cos(a2), -np.sin(a2)
    atw = 2 * np.pi * np.outer(i2, i1) / n
    f32 = lambda x: np.asarray(x, np.float32)
    return dict(
        wa=f32(np.concatenate([c1, s1], axis=1)),
        wb=f32(np.block([[c2, s2], [-s2, c2]])),
        wb_inv=f32(np.block([[c2, -s2], [s2, c2]])),
        wa_inv=f32(np.concatenate([c1, s1], axis=0)[:, :n1 // 2] / n),
        twr=f32(np.cos(atw)), twi=f32(-np.sin(atw)))


def _fft_fwd(x_t, n1):
    C = x_t.shape[0]
    n2 = FFT_N2
    tb = _dft_tables(n1)
    xa = jnp.transpose(x_t.reshape(C, n1, n2), (0, 2, 1)).reshape(C * n2, n1)
    a = mm3p(xa, tb["wa"]).reshape(C, n2, 2, n1)
    ar, ai = a[:, :, 0], a[:, :, 1]
    br = ar * tb["twr"] - ai * tb["twi"]
    bi = ar * tb["twi"] + ai * tb["twr"]
    b = jnp.concatenate([jnp.transpose(br, (0, 2, 1)), jnp.transpose(bi, (0, 2, 1))], axis=-1)
    return mm3(b.reshape(C * n1, 2 * n2), tb["wb"])


def _fft_inv(y, C, n1):
    n2 = FFT_N2
    tb = _dft_tables(n1)
    g = mm3(y, tb["wb_inv"]).reshape(C, n1, 2, n2)
    gr, gi = g[:, :, 0], g[:, :, 1]
    twr, twi = tb["twr"].T, tb["twi"].T
    hr = gr * twr + gi * twi
    hi = gi * twr - gr * twi
    h = jnp.concatenate([jnp.transpose(hr, (0, 2, 1)), jnp.transpose(hi, (0, 2, 1))], axis=-1)
    out = mm3p(h.reshape(C * n2, 2 * n1), tb["wa_inv"])
    return jnp.transpose(out.reshape(C, n2, n1 // 2), (0, 2, 1)).reshape(C, n1 // 2 * n2)


def _long_conv(z, h_fwd, h_bwd):
    T, C = z.shape
    n1 = 2 * T // FFT_N2
    f = jnp.concatenate([h_fwd, jnp.zeros((1, C), F32), h_bwd[:0:-1]], axis=0)
    zs = _fft_fwd(jnp.pad(z.T, ((0, 0), (0, T))), n1).reshape(C * n1, 2, FFT_N2)
    fs = _fft_fwd(f.T, n1).reshape(C * n1, 2, FFT_N2)
    yr = zs[:, 0] * fs[:, 0] - zs[:, 1] * fs[:, 1]
    yi = zs[:, 0] * fs[:, 1] + zs[:, 1] * fs[:, 0]
    return _fft_inv(jnp.concatenate([yr, yi], axis=-1), C, n1).T


HY_Q = 4
HY_F1Q = FFT_N2 // HY_Q
HY_F2H = FFT_N2 // 2
HY_QR = HY_F1Q * HY_F2H
HY_NN = FFT_N2 * FFT_N2


@functools.lru_cache(maxsize=None)
def _hy_tables():
    n = FFT_N2
    i = np.arange(n)
    w = np.exp(-2j * np.pi * np.outer(i, i) / n)
    tw = np.exp(-2j * np.pi * np.outer(i, i) / (n * n))
    ca = np.conj(w)[: n // 2] / (n * n)
    f32 = lambda x: np.ascontiguousarray(x, dtype=np.float32)
    return dict(
        wa=f32(np.concatenate([w.real, w.imag], axis=0)),
        wbr=f32(w.real), wbi=f32(w.imag),
        wbc=f32(2 * np.block([[w.real[:, : n // 2], w.imag[:, : n // 2]],
                              [-w.imag[:, : n // 2], w.real[:, : n // 2]]])),
        twr=f32(tw.real).reshape(n, 1, n), twi=f32(tw.imag).reshape(n, 1, n),
        ctr=f32(tw.real.T).reshape(n, 1, n), cti=f32(-tw.imag.T).reshape(n, 1, n),
        car=f32(ca.real), cai=f32(ca.imag))


def _dot3(a_hi, a_lo, b):
    b_hi, b_lo = _split(b)
    acc = jnp.dot(a_hi, b_hi, preferred_element_type=F32)
    acc += jnp.dot(a_hi, b_lo, preferred_element_type=F32)
    return acc + jnp.dot(a_lo, b_hi, preferred_element_type=F32)


def _hy_fwd_kernel(x_ref, wa_ref, wbr_ref, wbi_ref, twr_ref, twi_ref, o_ref, e_ref, a_ref, *, k1):
    n = FFT_N2
    h = HY_F2H
    q = pl.program_id(1)

    @pl.when(q == 0)
    def _():
        wa_hi, wa_lo = _split(wa_ref[...])
        for t2 in range(n):
            y = _dot3(wa_hi, wa_lo, x_ref[pl.ds(t2, k1, stride=n), :])
            a_ref[pl.ds(t2, n, stride=n), :] = y[:n]
            a_ref[pl.ds(HY_NN + t2, n, stride=n), :] = y[n:]
        col = a_ref[pl.ds(0, n), :]
        sign = (1 - 2 * (lax.broadcasted_iota(jnp.int32, (n, 1), 0) % 2)).astype(F32)
        dc = jnp.sum(col, axis=0, keepdims=True)
        nyq = jnp.sum(col * sign, axis=0, keepdims=True)
        r = lax.broadcasted_iota(jnp.int32, e_ref.shape, 0)
        e_ref[...] = jnp.where(r == 0, dc, jnp.where(r == 1, nyq, 0.0))

    cr, ci = wbr_ref[pl.ds(0, h), :], wbi_ref[pl.ds(0, h), :]

    def body(j, carry):
        f1 = q * HY_F1Q + j
        base = pl.multiple_of(f1 * n, n)
        rhs = jnp.concatenate([a_ref[pl.ds(base, n), :], a_ref[pl.ds(HY_NN + base, n), :]], axis=0)
        twr, twi = twr_ref[f1], twi_ref[f1]
        pr = cr * twr - ci * twi
        pi = cr * twi + ci * twr
        lhs = jnp.concatenate([jnp.concatenate([pr, -pi], axis=1), jnp.concatenate([pi, pr], axis=1)], axis=0)
        out = _dot3(*_split(lhs), rhs)
        ob = pl.multiple_of(j * h, h)
        o_ref[pl.ds(ob, h), :] = out[:h]
        o_ref[pl.ds(HY_QR + ob, h), :] = out[h:]
        return carry

    lax.fori_loop(0, HY_F1Q, body, 0, unroll=2)


def hy_fwd(x):
    T_in, C = x.shape
    k1 = T_in // FFT_N2
    tb = _hy_tables()
    const = lambda a: pl.BlockSpec(a.shape, lambda c, q: (0,) * a.ndim)
    wa = tb["wa"][:, :k1]
    args = (wa, tb["wbr"], tb["wbi"], tb["twr"], tb["twi"])
    return pl.pallas_call(
        functools.partial(_hy_fwd_kernel, k1=k1),
        grid=(C // LANE, HY_Q),
        in_specs=[pl.BlockSpec((T_in, LANE), lambda c, q: (0, c))] + [const(a) for a in args],
        out_specs=[pl.BlockSpec((None, 2 * HY_QR, LANE), lambda c, q: (q, 0, c)),
                   pl.BlockSpec((8, LANE), lambda c, q: (0, c))],
        out_shape=[jax.ShapeDtypeStruct((HY_Q, 2 * HY_QR, C), F32), jax.ShapeDtypeStruct((8, C), F32)],
        scratch_shapes=[pltpu.VMEM((2 * HY_NN, LANE), F32)],
        compiler_params=_params(("parallel", "arbitrary")),
        name="hy_fwd",
    )(x, *args)


def _hy_conv_kernel(z_ref, ze_ref, f_ref, fe_ref, wbc_ref, ctr_ref, cti_ref, car_ref, cai_ref, y_ref, g_ref):
    n = FFT_N2
    h = HY_F2H
    q = pl.program_id(1)
    l_hi, l_lo = _split(wbc_ref[...])

    def body(j, carry):
        ob = pl.multiple_of(j * h, h)
        zr, zi = z_ref[pl.ds(ob, h), :], z_ref[pl.ds(HY_QR + ob, h), :]
        fr, fi = f_ref[pl.ds(ob, h), :], f_ref[pl.ds(HY_QR + ob, h), :]
        prod = jnp.concatenate([zr * fr - zi * fi, zr * fi + zi * fr], axis=0)
        out = _dot3(l_hi, l_lo, prod)
        base = pl.multiple_of((q * HY_F1Q + j) * n, n)
        g_ref[pl.ds(base, n), :] = out[:n]
        g_ref[pl.ds(HY_NN + base, n), :] = out[n:]
        return carry

    lax.fori_loop(0, HY_F1Q, body, 0, unroll=2)

    @pl.when(q == HY_Q - 1)
    def _():
        car, cai = car_ref[...], cai_ref[...]
        y_dc = ze_ref[0:1, :] * fe_ref[0:1, :] * (-1.0 / HY_NN)
        y_nyq = ze_ref[1:2, :] * fe_ref[1:2, :] * (1.0 / HY_NN)
        fix = (y_dc + y_nyq, y_dc - y_nyq)
        for t2 in range(n):
            ctr, cti = ctr_ref[t2], cti_ref[t2]
            er = car * ctr - cai * cti
            ei = car * cti + cai * ctr
            rhs = jnp.concatenate([g_ref[pl.ds(t2, n, stride=n), :],
                                   g_ref[pl.ds(HY_NN + t2, n, stride=n), :]], axis=0)
            y_ref[pl.ds(t2, n // 2, stride=n), :] = (
                _dot3(*_split(jnp.concatenate([er, -ei], axis=1)), rhs) + fix[t2 % 2])


def hy_conv(z, f, f_block=0):
    (zs, ze), (fs, fe) = z, f
    C = zs.shape[-1]
    T = HY_NN // 2
    tb = _hy_tables()
    const = lambda a: pl.BlockSpec(a.shape, lambda c, q: (0,) * a.ndim)
    spec = pl.BlockSpec((None, 2 * HY_QR, LANE), lambda c, q: (q, 0, c))
    fspec = pl.BlockSpec((None, 2 * HY_QR, LANE), lambda c, q: (q, 0, c + f_block))
    espec = pl.BlockSpec((8, LANE), lambda c, q: (0, c))
    fespec = pl.BlockSpec((8, LANE), lambda c, q: (0, c + f_block))
    args = (tb["wbc"], tb["ctr"], tb["cti"], tb["car"], tb["cai"])
    return pl.pallas_call(
        _hy_conv_kernel,
        grid=(C // LANE, HY_Q),
        in_specs=[spec, espec, fspec, fespec] + [const(a) for a in args],
        out_specs=pl.BlockSpec((T, LANE), lambda c, q: (0, c)),
        out_shape=jax.ShapeDtypeStruct((T, C), F32),
        scratch_shapes=[pltpu.VMEM((2 * HY_NN, LANE), F32)],
        compiler_params=_params(("parallel", "arbitrary")),
        name="hy_conv",
    )(zs, ze, fs, fe, *args)


def hyena_filter_taps(T, ff1, ff1_b, ff2, ff2_b, ff3, ff3_b, ff_out, freq, decay):
    t = jnp.linspace(0.0, 1.0, T, dtype=F32)[:, None]
    bands = (HY_EMB - 1) // 2
    f = jnp.linspace(1e-4, bands - 1, bands, dtype=F32)[None, :]
    w = (2 * math.pi / T) * jnp.arange(T, dtype=F32)[:, None]
    z = jnp.concatenate([t, jnp.cos(f * w), -jnp.sin(f * w)], axis=-1)
    h = jnp.sin(freq * (mm3p(z, ff1) + ff1_b))
    h = jnp.sin(freq * (mm3p(h, ff2) + ff2_b))
    h = jnp.sin(freq * (mm3p(h, ff3) + ff3_b))
    w_out = ff_out.reshape(-1, HY_ORDER, 2, MIX_HALF)
    dec = jnp.abs(decay).reshape(HY_ORDER, 2, MIX_HALF)
    side = lambda s: (w_out[:, :, s].reshape(-1, HY_ORDER * MIX_HALF), dec[:, s].reshape(HY_ORDER * MIX_HALF))
    (w_f, d_f), (w_b, d_b) = side(0), side(1)
    fwd = mm3p(h, w_f) * jnp.exp(-t * d_f)
    bwd = mm3p(h[::-1], w_b) * jnp.exp(-t[::-1] * d_b)
    return jnp.concatenate([fwd, jnp.zeros((1, fwd.shape[1]), F32), bwd[:T - 1]], axis=0)


@functools.lru_cache(maxsize=None)
def _dft_small(n):
    i = np.arange(n)
    w = np.exp(-2j * np.pi * np.outer(i, i) / n)
    fwd = np.concatenate([w.real, w.imag], axis=0)
    inv = np.concatenate([w.real, w.imag], axis=1)[: n // 2] / n
    return np.asarray(fwd, np.float32), np.asarray(inv, np.float32)


def _short_conv(z, h_fwd, h_bwd):
    T, C = z.shape
    n = 2 * T
    fwd, inv = _dft_small(n)
    f = jnp.concatenate([h_fwd, jnp.zeros((1, C), F32), h_bwd[:0:-1]], axis=0)
    zs = mm3(fwd[:, :T], z)
    fs = mm3(fwd, f)
    prod = jnp.concatenate([zs[:n] * fs[:n] - zs[n:] * fs[n:], zs[:n] * fs[n:] + zs[n:] * fs[:n]], axis=0)
    return mm3(inv, prod)


def hyena_seq(p, short_w, short_b, conv, bias):
    u = short_w[0] * _shift_prev(p) + short_w[1] * p + short_w[2] * _shift_next(p) + short_b
    z, x1, x2 = jnp.split(u, 3, axis=-1)
    for n, gate in enumerate((x1, x2)):
        z = gate * (conv(z, n) + bias[n] * z)
    return z


def hyena_mix(p, short_w, short_b, filt_params, bias):
    T = p.shape[0] - CTX_LEN
    filt_c = hyena_filters(CTX_LEN, *filt_params)
    out_c = hyena_seq(p[:CTX_LEN], short_w, short_b,
                      lambda z, n: _short_conv(z, filt_c[:, n, 0], filt_c[:, n, 1]), bias)
    if 2 * T == HY_NN:
        spectra = hy_fwd(hyena_filter_taps(T, *filt_params))
        conv = lambda z, n: hy_conv(hy_fwd(z), spectra, n * MIX_HALF // LANE)
    else:
        filt = hyena_filters(T, *filt_params)
        conv = lambda z, n: _long_conv(z, filt[:, n, 0], filt[:, n, 1])
    out = hyena_seq(p[CTX_LEN:], short_w, short_b, conv, bias)
    return jnp.concatenate([out_c, out], axis=0)


def kernel(x, c, ctx, c_ctx, ada_down, ada_up, ada_bias, norm_g, ffn_w1, ffn_w3, ffn_w2, ab_w_in, ab_w_out, rwkv_mu, rwkv_w0, rwkv_w2, rwkv_a0, rwkv_a2, rwkv_g2, rwkv_kk, rwkv_ka, rwkv_rk, rwkv_gn, na_rpb, cd_w_in, cd_w_out, ml_gate_b, ml_norm, hy_short_w, hy_short_b, hy_ff1, hy_ff1_b, hy_ff2, hy_ff2_b, hy_ff3, hy_ff3_b, hy_ff_out, hy_freq, hy_decay, hy_bias):
    assert x.shape[0] == 1 and ctx.shape[1] == CTX_LEN
    X = jnp.concatenate([ctx[0], x[0]], axis=0)
    s2 = _pad_to(jnp.stack([jax.nn.silu(c_ctx), jax.nn.silu(c[0])]), 0, 16)
    n = jnp.arange(MIX_HALF)
    perm = ((n % RWKV_HEADS) * RWKV_HEAD + n // RWKV_HEADS == n[:, None]).astype(BF16)
    mods = [(mm(mm(s2, ada_down[l]), ada_up[l])[:2] + ada_bias[l]).reshape(2, N_SUB, 3, D_MODEL)
            for l in range(DEPTH)]
    hn = norm_mod(X, norm_g[0, 0], mods[0][:, 0, 0], mods[0][:, 0, 1])
    for l in range(DEPTH):
        i = l // 2
        mod = mods[l]
        g = norm_g[l]
        y = mm_w32(swiglu(hn, ffn_w1, ffn_w3, (l, 0)), ffn_w2, (l, 0))
        X, xn = post_res_norm(X, y, g[1], mod[:, 0, 2], FFN_RES, g[2], mod[:, 1, 0], mod[:, 1, 1])
        if l % 2 == 0:
            w_ab = ab_w_in[i].astype(BF16)
            w_rkv = [mm(t, perm, BF16) for t in jnp.split(w_ab[:, :AB_WD], 3, axis=-1)]
            p = mm(xn, _pack_cols(jnp.concatenate(w_rkv + [w_ab[:, AB_WD:]], axis=-1), _AB_CUTS, _AB_WIDTHS))
            ya = rwkv_mix(p, rwkv_mu[i], rwkv_w0[i], rwkv_w2[i], rwkv_a0[i], rwkv_a2[i],
                          rwkv_g2[i], rwkv_kk[i], rwkv_ka[i], rwkv_rk[i], rwkv_gn[i])
            y = jnp.concatenate([ya, na_mix(p, na_rpb[i])], axis=-1)
            w_out = ab_w_out[i].astype(BF16)
            w_rw = mm(perm.T, w_out[:MIX_HALF], BF16)
            yo = mm(y, jnp.concatenate([w_rw, w_out[MIX_HALF:]], axis=0))
        else:
            w_in = _pack_cols(cd_w_in[i].astype(BF16), (CD_GT, ML_COLS), (CD_GT, LANE, P_COLS - CD_HY))
            p = mm(xn, w_in)
            filt_params = (hy_ff1[i], hy_ff1_b[i], hy_ff2[i], hy_ff2_b[i], hy_ff3[i], hy_ff3_b[i],
                           hy_ff_out[i], hy_freq[i], hy_decay[i])
            yd = hyena_mix(p[:, CD_HY:CD_HY + 3 * MIX_HALF], hy_short_w[i], hy_short_b[i], filt_params,
                           hy_bias[i])
            y = jnp.concatenate([mlstm_mix(p, ml_gate_b[i], ml_norm[i]), yd], axis=-1)
            yo = mm_w32(y.astype(BF16), cd_w_out, (i,))
        X, hn = post_res_norm(X, yo, g[3], mod[:, 1, 2], 1.0, g[4], mod[:, 2, 0], mod[:, 2, 1])
        y = mm_w32(swiglu(hn, ffn_w1, ffn_w3, (l, 1)), ffn_w2, (l, 1))
        if l + 1 < DEPTH:
            nxt = mods[l + 1]
            X, hn = post_res_norm(X, y, g[5], mod[:, 2, 2], FFN_RES, norm_g[l + 1, 0], nxt[:, 0, 0], nxt[:, 0, 1])
        else:
            X = post_res(X, y, g[5], mod[:, 2, 2], FFN_RES)
    return X[CTX_LEN:][None]
```

```python
import functools
import math

import jax
import jax.numpy as jnp
import numpy as np
from jax import lax
from jax.experimental import pallas as pl
from jax.experimental.pallas import tpu as pltpu

F32 = jnp.float32
BF16 = jnp.bfloat16

D_MODEL = 4096
DEPTH = 4
GRID_W = 64
CTX_LEN = 256
N_SUB = 3
FFN_RES = 0.5
D_FF = 5632
RMS_EPS = 1e-6
MIX_HALF = D_MODEL // 2

RWKV_HEAD = 64
RWKV_HEADS = MIX_HALF // RWKV_HEAD
RWKV_DECAY_LORA = 96
RWKV_A_LORA = 96
RWKV_GATE_LORA = 256
RWKV_COLS = 3 * MIX_HALF + RWKV_DECAY_LORA + RWKV_A_LORA + RWKV_GATE_LORA
RWKV_GN_EPS = 64e-5

NA_HEAD_DIM = 128
NA_HEADS = MIX_HALF // NA_HEAD_DIM
NA_KH = 8
NA_KW = 16

ML_HEADS = 4
ML_DV = MIX_HALF // ML_HEADS
ML_DQK = ML_DV // 2
ML_COLS = 2 * ML_HEADS * ML_DQK + 2 * MIX_HALF + 4 * ML_HEADS
ROPE_BASE = 10000.0

HY_ORDER = 2
HY_EMB = 33

LANE = 128
ROW_TILE = CTX_LEN
VMEM_LIMIT = 56 * 1024 * 1024
P_COLS = 12800
NEG = -1e30


def _params(sem):
    return pltpu.CompilerParams(dimension_semantics=sem, vmem_limit_bytes=VMEM_LIMIT)


def _pick(n, cands):
    for c in cands:
        if n % c == 0:
            return c
    return n


def _mm_kernel(a_ref, b_ref, o_ref):
    o_ref[...] = jnp.dot(a_ref[...], b_ref[...], preferred_element_type=F32).astype(o_ref.dtype)


def _split(x):
    hi = x.astype(BF16)
    lo = (x - hi.astype(F32)).astype(BF16)
    return hi, lo


def _mm3_kernel(a_ref, b_ref, o_ref):
    a_hi, a_lo = _split(a_ref[...])
    b_hi, b_lo = _split(b_ref[...])
    acc = jnp.dot(a_hi, b_hi, preferred_element_type=F32)
    acc += jnp.dot(a_hi, b_lo, preferred_element_type=F32)
    acc += jnp.dot(a_lo, b_hi, preferred_element_type=F32)
    o_ref[...] = acc


def _mm_call(kern, a, b, out_dtype, name):
    M, K = a.shape
    N = b.shape[1]
    tm = _pick(M, (1024, 768, 512, 256, 128, 64, 32, 16, 8))
    tn = _pick(N, (512, 256, 128))
    return pl.pallas_call(
        kern,
        grid=(M // tm, N // tn),
        in_specs=[pl.BlockSpec((tm, K), lambda i, j: (i, 0)),
                  pl.BlockSpec((K, tn), lambda i, j: (0, j))],
        out_specs=pl.BlockSpec((tm, tn), lambda i, j: (i, j)),
        out_shape=jax.ShapeDtypeStruct((M, N), out_dtype),
        compiler_params=_params(("parallel", "parallel")),
        name=name,
    )(a, b)


def mm(a, b, out_dtype=F32):
    return _mm_call(_mm_kernel, a.astype(BF16), b.astype(BF16), out_dtype, "mm")


def mm3(a, b):
    return _mm_call(_mm3_kernel, a.astype(F32), b.astype(F32), F32, "mm3")


def _mm_w32_kernel(a_ref, w_ref, o_ref, wb_ref):
    @pl.when(pl.program_id(1) == 0)
    def _():
        wb_ref[...] = w_ref[...].astype(BF16)

    o_ref[...] = jnp.dot(a_ref[...], wb_ref[...], preferred_element_type=F32).astype(o_ref.dtype)


def _swiglu_kernel(a_ref, w1_ref, w3_ref, o_ref, w1b_ref, w3b_ref):
    @pl.when(pl.program_id(1) == 0)
    def _():
        w1b_ref[...] = w1_ref[...].astype(BF16)
        w3b_ref[...] = w3_ref[...].astype(BF16)

    a = a_ref[...]
    h1 = jnp.dot(a, w1b_ref[...], preferred_element_type=F32)
    h3 = jnp.dot(a, w3b_ref[...], preferred_element_type=F32)
    o_ref[...] = (h1 * jax.nn.sigmoid(h1) * h3).astype(o_ref.dtype)


def _w32_call(kern, a, ws, tms, tn, out_dtype, name, lead=()):
    M, K = a.shape
    N = ws[0].shape[-1]
    tm = _pick(M, tms)
    wspec = pl.BlockSpec((None,) * len(lead) + (K, tn), lambda j, i: tuple(lead) + (0, j))
    return pl.pallas_call(
        kern,
        grid=(N // tn, M // tm),
        in_specs=[pl.BlockSpec((tm, K), lambda j, i: (i, 0))] + [wspec] * len(ws),
        out_specs=pl.BlockSpec((tm, tn), lambda j, i: (i, j)),
        out_shape=jax.ShapeDtypeStruct((M, N), out_dtype),
        scratch_shapes=[pltpu.VMEM((K, tn), BF16)] * len(ws),
        compiler_params=_params(("parallel", "arbitrary")),
        name=name,
    )(a, *ws)


def swiglu(a, w1, w3, lead=()):
    return _w32_call(_swiglu_kernel, a, (w1, w3), (1408, 768, 512, 256), 256, BF16, "swiglu", lead)


def mm_w32(a, w, lead=()):
    return _w32_call(_mm_w32_kernel, a, (w,), (768, 512, 256), 512, F32, "mm_w32", lead)


def _group_spec(d):
    return pl.BlockSpec((1, 1, d), lambda i: (jnp.minimum(i, 1), 0, 0))


def _norm_mod_kernel(x_ref, g_ref, sh_ref, sc_ref, o_ref):
    x = x_ref[...]
    y = x * lax.rsqrt(jnp.mean(x * x, axis=-1, keepdims=True) + RMS_EPS) * g_ref[...]
    o_ref[...] = (y * (1 + sc_ref[0]) + sh_ref[0]).astype(o_ref.dtype)


def norm_mod(x, g, shift, scale):
    R, D = x.shape
    row = pl.BlockSpec((ROW_TILE, D), lambda i: (i, 0))
    return pl.pallas_call(
        _norm_mod_kernel,
        grid=(R // ROW_TILE,),
        in_specs=[row, pl.BlockSpec((1, D), lambda i: (0, 0)), _group_spec(D), _group_spec(D)],
        out_specs=row,
        out_shape=jax.ShapeDtypeStruct((R, D), BF16),
        compiler_params=_params(("parallel",)),
        name="norm_mod",
    )(x, g.reshape(1, D), shift.reshape(2, 1, D), scale.reshape(2, 1, D))


def _post_res_kernel(x_ref, y_ref, g_ref, gate_ref, o_ref, *, coef):
    y = y_ref[...]
    yn = y * lax.rsqrt(jnp.mean(y * y, axis=-1, keepdims=True) + RMS_EPS) * g_ref[...]
    o_ref[...] = x_ref[...] + coef * gate_ref[0] * yn


def _post_res_norm_kernel(x_ref, y_ref, g_ref, gate_ref, gn_ref, sh_ref, sc_ref, o_ref, h_ref, *, coef):
    y = y_ref[...]
    yn = y * lax.rsqrt(jnp.mean(y * y, axis=-1, keepdims=True) + RMS_EPS) * g_ref[...]
    x = x_ref[...] + coef * gate_ref[0] * yn
    o_ref[...] = x
    h = x * lax.rsqrt(jnp.mean(x * x, axis=-1, keepdims=True) + RMS_EPS) * gn_ref[...]
    h_ref[...] = (h * (1 + sc_ref[0]) + sh_ref[0]).astype(h_ref.dtype)


def post_res_norm(x, y, g, gate, coef, g_next, shift_next, scale_next):
    R, D = x.shape
    row = pl.BlockSpec((ROW_TILE, D), lambda i: (i, 0))
    vec = pl.BlockSpec((1, D), lambda i: (0, 0))
    grp = lambda t: t.reshape(2, 1, D)
    return pl.pallas_call(
        functools.partial(_post_res_norm_kernel, coef=coef),
        grid=(R // ROW_TILE,),
        in_specs=[row, row, vec, _group_spec(D), vec, _group_spec(D), _group_spec(D)],
        out_specs=[row, row],
        out_shape=[jax.ShapeDtypeStruct((R, D), F32), jax.ShapeDtypeStruct((R, D), BF16)],
        compiler_params=_params(("parallel",)),
        name="post_res_norm",
    )(x, y, g.reshape(1, D), grp(gate), g_next.reshape(1, D), grp(shift_next), grp(scale_next))


def post_res(x, y, g, gate, coef):
    R, D = x.shape
    row = pl.BlockSpec((ROW_TILE, D), lambda i: (i, 0))
    return pl.pallas_call(
        functools.partial(_post_res_kernel, coef=coef),
        grid=(R // ROW_TILE,),
        in_specs=[row, row, pl.BlockSpec((1, D), lambda i: (0, 0)), _group_spec(D)],
        out_specs=row,
        out_shape=jax.ShapeDtypeStruct((R, D), F32),
        compiler_params=_params(("parallel",)),
        name="post_res",
    )(x, y, g.reshape(1, D), gate.reshape(2, 1, D))


WKV_TB = 8
WKV_KG = MIX_HALF // LANE
WKV_VC = RWKV_HEAD // 8
WKV_CG = 2
PREP_TB = 128


def _kmajor(w):
    lead = w.shape[:-1]
    return jnp.swapaxes(w.reshape(*lead, RWKV_HEADS, RWKV_HEAD), -1, -2).reshape(*lead, MIX_HALF)


def _fold_lanes(s):
    s = s + pltpu.roll(s, 64, 1)
    return s + pltpu.roll(s, 32, 1)


def _rwkv_prep_kernel(p_ref, hp_ref, hn_ref, mu_ref, w2_ref, a2_ref, g2_ref, vec_ref,
                      a_out, b0_out, b1_out, k0_out, k1_out, w0_out, w1_out, r_out, v_out, g_out, bonus_out):
    W = MIX_HALF
    i = pl.program_id(0)
    cb = CTX_LEN // PREP_TB
    has_prev = jnp.logical_and(i != 0, i != cb).astype(F32)
    has_next = jnp.logical_and(i != cb - 1, i != pl.num_programs(0) - 1).astype(F32)
    row = lax.broadcasted_iota(jnp.int32, (PREP_TB, 1), 0)

    def shifted(c0, width):
        x = p_ref[:, c0:c0 + width]
        up = jnp.where(row == 0, hp_ref[7:8, c0:c0 + width] * has_prev, pltpu.roll(x, 1, 0))
        dn = jnp.where(row == PREP_TB - 1, hn_ref[0:1, c0:c0 + width] * has_next,
                       pltpu.roll(x, PREP_TB - 1, 0))
        return x + mu_ref[0:1, c0:c0 + width] * (up - x) + mu_ref[1:2, c0:c0 + width] * (dn - x)

    lo = shifted(AB_WD, AB_NA - AB_WD)
    wd, ad, gd = lo[:, :LANE], lo[:, LANE:2 * LANE], lo[:, 2 * LANE:]
    wlo = jnp.dot(jnp.tanh(wd).astype(BF16), w2_ref[...], preferred_element_type=F32)
    alo = jnp.dot(ad.astype(BF16), a2_ref[...], preferred_element_type=F32)
    g_out[...] = jnp.dot(jax.nn.sigmoid(gd).astype(BF16), g2_ref[...], preferred_element_type=F32)

    ss = None
    for g in range(WKV_KG):
        sl = slice(g * LANE, (g + 1) * LANE)
        kk = shifted(W + g * LANE, LANE) * vec_ref[4:5, sl]
        ss = kk * kk if ss is None else ss + kk * kk
    nrm = jnp.maximum(jnp.sqrt(_fold_lanes(ss)), 1e-12)
    bonus = None
    for g in range(WKV_KG):
        sl = slice(g * LANE, (g + 1) * LANE)
        r = shifted(g * LANE, LANE)
        k = shifted(W + g * LANE, LANE)
        r_out[:, sl] = r
        v_out[:, sl] = shifted(2 * W + g * LANE, LANE)
        kk = k * vec_ref[4:5, sl] / nrm
        a_out[:, sl] = -kk
        for d, (w_out, k_out, b_out) in enumerate(((w0_out, k0_out, b0_out), (w1_out, k1_out, b1_out))):
            dl = slice(d * W + g * LANE, d * W + (g + 1) * LANE)
            logw = -jax.nn.softplus(-(vec_ref[d:d + 1, sl] + wlo[:, dl])) - 0.5
            w_out[:, sl] = jnp.exp(-jnp.exp(logw))
            a = jax.nn.sigmoid(vec_ref[2 + d:3 + d, sl] + alo[:, dl])
            kd = k * (1 + (a - 1) * vec_ref[5:6, sl])
            k_out[:, sl] = kd
            b_out[:, sl] = kk * a
            term = r * kd * vec_ref[6:7, sl]
            bonus = term if bonus is None else bonus + term
    bonus_out[...] = _fold_lanes(bonus)


def rwkv_prep(p, mu, w2c, a2c, g2, vec):
    R = p.shape[0]
    W = MIX_HALF
    hb = PREP_TB // 8
    full = lambda a: pl.BlockSpec(a.shape, lambda i: (0,) * a.ndim)
    wide = pl.BlockSpec((PREP_TB, W), lambda i: (i, 0))
    return pl.pallas_call(
        _rwkv_prep_kernel,
        grid=(R // PREP_TB,),
        in_specs=[pl.BlockSpec((PREP_TB, AB_NA), lambda i: (i, 0)),
                  pl.BlockSpec((8, AB_NA), lambda i: (jnp.maximum(i * hb - 1, 0), 0)),
                  pl.BlockSpec((8, AB_NA), lambda i: (jnp.minimum((i + 1) * hb, R // 8 - 1), 0)),
                  full(mu), full(w2c), full(a2c), full(g2), full(vec)],
        out_specs=[wide] * 10 + [pl.BlockSpec((PREP_TB, LANE), lambda i: (i, 0))],
        out_shape=[jax.ShapeDtypeStruct((R, W), F32)] * 10 + [jax.ShapeDtypeStruct((R, LANE), F32)],
        compiler_params=_params(("parallel",)),
        name="rwkv_prep",
    )(p, p, p, mu, w2c, a2c, g2, vec)


def _wkv_kernel(a0_ref, b0_ref, k0_ref, w0_ref, r0_ref, v0_ref, a1_ref, b1_ref, k1_ref, w1_ref, r1_ref, v1_ref,
                y0_ref, y1_ref, s_ref):
    @pl.when(pl.program_id(0) == 0)
    def _():
        s_ref[...] = jnp.zeros_like(s_ref)

    dirs = ((a0_ref, b0_ref, k0_ref, w0_ref, r0_ref, v0_ref, y0_ref),
            (a1_ref, b1_ref, k1_ref, w1_ref, r1_ref, v1_ref, y1_ref))

    for t in range(WKV_TB):
        for d, (a_ref, b_ref, k_ref, w_ref, r_ref, v_ref, y_ref) in enumerate(dirs):
            tt = t if d == 0 else WKV_TB - 1 - t
            row = lambda ref, g: ref[tt:tt + 1, g * LANE:(g + 1) * LANE]
            for c0 in range(0, WKV_VC, WKV_CG):
                cs = range(c0, c0 + WKV_CG)
                acc = {c: None for c in cs}
                for g in range(WKV_KG):
                    ar = row(a_ref, g)
                    for c in cs:
                        term = s_ref[d, g, pl.ds(c * 8, 8), :] * ar
                        acc[c] = term if acc[c] is None else acc[c] + term
                sa = {c: _fold_lanes(acc[c]) for c in cs}
                vv = {c: v_ref[tt, pl.ds(c * 8, 8), :] for c in cs}
                acc = {c: None for c in cs}
                for g in range(WKV_KG):
                    wr, br, kr, rr = row(w_ref, g), row(b_ref, g), row(k_ref, g), row(r_ref, g)
                    for c in cs:
                        s_new = s_ref[d, g, pl.ds(c * 8, 8), :] * wr + sa[c] * br + vv[c] * kr
                        s_ref[d, g, pl.ds(c * 8, 8), :] = s_new
                        term = s_new * rr
                        acc[c] = term if acc[c] is None else acc[c] + term
                for c in cs:
                    y_ref[tt, pl.ds(c * 8, 8), :] = _fold_lanes(acc[c])


def wkv_scan(a, b0, b1, k0, k1, w0, w1, r, vp):
    R = a.shape[0]
    nb = R // WKV_TB
    cb = CTX_LEN // WKV_TB
    fwd = lambda i: i
    bwd = lambda i: jnp.where(i < cb, cb - 1 - i, nb + cb - 1 - i)
    kspec = lambda f: pl.BlockSpec((WKV_TB, MIX_HALF), lambda i: (f(i), 0))
    vspec = lambda f: pl.BlockSpec((WKV_TB, RWKV_HEAD, LANE), lambda i: (f(i), 0, 0))
    yshape = jax.ShapeDtypeStruct((R, RWKV_HEAD, LANE), F32)
    return pl.pallas_call(
        _wkv_kernel,
        grid=(nb,),
        in_specs=[kspec(fwd)] * 5 + [vspec(fwd)] + [kspec(bwd)] * 5 + [vspec(bwd)],
        out_specs=[vspec(fwd), vspec(bwd)],
        out_shape=[yshape, yshape],
        scratch_shapes=[pltpu.VMEM((2, WKV_KG, RWKV_HEAD, LANE), F32)],
        compiler_params=_params(("arbitrary",)),
        name="wkv_scan",
    )(a, b0, k0, w0, r, vp, a, b1, k1, w1, r, vp)


def _shift_prev(y):
    return jnp.pad(y, ((1, 0), (0, 0)))[:-1]


def _shift_next(y):
    return jnp.pad(y, ((0, 1), (0, 0)))[1:]


AB_WD = 3 * MIX_HALF
AB_AD = AB_WD + LANE
AB_GD = AB_AD + LANE
AB_NA = AB_GD + RWKV_GATE_LORA


def _pack_cols(w, cuts, widths):
    pieces = jnp.split(w, cuts, axis=-1)
    pad = [(0, 0)] * (w.ndim - 1)
    return jnp.concatenate([jnp.pad(p, pad + [(0, wd - p.shape[-1])]) for p, wd in zip(pieces, widths)],
                           axis=-1)


_AB_CUTS = (AB_WD, AB_WD + RWKV_DECAY_LORA, AB_WD + RWKV_DECAY_LORA + RWKV_A_LORA, RWKV_COLS)
_AB_WIDTHS = (AB_WD, LANE, LANE, RWKV_GATE_LORA, 3 * MIX_HALF)


def rwkv_mix(p, mu, w0, w2, a0, a2, g2, k_k, k_a, r_k, gn):
    R = p.shape[0]
    km3 = lambda t: jnp.concatenate([_kmajor(x) for x in jnp.split(t, 3, axis=-1)], axis=-1)
    mu_p = _pack_cols(mu, _AB_CUTS[:3], _AB_WIDTHS[:4])
    mu_p = jnp.concatenate([km3(mu_p[:, :AB_WD]), mu_p[:, AB_WD:]], axis=-1)
    lora = lambda w: jnp.pad(jnp.concatenate([_kmajor(w[0]), _kmajor(w[1])], axis=-1),
                             ((0, LANE - w.shape[1]), (0, 0))).astype(BF16)
    vec = jnp.stack([_kmajor(t) for t in (w0[0], w0[1], a0[0], a0[1], k_k, k_a, r_k, jnp.zeros_like(r_k))])
    a, b0, b1, k0, k1, d0, d1, r, v, g, bonus = rwkv_prep(p, mu_p, lora(w2), lora(a2),
                                                          _kmajor(g2).astype(BF16), vec)
    tile = lambda t: t.reshape(R, RWKV_HEAD, RWKV_HEADS)
    vp = jnp.broadcast_to(tile(v)[:, :, None, :], (R, RWKV_HEAD, 4, RWKV_HEADS)).reshape(R, RWKV_HEAD, LANE)
    y0, y1 = wkv_scan(a, b0, b1, k0, k1, d0, d1, r, vp)
    y = (y0 + y1)[:, :, :RWKV_HEADS]
    mean = jnp.mean(y, axis=1, keepdims=True)
    var = jnp.mean(jnp.square(y - mean), axis=1, keepdims=True)
    gn_t = _kmajor(gn).reshape(2, RWKV_HEAD, RWKV_HEADS)
    yn = (y - mean) * lax.rsqrt(var + RWKV_GN_EPS) * gn_t[0] + gn_t[1]
    out = (yn + bonus[:, None, :RWKV_HEADS] * tile(v)) * tile(g)
    return out.reshape(R, MIX_HALF)


NA_WIN = NA_KH * GRID_W


NA_RB = 4


def _na_kernel(q_ref, k_ref, v_ref, bias_ref, o_ref, *, rows):
    dn = (((1,), (1,)), ((), ()))
    k_ctx = k_ref[pl.ds(0, CTX_LEN), :].astype(BF16)
    v_ctx = v_ref[pl.ds(0, CTX_LEN), :].astype(BF16)
    for j in range(NA_RB):
        r = pl.program_id(1) * NA_RB + j
        r0 = jnp.clip(r - NA_KH // 2, 0, rows - NA_KH)
        start = pl.multiple_of(CTX_LEN + r0 * GRID_W, GRID_W)
        qs = slice(j * GRID_W, (j + 1) * GRID_W)
        q = (q_ref[qs, :] * NA_HEAD_DIM ** -0.5).astype(BF16)
        k_win = k_ref[pl.ds(start, NA_WIN), :].astype(BF16)
        s_loc = lax.dot_general(q, k_win, dn, preferred_element_type=F32) + bias_ref[0, r - r0]
        s_ctx = lax.dot_general(q, k_ctx, dn, preferred_element_type=F32)
        m = jnp.maximum(jnp.max(s_loc, axis=-1, keepdims=True), jnp.max(s_ctx, axis=-1, keepdims=True))
        p_loc = jnp.exp(s_loc - m)
        p_ctx = jnp.exp(s_ctx - m)
        den = jnp.sum(p_loc, axis=-1, keepdims=True) + jnp.sum(p_ctx, axis=-1, keepdims=True)
        num = jnp.dot(p_loc.astype(BF16), v_ref[pl.ds(start, NA_WIN), :].astype(BF16),
                      preferred_element_type=F32)
        num += jnp.dot(p_ctx.astype(BF16), v_ctx, preferred_element_type=F32)
        o_ref[qs, :] = num / den


def _na_ctx_kernel(q_ref, k_ref, v_ref, o_ref):
    q = (q_ref[...] * NA_HEAD_DIM ** -0.5).astype(BF16)
    s = lax.dot_general(q, k_ref[...].astype(BF16), (((1,), (1,)), ((), ())), preferred_element_type=F32)
    p = jnp.exp(s - jnp.max(s, axis=-1, keepdims=True))
    num = jnp.dot(p.astype(BF16), v_ref[...].astype(BF16), preferred_element_type=F32)
    o_ref[...] = num / jnp.sum(p, axis=-1, keepdims=True)


def _na_bias(rpb, rows):
    kh = min(NA_KH, rows)
    col = np.arange(GRID_W)
    c0 = np.clip(col - NA_KW // 2, 0, GRID_W - NA_KW)
    key = np.arange(GRID_W)
    inside = (key[None, :] >= c0[:, None]) & (key[None, :] < c0[:, None] + NA_KW)
    dc = np.clip(key[None, :] - col[:, None] + (NA_KW - 1), 0, 2 * NA_KW - 2)
    onehot = np.zeros((GRID_W, GRID_W, 2 * NA_KW - 1), np.float32)
    onehot[col[:, None], key[None, :], dc] = 1.0
    toep = jnp.einsum("qcd,hrd->hrqc", onehot, rpb, precision=lax.Precision.HIGHEST)
    toep = jnp.where(jnp.asarray(inside), toep, NEG)
    return jnp.stack([jnp.concatenate([toep[:, i - j + NA_KH - 1] for i in range(kh)], axis=-1)
                      for j in range(NA_KH)], axis=1)


def na_mix(p, rpb):
    R = p.shape[0]
    T = R - CTX_LEN
    rows = T // GRID_W
    qb, kb, vb = [(AB_NA + i * MIX_HALF) // NA_HEAD_DIM for i in range(3)]
    qrows = NA_RB * GRID_W
    cb = CTX_LEN // qrows
    bias = _na_bias(rpb, rows)
    strip = lambda b: pl.BlockSpec((R, NA_HEAD_DIM), lambda h, r: (0, b + h))
    lat = pl.pallas_call(
        functools.partial(_na_kernel, rows=rows),
        grid=(NA_HEADS, rows // NA_RB),
        in_specs=[pl.BlockSpec((qrows, NA_HEAD_DIM), lambda h, r: (cb + r, qb + h)),
                  strip(kb), strip(vb),
                  pl.BlockSpec((1, NA_KH, GRID_W, NA_WIN), lambda h, r: (h, 0, 0, 0))],
        out_specs=pl.BlockSpec((qrows, NA_HEAD_DIM), lambda h, r: (r, h)),
        out_shape=jax.ShapeDtypeStruct((T, MIX_HALF), F32),
        compiler_params=_params(("parallel", "arbitrary")),
        name="na_latent",
    )(p, p, p, bias)
    blk = lambda b: pl.BlockSpec((CTX_LEN, NA_HEAD_DIM), lambda h: (0, b + h))
    ctx = pl.pallas_call(
        _na_ctx_kernel,
        grid=(NA_HEADS,),
        in_specs=[blk(qb), blk(kb), blk(vb)],
        out_specs=pl.BlockSpec((CTX_LEN, NA_HEAD_DIM), lambda h: (0, h)),
        out_shape=jax.ShapeDtypeStruct((CTX_LEN, MIX_HALF), F32),
        compiler_params=_params(("parallel",)),
        name="na_context",
    )(p, p, p)
    return jnp.concatenate([ctx, lat], axis=0)


ML_CHUNK = CTX_LEN
CD_GT = 2 * ML_HEADS * ML_DQK + 2 * MIX_HALF
CD_HY = CD_GT + LANE


def _mlstm_kernel(q_ref, k_ref, v_ref, lir_ref, lfr_ref, lic_ref, lfc_ref, o_ref, c_ref, n_ref, m_ref):
    L = ML_CHUNK

    @pl.when(pl.program_id(2) == 0)
    def _():
        c_ref[...] = jnp.zeros_like(c_ref)
        n_ref[...] = jnp.zeros_like(n_ref)
        m_ref[...] = jnp.zeros_like(m_ref)

    sgn = 1 - 2 * pl.program_id(0)
    row = lax.broadcasted_iota(jnp.int32, (L, L), 0)
    col = lax.broadcasted_iota(jnp.int32, (L, L), 1)
    seen = (row - col) * sgn >= 0
    seen_t = (col - row) * sgn >= 0
    q = q_ref[...]
    k = k_ref[...]
    vb = v_ref[...].astype(BF16)
    li_r, lf_r, li_c, lf_c = lir_ref[0], lfr_ref[0], lic_ref[0], lfc_ref[0]
    b_c = jnp.sum(jnp.where(seen, lf_r, 0.0), axis=1, keepdims=True)
    b_r = jnp.sum(jnp.where(seen_t, lf_c, 0.0), axis=0, keepdims=True)
    m_prev = m_ref[...]
    dmat = jnp.where(seen, b_c - b_r + li_r, NEG)
    inter = b_c + m_prev
    m_t = jnp.maximum(jnp.max(dmat, axis=1, keepdims=True), inter)
    qb = q.astype(BF16)
    s = lax.dot_general(qb, k.astype(BF16), (((1,), (1,)), ((), ())), preferred_element_type=F32)
    s = s * jnp.exp(dmat - m_t)
    dec = jnp.exp(inter - m_t)
    num = jnp.dot(s.astype(BF16), vb, preferred_element_type=F32)
    num += dec * jnp.dot(qb, c_ref[...].astype(BF16), preferred_element_type=F32)
    den = jnp.sum(s, axis=1, keepdims=True) + dec * jnp.sum(q * n_ref[...], axis=1, keepdims=True)
    o_ref[0] = num / jnp.maximum(jnp.abs(den), jnp.exp(-m_t))
    total = jnp.sum(lf_r, axis=1, keepdims=True)
    gl = total - b_c + li_c
    m_new = jnp.maximum(total + m_prev, jnp.max(gl, axis=0, keepdims=True))
    kw = k * jnp.exp(gl - m_new)
    sc = jnp.exp(total + m_prev - m_new)
    c_ref[...] = sc * c_ref[...] + lax.dot_general(kw.astype(BF16), vb, (((0,), (0,)), ((), ())),
                                                   preferred_element_type=F32)
    n_ref[...] = sc * n_ref[...] + jnp.sum(kw, axis=0, keepdims=True)
    m_ref[...] = m_new


def _axial_rope(x):
    T, d = x.shape[0], x.shape[-1]
    half = d // 2
    nf = half // 2
    t = jnp.arange(T)
    row = (t // GRID_W).astype(F32)
    col = (t % GRID_W).astype(F32)
    inv = ROPE_BASE ** (-jnp.arange(nf, dtype=F32) / nf)
    ang = jnp.concatenate([row[:, None] * inv, col[:, None] * inv], axis=-1)[:, None, :]
    cos, sin = jnp.cos(ang), jnp.sin(ang)
    x1, x2 = x[..., :half], x[..., half:]
    return jnp.concatenate([x1 * cos - x2 * sin, x2 * cos + x1 * sin], axis=-1)


def mlstm_mix(p, gate_b, norm_w):
    R = p.shape[0]
    L = ML_CHUNK
    nc = R // L
    QK = ML_HEADS * ML_DQK

    def rope_lat(t):
        th = t.reshape(R, ML_HEADS, ML_DQK)
        return jnp.concatenate([th[:CTX_LEN], _axial_rope(th[CTX_LEN:])], axis=0).reshape(R, QK)

    q = rope_lat(p[:, :QK])
    k = rope_lat(p[:, QK:2 * QK]) * ML_DQK ** -0.5
    o = p[:, 2 * QK + MIX_HALF:CD_GT]
    gt = p[:, CD_GT:CD_GT + 4 * ML_HEADS].reshape(R, 2, 2, ML_HEADS) + gate_b
    log_i = jnp.transpose(gt[:, :, 0], (1, 2, 0))
    log_f = jax.nn.log_sigmoid(jnp.transpose(gt[:, :, 1], (1, 2, 0)))
    as_rows = lambda t: t.reshape(2 * ML_HEADS * nc, 1, L)
    as_cols = lambda t: t.reshape(2 * ML_HEADS * nc, L, 1)

    def chunk(d, c):
        return jnp.where(d == 0, c, jnp.where(c == 0, 0, nc - c))

    gidx = lambda d, h, c: ((d * ML_HEADS + h) * nc + chunk(d, c), 0, 0)
    vb = 2 * QK // ML_DV
    h_dir = pl.pallas_call(
        _mlstm_kernel,
        grid=(2, ML_HEADS, nc),
        in_specs=[pl.BlockSpec((L, ML_DQK), lambda d, h, c: (chunk(d, c), h)),
                  pl.BlockSpec((L, ML_DQK), lambda d, h, c: (chunk(d, c), h)),
                  pl.BlockSpec((L, ML_DV), lambda d, h, c: (chunk(d, c), vb + h)),
                  pl.BlockSpec((1, 1, L), gidx), pl.BlockSpec((1, 1, L), gidx),
                  pl.BlockSpec((1, L, 1), gidx), pl.BlockSpec((1, L, 1), gidx)],
        out_specs=pl.BlockSpec((1, L, ML_DV), lambda d, h, c: (d, chunk(d, c), h)),
        out_shape=jax.ShapeDtypeStruct((2, R, MIX_HALF), F32),
        scratch_shapes=[pltpu.VMEM((ML_DQK, ML_DV), F32), pltpu.VMEM((1, ML_DQK), F32),
                        pltpu.VMEM((1, 1), F32)],
        compiler_params=_params(("parallel", "parallel", "arbitrary")),
        name="mlstm",
    )(q, k, p, as_rows(log_i), as_rows(log_f), as_cols(log_i), as_cols(log_f))
    h = (h_dir[0] + h_dir[1]).reshape(R, ML_HEADS, ML_DV)
    h = h * lax.rsqrt(jnp.mean(h * h, axis=-1, keepdims=True) + RMS_EPS)
    return h.reshape(R, MIX_HALF) * norm_w * jax.nn.sigmoid(o)


def _pad_to(x, axis, mult):
    n = -x.shape[axis] % mult
    if n == 0:
        return x
    pad = [(0, 0)] * x.ndim
    pad[axis] = (0, n)
    return jnp.pad(x, pad)


def mm3p(a, b):
    M, N = a.shape[0], b.shape[1]
    a = _pad_to(_pad_to(a, 1, LANE), 0, 8)
    b = _pad_to(_pad_to(b, 0, LANE), 1, LANE)
    return mm3(a, b)[:M, :N]


def hyena_filters(T, ff1, ff1_b, ff2, ff2_b, ff3, ff3_b, ff_out, freq, decay):
    t = jnp.linspace(0.0, 1.0, T, dtype=F32)[:, None]
    bands = (HY_EMB - 1) // 2
    f = jnp.linspace(1e-4, bands - 1, bands, dtype=F32)[None, :]
    w = (2 * math.pi / T) * jnp.arange(T, dtype=F32)[:, None]
    z = jnp.concatenate([t, jnp.cos(f * w), -jnp.sin(f * w)], axis=-1)
    h = jnp.sin(freq * (mm3p(z, ff1) + ff1_b))
    h = jnp.sin(freq * (mm3p(h, ff2) + ff2_b))
    h = jnp.sin(freq * (mm3p(h, ff3) + ff3_b))
    h = mm3p(h, ff_out) * jnp.exp(-t * jnp.abs(decay))
    return h.reshape(T, HY_ORDER, 2, MIX_HALF)


FFT_N2 = LANE


@functools.lru_cache(maxsize=None)
def _dft_tables(n1):
    n2 = FFT_N2
    n = n1 * n2
    i1, i2 = np.arange(n1), np.arange(n2)
    a1 = 2 * np.pi * np.outer(i1, i1) / n1
    a2 = 2 * np.pi * np.outer(i2, i2) / n2
    c1, s1 = np.cos(a1), -np.sin(a1)
    c2, s2 = np.cos(a2), -np.sin(a2)
    atw = 2 * np.pi * np.outer(i2, i1) / n
    f32 = lambda x: np.asarray(x, np.float32)
    return dict(
        wa=f32(np.concatenate([c1, s1], axis=1)),
        wb=f32(np.block([[c2, s2], [-s2, c2]])),
        wb_inv=f32(np.block([[c2, -s2], [s2, c2]])),
        wa_inv=f32(np.concatenate([c1, s1], axis=0)[:, :n1 // 2] / n),
        twr=f32(np.cos(atw)), twi=f32(-np.sin(atw)))


def _fft_fwd(x_t, n1):
    C = x_t.shape[0]
    n2 = FFT_N2
    tb = _dft_tables(n1)
    xa = jnp.transpose(x_t.reshape(C, n1, n2), (0, 2, 1)).reshape(C * n2, n1)
    a = mm3p(xa, tb["wa"]).reshape(C, n2, 2, n1)
    ar, ai = a[:, :, 0], a[:, :, 1]
    br = ar * tb["twr"] - ai * tb["twi"]
    bi = ar * tb["twi"] + ai * tb["twr"]
    b = jnp.concatenate([jnp.transpose(br, (0, 2, 1)), jnp.transpose(bi, (0, 2, 1))], axis=-1)
    return mm3(b.reshape(C * n1, 2 * n2), tb["wb"])


def _fft_inv(y, C, n1):
    n2 = FFT_N2
    tb = _dft_tables(n1)
    g = mm3(y, tb["wb_inv"]).reshape(C, n1, 2, n2)
    gr, gi = g[:, :, 0], g[:, :, 1]
    twr, twi = tb["twr"].T, tb["twi"].T
    hr = gr * twr + gi * twi
    hi = gi * twr - gr * twi
    h = jnp.concatenate([jnp.transpose(hr, (0, 2, 1)), jnp.transpose(hi, (0, 2, 1))], axis=-1)
    out = mm3p(h.reshape(C * n2, 2 * n1), tb["wa_inv"])
    return jnp.transpose(out.reshape(C, n2, n1 // 2), (0, 2, 1)).reshape(C, n1 // 2 * n2)


def _long_conv(z, h_fwd, h_bwd):
    T, C = z.shape
    n1 = 2 * T // FFT_N2
    f = jnp.concatenate([h_fwd, jnp.zeros((1, C), F32), h_bwd[:0:-1]], axis=0)
    zs = _fft_fwd(jnp.pad(z.T, ((0, 0), (0, T))), n1).reshape(C * n1, 2, FFT_N2)
    fs = _fft_fwd(f.T, n1).reshape(C * n1, 2, FFT_N2)
    yr = zs[:, 0] * fs[:, 0] - zs[:, 1] * fs[:, 1]
    yi = zs[:, 0] * fs[:, 1] + zs[:, 1] * fs[:, 0]
    return _fft_inv(jnp.concatenate([yr, yi], axis=-1), C, n1).T


HY_Q = 4
HY_F1Q = FFT_N2 // HY_Q
HY_F2H = FFT_N2 // 2
HY_QR = HY_F1Q * HY_F2H
HY_NN = FFT_N2 * FFT_N2


@functools.lru_cache(maxsize=None)
def _hy_tables():
    n = FFT_N2
    i = np.arange(n)
    w = np.exp(-2j * np.pi * np.outer(i, i) / n)
    tw = np.exp(-2j * np.pi * np.outer(i, i) / (n * n))
    ca = np.conj(w)[: n // 2] / (n * n)
    f32 = lambda x: np.ascontiguousarray(x, dtype=np.float32)
    return dict(
        wa=f32(np.concatenate([w.real, w.imag], axis=0)),
        wbr=f32(w.real), wbi=f32(w.imag),
        wbc=f32(2 * np.block([[w.real[:, : n // 2], w.imag[:, : n // 2]],
                              [-w.imag[:, : n // 2], w.real[:, : n // 2]]])),
        twr=f32(tw.real).reshape(n, 1, n), twi=f32(tw.imag).reshape(n, 1, n),
        ctr=f32(tw.real.T).reshape(n, 1, n), cti=f32(-tw.imag.T).reshape(n, 1, n),
        car=f32(ca.real), cai=f32(ca.imag))


def _dot3(a_hi, a_lo, b):
    b_hi, b_lo = _split(b)
    acc = jnp.dot(a_hi, b_hi, preferred_element_type=F32)
    acc += jnp.dot(a_hi, b_lo, preferred_element_type=F32)
    return acc + jnp.dot(a_lo, b_hi, preferred_element_type=F32)


def _hy_fwd_kernel(x_ref, wa_ref, wbr_ref, wbi_ref, twr_ref, twi_ref, o_ref, e_ref, a_ref, *, k1):
    n = FFT_N2
    h = HY_F2H
    q = pl.program_id(1)

    @pl.when(q == 0)
    def _():
        wa_hi, wa_lo = _split(wa_ref[...])
        for t2 in range(n):
            y = _dot3(wa_hi, wa_lo, x_ref[pl.ds(t2, k1, stride=n), :])
            a_ref[pl.ds(t2, n, stride=n), :] = y[:n]
            a_ref[pl.ds(HY_NN + t2, n, stride=n), :] = y[n:]
        col = a_ref[pl.ds(0, n), :]
        sign = (1 - 2 * (lax.broadcasted_iota(jnp.int32, (n, 1), 0) % 2)).astype(F32)
        dc = jnp.sum(col, axis=0, keepdims=True)
        nyq = jnp.sum(col * sign, axis=0, keepdims=True)
        r = lax.broadcasted_iota(jnp.int32, e_ref.shape, 0)
        e_ref[...] = jnp.where(r == 0, dc, jnp.where(r == 1, nyq, 0.0))

    cr, ci = wbr_ref[pl.ds(0, h), :], wbi_ref[pl.ds(0, h), :]

    def body(j, carry):
        f1 = q * HY_F1Q + j
        base = pl.multiple_of(f1 * n, n)
        rhs = jnp.concatenate([a_ref[pl.ds(base, n), :], a_ref[pl.ds(HY_NN + base, n), :]], axis=0)
        twr, twi = twr_ref[f1], twi_ref[f1]
        pr = cr * twr - ci * twi
        pi = cr * twi + ci * twr
        lhs = jnp.concatenate([jnp.concatenate([pr, -pi], axis=1), jnp.concatenate([pi, pr], axis=1)], axis=0)
        out = _dot3(*_split(lhs), rhs)
        ob = pl.multiple_of(j * h, h)
        o_ref[pl.ds(ob, h), :] = out[:h]
        o_ref[pl.ds(HY_QR + ob, h), :] = out[h:]
        return carry

    lax.fori_loop(0, HY_F1Q, body, 0, unroll=2)


def hy_fwd(x):
    T_in, C = x.shape
    k1 = T_in // FFT_N2
    tb = _hy_tables()
    const = lambda a: pl.BlockSpec(a.shape, lambda c, q: (0,) * a.ndim)
    wa = tb["wa"][:, :k1]
    args = (wa, tb["wbr"], tb["wbi"], tb["twr"], tb["twi"])
    return pl.pallas_call(
        functools.partial(_hy_fwd_kernel, k1=k1),
        grid=(C // LANE, HY_Q),
        in_specs=[pl.BlockSpec((T_in, LANE), lambda c, q: (0, c))] + [const(a) for a in args],
        out_specs=[pl.BlockSpec((None, 2 * HY_QR, LANE), lambda c, q: (q, 0, c)),
                   pl.BlockSpec((8, LANE), lambda c, q: (0, c))],
        out_shape=[jax.ShapeDtypeStruct((HY_Q, 2 * HY_QR, C), F32), jax.ShapeDtypeStruct((8, C), F32)],
        scratch_shapes=[pltpu.VMEM((2 * HY_NN, LANE), F32)],
        compiler_params=_params(("parallel", "arbitrary")),
        name="hy_fwd",
    )(x, *args)


def _hy_conv_kernel(z_ref, ze_ref, f_ref, fe_ref, wbc_ref, ctr_ref, cti_ref, car_ref, cai_ref, y_ref, g_ref):
    n = FFT_N2
    h = HY_F2H
    q = pl.program_id(1)
    l_hi, l_lo = _split(wbc_ref[...])

    def body(j, carry):
        ob = pl.multiple_of(j * h, h)
        zr, zi = z_ref[pl.ds(ob, h), :], z_ref[pl.ds(HY_QR + ob, h), :]
        fr, fi = f_ref[pl.ds(ob, h), :], f_ref[pl.ds(HY_QR + ob, h), :]
        prod = jnp.concatenate([zr * fr - zi * fi, zr * fi + zi * fr], axis=0)
        out = _dot3(l_hi, l_lo, prod)
        base = pl.multiple_of((q * HY_F1Q + j) * n, n)
        g_ref[pl.ds(base, n), :] = out[:n]
        g_ref[pl.ds(HY_NN + base, n), :] = out[n:]
        return carry

    lax.fori_loop(0, HY_F1Q, body, 0, unroll=2)

    @pl.when(q == HY_Q - 1)
    def _():
        car, cai = car_ref[...], cai_ref[...]
        y_dc = ze_ref[0:1, :] * fe_ref[0:1, :] * (-1.0 / HY_NN)
        y_nyq = ze_ref[1:2, :] * fe_ref[1:2, :] * (1.0 / HY_NN)
        fix = (y_dc + y_nyq, y_dc - y_nyq)
        for t2 in range(n):
            ctr, cti = ctr_ref[t2], cti_ref[t2]
            er = car * ctr - cai * cti
            ei = car * cti + cai * ctr
            rhs = jnp.concatenate([g_ref[pl.ds(t2, n, stride=n), :],
                                   g_ref[pl.ds(HY_NN + t2, n, stride=n), :]], axis=0)
            y_ref[pl.ds(t2, n // 2, stride=n), :] = (
                _dot3(*_split(jnp.concatenate([er, -ei], axis=1)), rhs) + fix[t2 % 2])


def hy_conv(z, f, f_block=0):
    (zs, ze), (fs, fe) = z, f
    C = zs.shape[-1]
    T = HY_NN // 2
    tb = _hy_tables()
    const = lambda a: pl.BlockSpec(a.shape, lambda c, q: (0,) * a.ndim)
    spec = pl.BlockSpec((None, 2 * HY_QR, LANE), lambda c, q: (q, 0, c))
    fspec = pl.BlockSpec((None, 2 * HY_QR, LANE), lambda c, q: (q, 0, c + f_block))
    espec = pl.BlockSpec((8, LANE), lambda c, q: (0, c))
    fespec = pl.BlockSpec((8, LANE), lambda c, q: (0, c + f_block))
    args = (tb["wbc"], tb["ctr"], tb["cti"], tb["car"], tb["cai"])
    return pl.pallas_call(
        _hy_conv_kernel,
        grid=(C // LANE, HY_Q),
        in_specs=[spec, espec, fspec, fespec] + [const(a) for a in args],
        out_specs=pl.BlockSpec((T, LANE), lambda c, q: (0, c)),
        out_shape=jax.ShapeDtypeStruct((T, C), F32),
        scratch_shapes=[pltpu.VMEM((2 * HY_NN, LANE), F32)],
        compiler_params=_params(("parallel", "arbitrary")),
        name="hy_conv",
    )(zs, ze, fs, fe, *args)


def hyena_filter_taps(T, ff1, ff1_b, ff2, ff2_b, ff3, ff3_b, ff_out, freq, decay):
    t = jnp.linspace(0.0, 1.0, T, dtype=F32)[:, None]
    bands = (HY_EMB - 1) // 2
    f = jnp.linspace(1e-4, bands - 1, bands, dtype=F32)[None, :]
    w = (2 * math.pi / T) * jnp.arange(T, dtype=F32)[:, None]
    z = jnp.concatenate([t, jnp.cos(f * w), -jnp.sin(f * w)], axis=-1)
    h = jnp.sin(freq * (mm3p(z, ff1) + ff1_b))
    h = jnp.sin(freq * (mm3p(h, ff2) + ff2_b))
    h = jnp.sin(freq * (mm3p(h, ff3) + ff3_b))
    w_out = ff_out.reshape(-1, HY_ORDER, 2, MIX_HALF)
    dec = jnp.abs(decay).reshape(HY_ORDER, 2, MIX_HALF)
    side = lambda s: (w_out[:, :, s].reshape(-1, HY_ORDER * MIX_HALF), dec[:, s].reshape(HY_ORDER * MIX_HALF))
    (w_f, d_f), (w_b, d_b) = side(0), side(1)
    zero = jnp.zeros((1, h.shape[1]), F32)
    h_b = jnp.concatenate([zero, h[:0:-1]], axis=0)
    lhs = jnp.concatenate([jnp.concatenate([h, jnp.zeros_like(h)], axis=1),
                           jnp.concatenate([jnp.zeros_like(h), h_b], axis=1)], axis=0)
    t_b = jnp.concatenate([jnp.zeros((1, 1), F32), t[:0:-1]], axis=0)
    first = jnp.arange(2 * T)[:, None] < T
    scale = jnp.exp(-jnp.concatenate([t, t_b], axis=0) * jnp.where(first, d_f, d_b))
    return mm3p(lhs, jnp.concatenate([w_f, w_b], axis=0)) * scale


@functools.lru_cache(maxsize=None)
def _dft_small(n):
    i = np.arange(n)
    w = np.exp(-2j * np.pi * np.outer(i, i) / n)
    fwd = np.concatenate([w.real, w.imag], axis=0)
    inv = np.concatenate([w.real, w.imag], axis=1)[: n // 2] / n
    return np.asarray(fwd, np.float32), np.asarray(inv, np.float32)


def _short_conv(z, h_fwd, h_bwd):
    T, C = z.shape
    n = 2 * T
    fwd, inv = _dft_small(n)
    f = jnp.concatenate([h_fwd, jnp.zeros((1, C), F32), h_bwd[:0:-1]], axis=0)
    zs = mm3(fwd[:, :T], z)
    fs = mm3(fwd, f)
    prod = jnp.concatenate([zs[:n] * fs[:n] - zs[n:] * fs[n:], zs[:n] * fs[n:] + zs[n:] * fs[:n]], axis=0)
    return mm3(inv, prod)


def hyena_seq(p, short_w, short_b, conv, bias):
    u = short_w[0] * _shift_prev(p) + short_w[1] * p + short_w[2] * _shift_next(p) + short_b
    z, x1, x2 = jnp.split(u, 3, axis=-1)
    for n, gate in enumerate((x1, x2)):
        z = gate * (conv(z, n) + bias[n] * z)
    return z


def hyena_mix(p, short_w, short_b, filt_params, bias):
    T = p.shape[0] - CTX_LEN
    filt_c = hyena_filters(CTX_LEN, *filt_params)
    out_c = hyena_seq(p[:CTX_LEN], short_w, short_b,
                      lambda z, n: _short_conv(z, filt_c[:, n, 0], filt_c[:, n, 1]), bias)
    if 2 * T == HY_NN:
        spectra = hy_fwd(hyena_filter_taps(T, *filt_params))
        conv = lambda z, n: hy_conv(hy_fwd(z), spectra, n * MIX_HALF // LANE)
    else:
        filt = hyena_filters(T, *filt_params)
        conv = lambda z, n: _long_conv(z, filt[:, n, 0], filt[:, n, 1])
    out = hyena_seq(p[CTX_LEN:], short_w, short_b, conv, bias)
    return jnp.concatenate([out_c, out], axis=0)


def kernel(x, c, ctx, c_ctx, ada_down, ada_up, ada_bias, norm_g, ffn_w1, ffn_w3, ffn_w2, ab_w_in, ab_w_out, rwkv_mu, rwkv_w0, rwkv_w2, rwkv_a0, rwkv_a2, rwkv_g2, rwkv_kk, rwkv_ka, rwkv_rk, rwkv_gn, na_rpb, cd_w_in, cd_w_out, ml_gate_b, ml_norm, hy_short_w, hy_short_b, hy_ff1, hy_ff1_b, hy_ff2, hy_ff2_b, hy_ff3, hy_ff3_b, hy_ff_out, hy_freq, hy_decay, hy_bias):
    assert x.shape[0] == 1 and ctx.shape[1] == CTX_LEN
    X = jnp.concatenate([ctx[0], x[0]], axis=0)
    s2 = _pad_to(jnp.stack([jax.nn.silu(c_ctx), jax.nn.silu(c[0])]), 0, 16)
    n = jnp.arange(MIX_HALF)
    perm = ((n % RWKV_HEADS) * RWKV_HEAD + n // RWKV_HEADS == n[:, None]).astype(BF16)
    mods = [(mm(mm(s2, ada_down[l]), ada_up[l])[:2] + ada_bias[l]).reshape(2, N_SUB, 3, D_MODEL)
            for l in range(DEPTH)]
    hn = norm_mod(X, norm_g[0, 0], mods[0][:, 0, 0], mods[0][:, 0, 1])
    for l in range(DEPTH):
        i = l // 2
        mod = mods[l]
        g = norm_g[l]
        y = mm_w32(swiglu(hn, ffn_w1, ffn_w3, (l, 0)), ffn_w2, (l, 0))
        X, xn = post_res_norm(X, y, g[1], mod[:, 0, 2], FFN_RES, g[2], mod[:, 1, 0], mod[:, 1, 1])
        if l % 2 == 0:
            w_ab = ab_w_in[i].astype(BF16)
            w_rkv = [mm(t, perm, BF16) for t in jnp.split(w_ab[:, :AB_WD], 3, axis=-1)]
            p = mm(xn, _pack_cols(jnp.concatenate(w_rkv + [w_ab[:, AB_WD:]], axis=-1), _AB_CUTS, _AB_WIDTHS))
            ya = rwkv_mix(p, rwkv_mu[i], rwkv_w0[i], rwkv_w2[i], rwkv_a0[i], rwkv_a2[i],
                          rwkv_g2[i], rwkv_kk[i], rwkv_ka[i], rwkv_rk[i], rwkv_gn[i])
            y = jnp.concatenate([ya, na_mix(p, na_rpb[i])], axis=-1)
            w_out = ab_w_out[i].astype(BF16)
            w_rw = mm(perm.T, w_out[:MIX_HALF], BF16)
            yo = mm(y, jnp.concatenate([w_rw, w_out[MIX_HALF:]], axis=0))
        else:
            w_in = _pack_cols(cd_w_in[i].astype(BF16), (CD_GT, ML_COLS), (CD_GT, LANE, P_COLS - CD_HY))
            p = mm(xn, w_in)
            filt_params = (hy_ff1[i], hy_ff1_b[i], hy_ff2[i], hy_ff2_b[i], hy_ff3[i], hy_ff3_b[i],
                           hy_ff_out[i], hy_freq[i], hy_decay[i])
            yd = hyena_mix(p[:, CD_HY:CD_HY + 3 * MIX_HALF], hy_short_w[i], hy_short_b[i], filt_params,
                           hy_bias[i])
            y = jnp.concatenate([mlstm_mix(p, ml_gate_b[i], ml_norm[i]), yd], axis=-1)
            yo = mm_w32(y.astype(BF16), cd_w_out, (i,))
        X, hn = post_res_norm(X, yo, g[3], mod[:, 1, 2], 1.0, g[4], mod[:, 2, 0], mod[:, 2, 1])
        y = mm_w32(swiglu(hn, ffn_w1, ffn_w3, (l, 1)), ffn_w2, (l, 1))
        if l + 1 < DEPTH:
            nxt = mods[l + 1]
            X, hn = post_res_norm(X, y, g[5], mod[:, 2, 2], FFN_RES, norm_g[l + 1, 0], nxt[:, 0, 0], nxt[:, 0, 1])
        else:
            X = post_res(X, y, g[5], mod[:, 2, 2], FFN_RES)
    return X[CTX_LEN:][None]
```

```python
import functools
import math

import jax
import jax.numpy as jnp
import numpy as np
from jax import lax
from jax.experimental import pallas as pl
from jax.experimental.pallas import tpu as pltpu

F32 = jnp.float32
BF16 = jnp.bfloat16

D_MODEL = 4096
DEPTH = 4
GRID_W = 64
CTX_LEN = 256
N_SUB = 3
FFN_RES = 0.5
D_FF = 5632
RMS_EPS = 1e-6
MIX_HALF = D_MODEL // 2

RWKV_HEAD = 64
RWKV_HEADS = MIX_HALF // RWKV_HEAD
RWKV_DECAY_LORA = 96
RWKV_A_LORA = 96
RWKV_GATE_LORA = 256
RWKV_COLS = 3 * MIX_HALF + RWKV_DECAY_LORA + RWKV_A_LORA + RWKV_GATE_LORA
RWKV_GN_EPS = 64e-5

NA_HEAD_DIM = 128
NA_HEADS = MIX_HALF // NA_HEAD_DIM
NA_KH = 8
NA_KW = 16

ML_HEADS = 4
ML_DV = MIX_HALF // ML_HEADS
ML_DQK = ML_DV // 2
ML_COLS = 2 * ML_HEADS * ML_DQK + 2 * MIX_HALF + 4 * ML_HEADS
ROPE_BASE = 10000.0

HY_ORDER = 2
HY_EMB = 33

LANE = 128
ROW_TILE = CTX_LEN
VMEM_LIMIT = 56 * 1024 * 1024
P_COLS = 12800
NEG = -1e30


def _params(sem):
    return pltpu.CompilerParams(dimension_semantics=sem, vmem_limit_bytes=VMEM_LIMIT)


def _pick(n, cands):
    for c in cands:
        if n % c == 0:
            return c
    return n


def _mm_kernel(a_ref, b_ref, o_ref):
    o_ref[...] = jnp.dot(a_ref[...], b_ref[...], preferred_element_type=F32).astype(o_ref.dtype)


def _split(x):
    hi = x.astype(BF16)
    lo = (x - hi.astype(F32)).astype(BF16)
    return hi, lo


def _mm3_kernel(a_ref, b_ref, o_ref):
    a_hi, a_lo = _split(a_ref[...])
    b_hi, b_lo = _split(b_ref[...])
    acc = jnp.dot(a_hi, b_hi, preferred_element_type=F32)
    acc += jnp.dot(a_hi, b_lo, preferred_element_type=F32)
    acc += jnp.dot(a_lo, b_hi, preferred_element_type=F32)
    o_ref[...] = acc


def _mm_call(kern, a, b, out_dtype, name):
    M, K = a.shape
    N = b.shape[1]
    tm = _pick(M, (1024, 768, 512, 256, 128, 64, 32, 16, 8))
    tn = _pick(N, (512, 256, 128))
    return pl.pallas_call(
        kern,
        grid=(M // tm, N // tn),
        in_specs=[pl.BlockSpec((tm, K), lambda i, j: (i, 0)),
                  pl.BlockSpec((K, tn), lambda i, j: (0, j))],
        out_specs=pl.BlockSpec((tm, tn), lambda i, j: (i, j)),
        out_shape=jax.ShapeDtypeStruct((M, N), out_dtype),
        compiler_params=_params(("parallel", "parallel")),
        name=name,
    )(a, b)


def mm(a, b, out_dtype=F32):
    return _mm_call(_mm_kernel, a.astype(BF16), b.astype(BF16), out_dtype, "mm")


def mm3(a, b):
    return _mm_call(_mm3_kernel, a.astype(F32), b.astype(F32), F32, "mm3")


def _mm_w32_kernel(a_ref, w_ref, o_ref, wb_ref):
    @pl.when(pl.program_id(1) == 0)
    def _():
        wb_ref[...] = w_ref[...].astype(BF16)

    o_ref[...] = jnp.dot(a_ref[...], wb_ref[...], preferred_element_type=F32).astype(o_ref.dtype)


def _swiglu_kernel(a_ref, w1_ref, w3_ref, o_ref, w1b_ref, w3b_ref):
    @pl.when(pl.program_id(1) == 0)
    def _():
        w1b_ref[...] = w1_ref[...].astype(BF16)
        w3b_ref[...] = w3_ref[...].astype(BF16)

    a = a_ref[...]
    h1 = jnp.dot(a, w1b_ref[...], preferred_element_type=F32)
    h3 = jnp.dot(a, w3b_ref[...], preferred_element_type=F32)
    o_ref[...] = (h1 * jax.nn.sigmoid(h1) * h3).astype(o_ref.dtype)


def _w32_call(kern, a, ws, tms, tn, out_dtype, name, lead=()):
    M, K = a.shape
    N = ws[0].shape[-1]
    tm = _pick(M, tms)
    wspec = pl.BlockSpec((None,) * len(lead) + (K, tn), lambda j, i: tuple(lead) + (0, j))
    return pl.pallas_call(
        kern,
        grid=(N // tn, M // tm),
        in_specs=[pl.BlockSpec((tm, K), lambda j, i: (i, 0))] + [wspec] * len(ws),
        out_specs=pl.BlockSpec((tm, tn), lambda j, i: (i, j)),
        out_shape=jax.ShapeDtypeStruct((M, N), out_dtype),
        scratch_shapes=[pltpu.VMEM((K, tn), BF16)] * len(ws),
        compiler_params=_params(("parallel", "arbitrary")),
        name=name,
    )(a, *ws)


def swiglu(a, w1, w3, lead=()):
    return _w32_call(_swiglu_kernel, a, (w1, w3), (1408, 768, 512, 256), 256, BF16, "swiglu", lead)


def mm_w32(a, w, lead=()):
    return _w32_call(_mm_w32_kernel, a, (w,), (768, 512, 256), 512, F32, "mm_w32", lead)


def _group_spec(d):
    return pl.BlockSpec((1, 1, d), lambda i: (jnp.minimum(i, 1), 0, 0))


def _norm_mod_kernel(x_ref, g_ref, sh_ref, sc_ref, o_ref):
    x = x_ref[...]
    y = x * lax.rsqrt(jnp.mean(x * x, axis=-1, keepdims=True) + RMS_EPS) * g_ref[...]
    o_ref[...] = (y * (1 + sc_ref[0]) + sh_ref[0]).astype(o_ref.dtype)


def norm_mod(x, g, shift, scale):
    R, D = x.shape
    row = pl.BlockSpec((ROW_TILE, D), lambda i: (i, 0))
    return pl.pallas_call(
        _norm_mod_kernel,
        grid=(R // ROW_TILE,),
        in_specs=[row, pl.BlockSpec((1, D), lambda i: (0, 0)), _group_spec(D), _group_spec(D)],
        out_specs=row,
        out_shape=jax.ShapeDtypeStruct((R, D), BF16),
        compiler_params=_params(("parallel",)),
        name="norm_mod",
    )(x, g.reshape(1, D), shift.reshape(2, 1, D), scale.reshape(2, 1, D))


def _post_res_kernel(x_ref, y_ref, g_ref, gate_ref, o_ref, *, coef):
    y = y_ref[...]
    yn = y * lax.rsqrt(jnp.mean(y * y, axis=-1, keepdims=True) + RMS_EPS) * g_ref[...]
    o_ref[...] = x_ref[...] + coef * gate_ref[0] * yn


def _post_res_norm_kernel(x_ref, y_ref, g_ref, gate_ref, gn_ref, sh_ref, sc_ref, o_ref, h_ref, *, coef):
    y = y_ref[...]
    yn = y * lax.rsqrt(jnp.mean(y * y, axis=-1, keepdims=True) + RMS_EPS) * g_ref[...]
    x = x_ref[...] + coef * gate_ref[0] * yn
    o_ref[...] = x
    h = x * lax.rsqrt(jnp.mean(x * x, axis=-1, keepdims=True) + RMS_EPS) * gn_ref[...]
    h_ref[...] = (h * (1 + sc_ref[0]) + sh_ref[0]).astype(h_ref.dtype)


def post_res_norm(x, y, g, gate, coef, g_next, shift_next, scale_next):
    R, D = x.shape
    row = pl.BlockSpec((ROW_TILE, D), lambda i: (i, 0))
    vec = pl.BlockSpec((1, D), lambda i: (0, 0))
    grp = lambda t: t.reshape(2, 1, D)
    return pl.pallas_call(
        functools.partial(_post_res_norm_kernel, coef=coef),
        grid=(R // ROW_TILE,),
        in_specs=[row, row, vec, _group_spec(D), vec, _group_spec(D), _group_spec(D)],
        out_specs=[row, row],
        out_shape=[jax.ShapeDtypeStruct((R, D), F32), jax.ShapeDtypeStruct((R, D), BF16)],
        compiler_params=_params(("parallel",)),
        name="post_res_norm",
    )(x, y, g.reshape(1, D), grp(gate), g_next.reshape(1, D), grp(shift_next), grp(scale_next))


def post_res(x, y, g, gate, coef):
    R, D = x.shape
    row = pl.BlockSpec((ROW_TILE, D), lambda i: (i, 0))
    return pl.pallas_call(
        functools.partial(_post_res_kernel, coef=coef),
        grid=(R // ROW_TILE,),
        in_specs=[row, row, pl.BlockSpec((1, D), lambda i: (0, 0)), _group_spec(D)],
        out_specs=row,
        out_shape=jax.ShapeDtypeStruct((R, D), F32),
        compiler_params=_params(("parallel",)),
        name="post_res",
    )(x, y, g.reshape(1, D), gate.reshape(2, 1, D))


WKV_TB = 8
WKV_KG = MIX_HALF // LANE
WKV_VC = RWKV_HEAD // 8
WKV_CG = 2
PREP_TB = 128


def _kmajor(w):
    lead = w.shape[:-1]
    return jnp.swapaxes(w.reshape(*lead, RWKV_HEADS, RWKV_HEAD), -1, -2).reshape(*lead, MIX_HALF)


def _fold_lanes(s):
    s = s + pltpu.roll(s, 64, 1)
    return s + pltpu.roll(s, 32, 1)


def _rwkv_prep_kernel(p_ref, hp_ref, hn_ref, mu_ref, w2_ref, a2_ref, g2_ref, vec_ref,
                      a_out, b0_out, b1_out, k0_out, k1_out, w0_out, w1_out, r_out, v_out, g_out, bonus_out):
    W = MIX_HALF
    i = pl.program_id(0)
    cb = CTX_LEN // PREP_TB
    has_prev = jnp.logical_and(i != 0, i != cb).astype(F32)
    has_next = jnp.logical_and(i != cb - 1, i != pl.num_programs(0) - 1).astype(F32)
    row = lax.broadcasted_iota(jnp.int32, (PREP_TB, 1), 0)

    def shifted(c0, width):
        x = p_ref[:, c0:c0 + width]
        up = jnp.where(row == 0, hp_ref[7:8, c0:c0 + width] * has_prev, pltpu.roll(x, 1, 0))
        dn = jnp.where(row == PREP_TB - 1, hn_ref[0:1, c0:c0 + width] * has_next,
                       pltpu.roll(x, PREP_TB - 1, 0))
        return x + mu_ref[0:1, c0:c0 + width] * (up - x) + mu_ref[1:2, c0:c0 + width] * (dn - x)

    lo = shifted(AB_WD, AB_NA - AB_WD)
    wd, ad, gd = lo[:, :LANE], lo[:, LANE:2 * LANE], lo[:, 2 * LANE:]
    wlo = jnp.dot(jnp.tanh(wd).astype(BF16), w2_ref[...], preferred_element_type=F32)
    alo = jnp.dot(ad.astype(BF16), a2_ref[...], preferred_element_type=F32)
    g_out[...] = jnp.dot(jax.nn.sigmoid(gd).astype(BF16), g2_ref[...], preferred_element_type=F32)

    ss = None
    for g in range(WKV_KG):
        sl = slice(g * LANE, (g + 1) * LANE)
        kk = shifted(W + g * LANE, LANE) * vec_ref[4:5, sl]
        ss = kk * kk if ss is None else ss + kk * kk
    nrm = jnp.maximum(jnp.sqrt(_fold_lanes(ss)), 1e-12)
    bonus = None
    for g in range(WKV_KG):
        sl = slice(g * LANE, (g + 1) * LANE)
        r = shifted(g * LANE, LANE)
        k = shifted(W + g * LANE, LANE)
        r_out[:, sl] = r
        v_out[:, sl] = shifted(2 * W + g * LANE, LANE)
        kk = k * vec_ref[4:5, sl] / nrm
        a_out[:, sl] = -kk
        for d, (w_out, k_out, b_out) in enumerate(((w0_out, k0_out, b0_out), (w1_out, k1_out, b1_out))):
            dl = slice(d * W + g * LANE, d * W + (g + 1) * LANE)
            logw = -jax.nn.softplus(-(vec_ref[d:d + 1, sl] + wlo[:, dl])) - 0.5
            w_out[:, sl] = jnp.exp(-jnp.exp(logw))
            a = jax.nn.sigmoid(vec_ref[2 + d:3 + d, sl] + alo[:, dl])
            kd = k * (1 + (a - 1) * vec_ref[5:6, sl])
            k_out[:, sl] = kd
            b_out[:, sl] = kk * a
            term = r * kd * vec_ref[6:7, sl]
            bonus = term if bonus is None else bonus + term
    bonus_out[...] = _fold_lanes(bonus)


def rwkv_prep(p, mu, w2c, a2c, g2, vec):
    R = p.shape[0]
    W = MIX_HALF
    hb = PREP_TB // 8
    full = lambda a: pl.BlockSpec(a.shape, lambda i: (0,) * a.ndim)
    wide = pl.BlockSpec((PREP_TB, W), lambda i: (i, 0))
    return pl.pallas_call(
        _rwkv_prep_kernel,
        grid=(R // PREP_TB,),
        in_specs=[pl.BlockSpec((PREP_TB, AB_NA), lambda i: (i, 0)),
                  pl.BlockSpec((8, AB_NA), lambda i: (jnp.maximum(i * hb - 1, 0), 0)),
                  pl.BlockSpec((8, AB_NA), lambda i: (jnp.minimum((i + 1) * hb, R // 8 - 1), 0)),
                  full(mu), full(w2c), full(a2c), full(g2), full(vec)],
        out_specs=[wide] * 10 + [pl.BlockSpec((PREP_TB, LANE), lambda i: (i, 0))],
        out_shape=[jax.ShapeDtypeStruct((R, W), F32)] * 10 + [jax.ShapeDtypeStruct((R, LANE), F32)],
        compiler_params=_params(("parallel",)),
        name="rwkv_prep",
    )(p, p, p, mu, w2c, a2c, g2, vec)


def _wkv_kernel(a0_ref, b0_ref, k0_ref, w0_ref, r0_ref, v0_ref, a1_ref, b1_ref, k1_ref, w1_ref, r1_ref, v1_ref,
                y0_ref, y1_ref, s_ref):
    @pl.when(pl.program_id(0) == 0)
    def _():
        s_ref[...] = jnp.zeros_like(s_ref)

    dirs = ((a0_ref, b0_ref, k0_ref, w0_ref, r0_ref, v0_ref, y0_ref),
            (a1_ref, b1_ref, k1_ref, w1_ref, r1_ref, v1_ref, y1_ref))

    for t in range(WKV_TB):
        for d, (a_ref, b_ref, k_ref, w_ref, r_ref, v_ref, y_ref) in enumerate(dirs):
            tt = t if d == 0 else WKV_TB - 1 - t
            row = lambda ref, g: ref[tt:tt + 1, g * LANE:(g + 1) * LANE]
            for c0 in range(0, WKV_VC, WKV_CG):
                cs = range(c0, c0 + WKV_CG)
                acc = {c: None for c in cs}
                for g in range(WKV_KG):
                    ar = row(a_ref, g)
                    for c in cs:
                        term = s_ref[d, g, pl.ds(c * 8, 8), :] * ar
                        acc[c] = term if acc[c] is None else acc[c] + term
                sa = {c: _fold_lanes(acc[c]) for c in cs}
                vv = {c: v_ref[tt, pl.ds(c * 8, 8), :] for c in cs}
                acc = {c: None for c in cs}
                for g in range(WKV_KG):
                    wr, br, kr, rr = row(w_ref, g), row(b_ref, g), row(k_ref, g), row(r_ref, g)
                    for c in cs:
                        s_new = s_ref[d, g, pl.ds(c * 8, 8), :] * wr + sa[c] * br + vv[c] * kr
                        s_ref[d, g, pl.ds(c * 8, 8), :] = s_new
                        term = s_new * rr
                        acc[c] = term if acc[c] is None else acc[c] + term
                for c in cs:
                    y_ref[tt, pl.ds(c * 8, 8), :] = _fold_lanes(acc[c])


def wkv_scan(a, b0, b1, k0, k1, w0, w1, r, vp):
    R = a.shape[0]
    nb = R // WKV_TB
    cb = CTX_LEN // WKV_TB
    fwd = lambda i: i
    bwd = lambda i: jnp.where(i < cb, cb - 1 - i, nb + cb - 1 - i)
    kspec = lambda f: pl.BlockSpec((WKV_TB, MIX_HALF), lambda i: (f(i), 0))
    vspec = lambda f: pl.BlockSpec((WKV_TB, RWKV_HEAD, LANE), lambda i: (f(i), 0, 0))
    yshape = jax.ShapeDtypeStruct((R, RWKV_HEAD, LANE), F32)
    return pl.pallas_call(
        _wkv_kernel,
        grid=(nb,),
        in_specs=[kspec(fwd)] * 5 + [vspec(fwd)] + [kspec(bwd)] * 5 + [vspec(bwd)],
        out_specs=[vspec(fwd), vspec(bwd)],
        out_shape=[yshape, yshape],
        scratch_shapes=[pltpu.VMEM((2, WKV_KG, RWKV_HEAD, LANE), F32)],
        compiler_params=_params(("arbitrary",)),
        name="wkv_scan",
    )(a, b0, k0, w0, r, vp, a, b1, k1, w1, r, vp)


def _shift_prev(y):
    return jnp.pad(y, ((1, 0), (0, 0)))[:-1]


def _shift_next(y):
    return jnp.pad(y, ((0, 1), (0, 0)))[1:]


AB_WD = 3 * MIX_HALF
AB_AD = AB_WD + LANE
AB_GD = AB_AD + LANE
AB_NA = AB_GD + RWKV_GATE_LORA


def _pack_cols(w, cuts, widths):
    pieces = jnp.split(w, cuts, axis=-1)
    pad = [(0, 0)] * (w.ndim - 1)
    return jnp.concatenate([jnp.pad(p, pad + [(0, wd - p.shape[-1])]) for p, wd in zip(pieces, widths)],
                           axis=-1)


_AB_CUTS = (AB_WD, AB_WD + RWKV_DECAY_LORA, AB_WD + RWKV_DECAY_LORA + RWKV_A_LORA, RWKV_COLS)
_AB_WIDTHS = (AB_WD, LANE, LANE, RWKV_GATE_LORA, 3 * MIX_HALF)


def rwkv_mix(p, mu, w0, w2, a0, a2, g2, k_k, k_a, r_k, gn):
    R = p.shape[0]
    km3 = lambda t: jnp.concatenate([_kmajor(x) for x in jnp.split(t, 3, axis=-1)], axis=-1)
    mu_p = _pack_cols(mu, _AB_CUTS[:3], _AB_WIDTHS[:4])
    mu_p = jnp.concatenate([km3(mu_p[:, :AB_WD]), mu_p[:, AB_WD:]], axis=-1)
    lora = lambda w: jnp.pad(jnp.concatenate([_kmajor(w[0]), _kmajor(w[1])], axis=-1),
                             ((0, LANE - w.shape[1]), (0, 0))).astype(BF16)
    vec = jnp.stack([_kmajor(t) for t in (w0[0], w0[1], a0[0], a0[1], k_k, k_a, r_k, jnp.zeros_like(r_k))])
    a, b0, b1, k0, k1, d0, d1, r, v, g, bonus = rwkv_prep(p, mu_p, lora(w2), lora(a2),
                                                          _kmajor(g2).astype(BF16), vec)
    tile = lambda t: t.reshape(R, RWKV_HEAD, RWKV_HEADS)
    vp = jnp.broadcast_to(tile(v)[:, :, None, :], (R, RWKV_HEAD, 4, RWKV_HEADS)).reshape(R, RWKV_HEAD, LANE)
    y0, y1 = wkv_scan(a, b0, b1, k0, k1, d0, d1, r, vp)
    y = (y0 + y1)[:, :, :RWKV_HEADS]
    mean = jnp.mean(y, axis=1, keepdims=True)
    var = jnp.mean(jnp.square(y - mean), axis=1, keepdims=True)
    gn_t = _kmajor(gn).reshape(2, RWKV_HEAD, RWKV_HEADS)
    yn = (y - mean) * lax.rsqrt(var + RWKV_GN_EPS) * gn_t[0] + gn_t[1]
    out = (yn + bonus[:, None, :RWKV_HEADS] * tile(v)) * tile(g)
    return out.reshape(R, MIX_HALF)


NA_WIN = NA_KH * GRID_W


NA_RB = 4


def _na_kernel(q_ref, k_ref, v_ref, bias_ref, o_ref, *, rows):
    dn = (((1,), (1,)), ((), ()))
    k_ctx = k_ref[pl.ds(0, CTX_LEN), :].astype(BF16)
    v_ctx = v_ref[pl.ds(0, CTX_LEN), :].astype(BF16)
    for j in range(NA_RB):
        r = pl.program_id(1) * NA_RB + j
        r0 = jnp.clip(r - NA_KH // 2, 0, rows - NA_KH)
        start = pl.multiple_of(CTX_LEN + r0 * GRID_W, GRID_W)
        qs = slice(j * GRID_W, (j + 1) * GRID_W)
        q = (q_ref[qs, :] * NA_HEAD_DIM ** -0.5).astype(BF16)
        k_win = k_ref[pl.ds(start, NA_WIN), :].astype(BF16)
        s_loc = lax.dot_general(q, k_win, dn, preferred_element_type=F32) + bias_ref[0, r - r0]
        s_ctx = lax.dot_general(q, k_ctx, dn, preferred_element_type=F32)
        m = jnp.maximum(jnp.max(s_loc, axis=-1, keepdims=True), jnp.max(s_ctx, axis=-1, keepdims=True))
        p_loc = jnp.exp(s_loc - m)
        p_ctx = jnp.exp(s_ctx - m)
        den = jnp.sum(p_loc, axis=-1, keepdims=True) + jnp.sum(p_ctx, axis=-1, keepdims=True)
        num = jnp.dot(p_loc.astype(BF16), v_ref[pl.ds(start, NA_WIN), :].astype(BF16),
                      preferred_element_type=F32)
        num += jnp.dot(p_ctx.astype(BF16), v_ctx, preferred_element_type=F32)
        o_ref[qs, :] = num / den


def _na_ctx_kernel(q_ref, k_ref, v_ref, o_ref):
    q = (q_ref[...] * NA_HEAD_DIM ** -0.5).astype(BF16)
    s = lax.dot_general(q, k_ref[...].astype(BF16), (((1,), (1,)), ((), ())), preferred_element_type=F32)
    p = jnp.exp(s - jnp.max(s, axis=-1, keepdims=True))
    num = jnp.dot(p.astype(BF16), v_ref[...].astype(BF16), preferred_element_type=F32)
    o_ref[...] = num / jnp.sum(p, axis=-1, keepdims=True)


def _na_bias(rpb, rows):
    kh = min(NA_KH, rows)
    col = np.arange(GRID_W)
    c0 = np.clip(col - NA_KW // 2, 0, GRID_W - NA_KW)
    key = np.arange(GRID_W)
    inside = (key[None, :] >= c0[:, None]) & (key[None, :] < c0[:, None] + NA_KW)
    dc = np.clip(key[None, :] - col[:, None] + (NA_KW - 1), 0, 2 * NA_KW - 2)
    onehot = np.zeros((GRID_W, GRID_W, 2 * NA_KW - 1), np.float32)
    onehot[col[:, None], key[None, :], dc] = 1.0
    toep = jnp.einsum("qcd,hrd->hrqc", onehot, rpb, precision=lax.Precision.HIGHEST)
    toep = jnp.where(jnp.asarray(inside), toep, NEG)
    return jnp.stack([jnp.concatenate([toep[:, i - j + NA_KH - 1] for i in range(kh)], axis=-1)
                      for j in range(NA_KH)], axis=1)


def na_mix(p, rpb):
    R = p.shape[0]
    T = R - CTX_LEN
    rows = T // GRID_W
    qb, kb, vb = [(AB_NA + i * MIX_HALF) // NA_HEAD_DIM for i in range(3)]
    qrows = NA_RB * GRID_W
    cb = CTX_LEN // qrows
    bias = _na_bias(rpb, rows)
    strip = lambda b: pl.BlockSpec((R, NA_HEAD_DIM), lambda h, r: (0, b + h))
    lat = pl.pallas_call(
        functools.partial(_na_kernel, rows=rows),
        grid=(NA_HEADS, rows // NA_RB),
        in_specs=[pl.BlockSpec((qrows, NA_HEAD_DIM), lambda h, r: (cb + r, qb + h)),
                  strip(kb), strip(vb),
                  pl.BlockSpec((1, NA_KH, GRID_W, NA_WIN), lambda h, r: (h, 0, 0, 0))],
        out_specs=pl.BlockSpec((qrows, NA_HEAD_DIM), lambda h, r: (r, h)),
        out_shape=jax.ShapeDtypeStruct((T, MIX_HALF), F32),
        compiler_params=_params(("parallel", "arbitrary")),
        name="na_latent",
    )(p, p, p, bias)
    blk = lambda b: pl.BlockSpec((CTX_LEN, NA_HEAD_DIM), lambda h: (0, b + h))
    ctx = pl.pallas_call(
        _na_ctx_kernel,
        grid=(NA_HEADS,),
        in_specs=[blk(qb), blk(kb), blk(vb)],
        out_specs=pl.BlockSpec((CTX_LEN, NA_HEAD_DIM), lambda h: (0, h)),
        out_shape=jax.ShapeDtypeStruct((CTX_LEN, MIX_HALF), F32),
        compiler_params=_params(("parallel",)),
        name="na_context",
    )(p, p, p)
    return jnp.concatenate([ctx, lat], axis=0)


ML_CHUNK = CTX_LEN
CD_GT = 2 * ML_HEADS * ML_DQK + 2 * MIX_HALF
CD_HY = CD_GT + LANE


def _mlstm_kernel(q_ref, k_ref, v_ref, lir_ref, lfr_ref, lic_ref, lfc_ref, o_ref, c_ref, n_ref, m_ref):
    L = ML_CHUNK

    @pl.when(pl.program_id(2) == 0)
    def _():
        c_ref[...] = jnp.zeros_like(c_ref)
        n_ref[...] = jnp.zeros_like(n_ref)
        m_ref[...] = jnp.zeros_like(m_ref)

    sgn = 1 - 2 * pl.program_id(0)
    row = lax.broadcasted_iota(jnp.int32, (L, L), 0)
    col = lax.broadcasted_iota(jnp.int32, (L, L), 1)
    seen = (row - col) * sgn >= 0
    seen_t = (col - row) * sgn >= 0
    q = q_ref[...]
    k = k_ref[...]
    vb = v_ref[...].astype(BF16)
    li_r, lf_r, li_c, lf_c = lir_ref[0], lfr_ref[0], lic_ref[0], lfc_ref[0]
    b_c = jnp.sum(jnp.where(seen, lf_r, 0.0), axis=1, keepdims=True)
    b_r = jnp.sum(jnp.where(seen_t, lf_c, 0.0), axis=0, keepdims=True)
    m_prev = m_ref[...]
    dmat = jnp.where(seen, b_c - b_r + li_r, NEG)
    inter = b_c + m_prev
    m_t = jnp.maximum(jnp.max(dmat, axis=1, keepdims=True), inter)
    qb = q.astype(BF16)
    s = lax.dot_general(qb, k.astype(BF16), (((1,), (1,)), ((), ())), preferred_element_type=F32)
    s = s * jnp.exp(dmat - m_t)
    dec = jnp.exp(inter - m_t)
    num = jnp.dot(s.astype(BF16), vb, preferred_element_type=F32)
    num += dec * jnp.dot(qb, c_ref[...].astype(BF16), preferred_element_type=F32)
    den = jnp.sum(s, axis=1, keepdims=True) + dec * jnp.sum(q * n_ref[...], axis=1, keepdims=True)
    o_ref[0] = num / jnp.maximum(jnp.abs(den), jnp.exp(-m_t))
    total = jnp.sum(lf_r, axis=1, keepdims=True)
    gl = total - b_c + li_c
    m_new = jnp.maximum(total + m_prev, jnp.max(gl, axis=0, keepdims=True))
    kw = k * jnp.exp(gl - m_new)
    sc = jnp.exp(total + m_prev - m_new)
    c_ref[...] = sc * c_ref[...] + lax.dot_general(kw.astype(BF16), vb, (((0,), (0,)), ((), ())),
                                                   preferred_element_type=F32)
    n_ref[...] = sc * n_ref[...] + jnp.sum(kw, axis=0, keepdims=True)
    m_ref[...] = m_new


def _axial_rope(x):
    T, d = x.shape[0], x.shape[-1]
    half = d // 2
    nf = half // 2
    t = jnp.arange(T)
    row = (t // GRID_W).astype(F32)
    col = (t % GRID_W).astype(F32)
    inv = ROPE_BASE ** (-jnp.arange(nf, dtype=F32) / nf)
    ang = jnp.concatenate([row[:, None] * inv, col[:, None] * inv], axis=-1)[:, None, :]
    cos, sin = jnp.cos(ang), jnp.sin(ang)
    x1, x2 = x[..., :half], x[..., half:]
    return jnp.concatenate([x1 * cos - x2 * sin, x2 * cos + x1 * sin], axis=-1)


def mlstm_mix(p, gate_b, norm_w):
    R = p.shape[0]
    L = ML_CHUNK
    nc = R // L
    QK = ML_HEADS * ML_DQK

    def rope_lat(t):
        th = t.reshape(R, ML_HEADS, ML_DQK)
        return jnp.concatenate([th[:CTX_LEN], _axial_rope(th[CTX_LEN:])], axis=0).reshape(R, QK)

    q = rope_lat(p[:, :QK])
    k = rope_lat(p[:, QK:2 * QK]) * ML_DQK ** -0.5
    o = p[:, 2 * QK + MIX_HALF:CD_GT]
    gt = p[:, CD_GT:CD_GT + 4 * ML_HEADS].reshape(R, 2, 2, ML_HEADS) + gate_b
    log_i = jnp.transpose(gt[:, :, 0], (1, 2, 0))
    log_f = jax.nn.log_sigmoid(jnp.transpose(gt[:, :, 1], (1, 2, 0)))
    as_rows = lambda t: t.reshape(2 * ML_HEADS * nc, 1, L)
    as_cols = lambda t: t.reshape(2 * ML_HEADS * nc, L, 1)

    def chunk(d, c):
        return jnp.where(d == 0, c, jnp.where(c == 0, 0, nc - c))

    gidx = lambda d, h, c: ((d * ML_HEADS + h) * nc + chunk(d, c), 0, 0)
    vb = 2 * QK // ML_DV
    h_dir = pl.pallas_call(
        _mlstm_kernel,
        grid=(2, ML_HEADS, nc),
        in_specs=[pl.BlockSpec((L, ML_DQK), lambda d, h, c: (chunk(d, c), h)),
                  pl.BlockSpec((L, ML_DQK), lambda d, h, c: (chunk(d, c), h)),
                  pl.BlockSpec((L, ML_DV), lambda d, h, c: (chunk(d, c), vb + h)),
                  pl.BlockSpec((1, 1, L), gidx), pl.BlockSpec((1, 1, L), gidx),
                  pl.BlockSpec((1, L, 1), gidx), pl.BlockSpec((1, L, 1), gidx)],
        out_specs=pl.BlockSpec((1, L, ML_DV), lambda d, h, c: (d, chunk(d, c), h)),
        out_shape=jax.ShapeDtypeStruct((2, R, MIX_HALF), F32),
        scratch_shapes=[pltpu.VMEM((ML_DQK, ML_DV), F32), pltpu.VMEM((1, ML_DQK), F32),
                        pltpu.VMEM((1, 1), F32)],
        compiler_params=_params(("parallel", "parallel", "arbitrary")),
        name="mlstm",
    )(q, k, p, as_rows(log_i), as_rows(log_f), as_cols(log_i), as_cols(log_f))
    h = (h_dir[0] + h_dir[1]).reshape(R, ML_HEADS, ML_DV)
    h = h * lax.rsqrt(jnp.mean(h * h, axis=-1, keepdims=True) + RMS_EPS)
    return h.reshape(R, MIX_HALF) * norm_w * jax.nn.sigmoid(o)


def _pad_to(x, axis, mult):
    n = -x.shape[axis] % mult
    if n == 0:
        return x
    pad = [(0, 0)] * x.ndim
    pad[axis] = (0, n)
    return jnp.pad(x, pad)


def mm3p(a, b):
    M, N = a.shape[0], b.shape[1]
    a = _pad_to(_pad_to(a, 1, LANE), 0, 8)
    b = _pad_to(_pad_to(b, 0, LANE), 1, LANE)
    return mm3(a, b)[:M, :N]


def hyena_filters(T, ff1, ff1_b, ff2, ff2_b, ff3, ff3_b, ff_out, freq, decay):
    t = jnp.linspace(0.0, 1.0, T, dtype=F32)[:, None]
    bands = (HY_EMB - 1) // 2
    f = jnp.linspace(1e-4, bands - 1, bands, dtype=F32)[None, :]
    w = (2 * math.pi / T) * jnp.arange(T, dtype=F32)[:, None]
    z = jnp.concatenate([t, jnp.cos(f * w), -jnp.sin(f * w)], axis=-1)
    h = jnp.sin(freq * (mm3p(z, ff1) + ff1_b))
    h = jnp.sin(freq * (mm3p(h, ff2) + ff2_b))
    h = jnp.sin(freq * (mm3p(h, ff3) + ff3_b))
    h = mm3p(h, ff_out) * jnp.exp(-t * jnp.abs(decay))
    return h.reshape(T, HY_ORDER, 2, MIX_HALF)


FFT_N2 = LANE


HY_Q = 4
HY_F1Q = FFT_N2 // HY_Q
HY_F2H = FFT_N2 // 2
HY_QR = HY_F1Q * HY_F2H
HY_NN = FFT_N2 * FFT_N2


@functools.lru_cache(maxsize=None)
def _hy_tables():
    n = FFT_N2
    i = np.arange(n)
    w = np.exp(-2j * np.pi * np.outer(i, i) / n)
    tw = np.exp(-2j * np.pi * np.outer(i, i) / (n * n))
    ca = np.conj(w)[: n // 2] / (n * n)
    f32 = lambda x: np.ascontiguousarray(x, dtype=np.float32)
    return dict(
        wa=f32(np.concatenate([w.real, w.imag], axis=0)),
        wbr=f32(w.real), wbi=f32(w.imag),
        wbc=f32(2 * np.block([[w.real[:, : n // 2], w.imag[:, : n // 2]],
                              [-w.imag[:, : n // 2], w.real[:, : n // 2]]])),
        twr=f32(tw.real).reshape(n, 1, n), twi=f32(tw.imag).reshape(n, 1, n),
        ctr=f32(tw.real.T).reshape(n, 1, n), cti=f32(-tw.imag.T).reshape(n, 1, n),
        car=f32(ca.real), cai=f32(ca.imag))


def _dot3(a_hi, a_lo, b):
    b_hi, b_lo = _split(b)
    acc = jnp.dot(a_hi, b_hi, preferred_element_type=F32)
    acc += jnp.dot(a_hi, b_lo, preferred_element_type=F32)
    return acc + jnp.dot(a_lo, b_hi, preferred_element_type=F32)


def _hy_fwd_kernel(x_ref, wa_ref, wbr_ref, wbi_ref, twr_ref, twi_ref, o_ref, e_ref, a_ref, *, k1):
    n = FFT_N2
    h = HY_F2H
    q = pl.program_id(1)

    @pl.when(q == 0)
    def _():
        wa_hi, wa_lo = _split(wa_ref[...])
        for t2 in range(n):
            y = _dot3(wa_hi, wa_lo, x_ref[pl.ds(t2, k1, stride=n), :])
            a_ref[pl.ds(t2, n, stride=n), :] = y[:n]
            a_ref[pl.ds(HY_NN + t2, n, stride=n), :] = y[n:]
        col = a_ref[pl.ds(0, n), :]
        sign = (1 - 2 * (lax.broadcasted_iota(jnp.int32, (n, 1), 0) % 2)).astype(F32)
        dc = jnp.sum(col, axis=0, keepdims=True)
        nyq = jnp.sum(col * sign, axis=0, keepdims=True)
        r = lax.broadcasted_iota(jnp.int32, e_ref.shape, 0)
        e_ref[...] = jnp.where(r == 0, dc, jnp.where(r == 1, nyq, 0.0))

    cr, ci = wbr_ref[pl.ds(0, h), :], wbi_ref[pl.ds(0, h), :]

    def body(j, carry):
        f1 = q * HY_F1Q + j
        base = pl.multiple_of(f1 * n, n)
        rhs = jnp.concatenate([a_ref[pl.ds(base, n), :], a_ref[pl.ds(HY_NN + base, n), :]], axis=0)
        twr, twi = twr_ref[f1], twi_ref[f1]
        pr = cr * twr - ci * twi
        pi = cr * twi + ci * twr
        lhs = jnp.concatenate([jnp.concatenate([pr, -pi], axis=1), jnp.concatenate([pi, pr], axis=1)], axis=0)
        out = _dot3(*_split(lhs), rhs)
        ob = pl.multiple_of(j * h, h)
        o_ref[pl.ds(ob, h), :] = out[:h]
        o_ref[pl.ds(HY_QR + ob, h), :] = out[h:]
        return carry

    lax.fori_loop(0, HY_F1Q, body, 0, unroll=4)


def hy_fwd(x):
    T_in, C = x.shape
    k1 = T_in // FFT_N2
    tb = _hy_tables()
    const = lambda a: pl.BlockSpec(a.shape, lambda c, q: (0,) * a.ndim)
    wa = tb["wa"][:, :k1]
    args = (wa, tb["wbr"], tb["wbi"], tb["twr"], tb["twi"])
    return pl.pallas_call(
        functools.partial(_hy_fwd_kernel, k1=k1),
        grid=(C // LANE, HY_Q),
        in_specs=[pl.BlockSpec((T_in, LANE), lambda c, q: (0, c))] + [const(a) for a in args],
        out_specs=[pl.BlockSpec((None, 2 * HY_QR, LANE), lambda c, q: (q, 0, c)),
                   pl.BlockSpec((8, LANE), lambda c, q: (0, c))],
        out_shape=[jax.ShapeDtypeStruct((HY_Q, 2 * HY_QR, C), F32), jax.ShapeDtypeStruct((8, C), F32)],
        scratch_shapes=[pltpu.VMEM((2 * HY_NN, LANE), F32)],
        compiler_params=_params(("parallel", "arbitrary")),
        name="hy_fwd",
    )(x, *args)


def _hy_conv_kernel(z_ref, ze_ref, f_ref, fe_ref, wbc_ref, ctr_ref, cti_ref, car_ref, cai_ref, y_ref, g_ref):
    n = FFT_N2
    h = HY_F2H
    q = pl.program_id(1)
    l_hi, l_lo = _split(wbc_ref[...])

    def body(j, carry):
        ob = pl.multiple_of(j * h, h)
        zr, zi = z_ref[pl.ds(ob, h), :], z_ref[pl.ds(HY_QR + ob, h), :]
        fr, fi = f_ref[pl.ds(ob, h), :], f_ref[pl.ds(HY_QR + ob, h), :]
        prod = jnp.concatenate([zr * fr - zi * fi, zr * fi + zi * fr], axis=0)
        out = _dot3(l_hi, l_lo, prod)
        base = pl.multiple_of((q * HY_F1Q + j) * n, n)
        g_ref[pl.ds(base, n), :] = out[:n]
        g_ref[pl.ds(HY_NN + base, n), :] = out[n:]
        return carry

    lax.fori_loop(0, HY_F1Q, body, 0, unroll=4)

    @pl.when(q == HY_Q - 1)
    def _():
        car, cai = car_ref[...], cai_ref[...]
        y_dc = ze_ref[0:1, :] * fe_ref[0:1, :] * (-1.0 / HY_NN)
        y_nyq = ze_ref[1:2, :] * fe_ref[1:2, :] * (1.0 / HY_NN)
        fix = (y_dc + y_nyq, y_dc - y_nyq)
        for t2 in range(n):
            ctr, cti = ctr_ref[t2], cti_ref[t2]
            er = car * ctr - cai * cti
            ei = car * cti + cai * ctr
            rhs = jnp.concatenate([g_ref[pl.ds(t2, n, stride=n), :],
                                   g_ref[pl.ds(HY_NN + t2, n, stride=n), :]], axis=0)
            y_ref[pl.ds(t2, n // 2, stride=n), :] = (
                _dot3(*_split(jnp.concatenate([er, -ei], axis=1)), rhs) + fix[t2 % 2])


def hy_conv(z, f, f_block=0):
    (zs, ze), (fs, fe) = z, f
    C = zs.shape[-1]
    T = HY_NN // 2
    tb = _hy_tables()
    const = lambda a: pl.BlockSpec(a.shape, lambda c, q: (0,) * a.ndim)
    spec = pl.BlockSpec((None, 2 * HY_QR, LANE), lambda c, q: (q, 0, c))
    fspec = pl.BlockSpec((None, 2 * HY_QR, LANE), lambda c, q: (q, 0, c + f_block))
    espec = pl.BlockSpec((8, LANE), lambda c, q: (0, c))
    fespec = pl.BlockSpec((8, LANE), lambda c, q: (0, c + f_block))
    args = (tb["wbc"], tb["ctr"], tb["cti"], tb["car"], tb["cai"])
    return pl.pallas_call(
        _hy_conv_kernel,
        grid=(C // LANE, HY_Q),
        in_specs=[spec, espec, fspec, fespec] + [const(a) for a in args],
        out_specs=pl.BlockSpec((T, LANE), lambda c, q: (0, c)),
        out_shape=jax.ShapeDtypeStruct((T, C), F32),
        scratch_shapes=[pltpu.VMEM((2 * HY_NN, LANE), F32)],
        compiler_params=_params(("parallel", "arbitrary")),
        name="hy_conv",
    )(zs, ze, fs, fe, *args)


def hyena_filter_taps(T, ff1, ff1_b, ff2, ff2_b, ff3, ff3_b, ff_out, freq, decay):
    t = jnp.linspace(0.0, 1.0, T, dtype=F32)[:, None]
    bands = (HY_EMB - 1) // 2
    f = jnp.linspace(1e-4, bands - 1, bands, dtype=F32)[None, :]
    w = (2 * math.pi / T) * jnp.arange(T, dtype=F32)[:, None]
    z = jnp.concatenate([t, jnp.cos(f * w), -jnp.sin(f * w)], axis=-1)
    h = jnp.sin(freq * (mm3p(z, ff1) + ff1_b))
    h = jnp.sin(freq * (mm3p(h, ff2) + ff2_b))
    h = jnp.sin(freq * (mm3p(h, ff3) + ff3_b))
    w_out = ff_out.reshape(-1, HY_ORDER, 2, MIX_HALF)
    dec = jnp.abs(decay).reshape(HY_ORDER, 2, MIX_HALF)
    side = lambda s: (w_out[:, :, s].reshape(-1, HY_ORDER * MIX_HALF), dec[:, s].reshape(HY_ORDER * MIX_HALF))
    (w_f, d_f), (w_b, d_b) = side(0), side(1)
    zero = jnp.zeros((1, h.shape[1]), F32)
    h_b = jnp.concatenate([zero, h[:0:-1]], axis=0)
    lhs = jnp.concatenate([jnp.concatenate([h, jnp.zeros_like(h)], axis=1),
                           jnp.concatenate([jnp.zeros_like(h), h_b], axis=1)], axis=0)
    t_b = jnp.concatenate([jnp.zeros((1, 1), F32), t[:0:-1]], axis=0)
    first = jnp.arange(2 * T)[:, None] < T
    scale = jnp.exp(-jnp.concatenate([t, t_b], axis=0) * jnp.where(first, d_f, d_b))
    return mm3p(lhs, jnp.concatenate([w_f, w_b], axis=0)) * scale


@functools.lru_cache(maxsize=None)
def _dft_small(n):
    i = np.arange(n)
    w = np.exp(-2j * np.pi * np.outer(i, i) / n)
    fwd = np.concatenate([w.real, w.imag], axis=0)
    inv = np.concatenate([w.real, w.imag], axis=1)[: n // 2] / n
    return np.asarray(fwd, np.float32), np.asarray(inv, np.float32)


def _short_conv(z, h_fwd, h_bwd):
    T, C = z.shape
    n = 2 * T
    fwd, inv = _dft_small(n)
    f = jnp.concatenate([h_fwd, jnp.zeros((1, C), F32), h_bwd[:0:-1]], axis=0)
    zs = mm3(fwd[:, :T], z)
    fs = mm3(fwd, f)
    prod = jnp.concatenate([zs[:n] * fs[:n] - zs[n:] * fs[n:], zs[:n] * fs[n:] + zs[n:] * fs[:n]], axis=0)
    return mm3(inv, prod)


def hyena_seq(p, short_w, short_b, conv, bias):
    u = short_w[0] * _shift_prev(p) + short_w[1] * p + short_w[2] * _shift_next(p) + short_b
    z, x1, x2 = jnp.split(u, 3, axis=-1)
    for n, gate in enumerate((x1, x2)):
        z = gate * (conv(z, n) + bias[n] * z)
    return z


def hyena_mix(p, short_w, short_b, filt_params, bias):
    T = p.shape[0] - CTX_LEN
    filt_c = hyena_filters(CTX_LEN, *filt_params)
    out_c = hyena_seq(p[:CTX_LEN], short_w, short_b,
                      lambda z, n: _short_conv(z, filt_c[:, n, 0], filt_c[:, n, 1]), bias)
    assert 2 * T == HY_NN, "the latent convolution kernels are built for 8192 latent tokens"
    spectra = hy_fwd(hyena_filter_taps(T, *filt_params))
    out = hyena_seq(p[CTX_LEN:], short_w, short_b,
                    lambda z, n: hy_conv(hy_fwd(z), spectra, n * MIX_HALF // LANE), bias)
    return jnp.concatenate([out_c, out], axis=0)


def kernel(x, c, ctx, c_ctx, ada_down, ada_up, ada_bias, norm_g, ffn_w1, ffn_w3, ffn_w2, ab_w_in, ab_w_out, rwkv_mu, rwkv_w0, rwkv_w2, rwkv_a0, rwkv_a2, rwkv_g2, rwkv_kk, rwkv_ka, rwkv_rk, rwkv_gn, na_rpb, cd_w_in, cd_w_out, ml_gate_b, ml_norm, hy_short_w, hy_short_b, hy_ff1, hy_ff1_b, hy_ff2, hy_ff2_b, hy_ff3, hy_ff3_b, hy_ff_out, hy_freq, hy_decay, hy_bias):
    assert x.shape[0] == 1 and ctx.shape[1] == CTX_LEN
    X = jnp.concatenate([ctx[0], x[0]], axis=0)
    s2 = _pad_to(jnp.stack([jax.nn.silu(c_ctx), jax.nn.silu(c[0])]), 0, 16)
    n = jnp.arange(MIX_HALF)
    perm = ((n % RWKV_HEADS) * RWKV_HEAD + n // RWKV_HEADS == n[:, None]).astype(BF16)
    mods = [(mm(mm(s2, ada_down[l]), ada_up[l])[:2] + ada_bias[l]).reshape(2, N_SUB, 3, D_MODEL)
            for l in range(DEPTH)]
    hn = norm_mod(X, norm_g[0, 0], mods[0][:, 0, 0], mods[0][:, 0, 1])
    for l in range(DEPTH):
        i = l // 2
        mod = mods[l]
        g = norm_g[l]
        y = mm_w32(swiglu(hn, ffn_w1, ffn_w3, (l, 0)), ffn_w2, (l, 0))
        X, xn = post_res_norm(X, y, g[1], mod[:, 0, 2], FFN_RES, g[2], mod[:, 1, 0], mod[:, 1, 1])
        if l % 2 == 0:
            w_ab = ab_w_in[i].astype(BF16)
            w_rkv = [mm(t, perm, BF16) for t in jnp.split(w_ab[:, :AB_WD], 3, axis=-1)]
            p = mm(xn, _pack_cols(jnp.concatenate(w_rkv + [w_ab[:, AB_WD:]], axis=-1), _AB_CUTS, _AB_WIDTHS))
            ya = rwkv_mix(p, rwkv_mu[i], rwkv_w0[i], rwkv_w2[i], rwkv_a0[i], rwkv_a2[i],
                          rwkv_g2[i], rwkv_kk[i], rwkv_ka[i], rwkv_rk[i], rwkv_gn[i])
            y = jnp.concatenate([ya, na_mix(p, na_rpb[i])], axis=-1)
            w_out = ab_w_out[i].astype(BF16)
            w_rw = mm(perm.T, w_out[:MIX_HALF], BF16)
            yo = mm(y, jnp.concatenate([w_rw, w_out[MIX_HALF:]], axis=0))
        else:
            w_in = _pack_cols(cd_w_in[i].astype(BF16), (CD_GT, ML_COLS), (CD_GT, LANE, P_COLS - CD_HY))
            p = mm(xn, w_in)
            filt_params = (hy_ff1[i], hy_ff1_b[i], hy_ff2[i], hy_ff2_b[i], hy_ff3[i], hy_ff3_b[i],
                           hy_ff_out[i], hy_freq[i], hy_decay[i])
            yd = hyena_mix(p[:, CD_HY:CD_HY + 3 * MIX_HALF], hy_short_w[i], hy_short_b[i], filt_params,
                           hy_bias[i])
            y = jnp.concatenate([mlstm_mix(p, ml_gate_b[i], ml_norm[i]), yd], axis=-1)
            yo = mm_w32(y.astype(BF16), cd_w_out, (i,))
        X, hn = post_res_norm(X, yo, g[3], mod[:, 1, 2], 1.0, g[4], mod[:, 2, 0], mod[:, 2, 1])
        y = mm_w32(swiglu(hn, ffn_w1, ffn_w3, (l, 1)), ffn_w2, (l, 1))
        if l + 1 < DEPTH:
            nxt = mods[l + 1]
            X, hn = post_res_norm(X, y, g[5], mod[:, 2, 2], FFN_RES, norm_g[l + 1, 0], nxt[:, 0, 0], nxt[:, 0, 1])
        else:
            X = post_res(X, y, g[5], mod[:, 2, 2], FFN_RES)
    return X[CTX_LEN:][None]
```

```python
import functools
import math

import jax
import jax.numpy as jnp
import numpy as np
from jax import lax
from jax.experimental import pallas as pl
from jax.experimental.pallas import tpu as pltpu

F32 = jnp.float32
BF16 = jnp.bfloat16

D_MODEL = 4096
DEPTH = 4
GRID_W = 64
CTX_LEN = 256
N_SUB = 3
FFN_RES = 0.5
D_FF = 5632
RMS_EPS = 1e-6
MIX_HALF = D_MODEL // 2

RWKV_HEAD = 64
RWKV_HEADS = MIX_HALF // RWKV_HEAD
RWKV_DECAY_LORA = 96
RWKV_A_LORA = 96
RWKV_GATE_LORA = 256
RWKV_COLS = 3 * MIX_HALF + RWKV_DECAY_LORA + RWKV_A_LORA + RWKV_GATE_LORA
RWKV_GN_EPS = 64e-5

NA_HEAD_DIM = 128
NA_HEADS = MIX_HALF // NA_HEAD_DIM
NA_KH = 8
NA_KW = 16

ML_HEADS = 4
ML_DV = MIX_HALF // ML_HEADS
ML_DQK = ML_DV // 2
ML_COLS = 2 * ML_HEADS * ML_DQK + 2 * MIX_HALF + 4 * ML_HEADS
ROPE_BASE = 10000.0

HY_ORDER = 2
HY_EMB = 33

LANE = 128
ROW_TILE = CTX_LEN
VMEM_LIMIT = 56 * 1024 * 1024
P_COLS = 12800
NEG = -1e30


def _params(sem):
    return pltpu.CompilerParams(dimension_semantics=sem, vmem_limit_bytes=VMEM_LIMIT)


def _pick(n, cands):
    for c in cands:
        if n % c == 0:
            return c
    return n


def _mm_kernel(a_ref, b_ref, o_ref):
    o_ref[...] = jnp.dot(a_ref[...], b_ref[...], preferred_element_type=F32).astype(o_ref.dtype)


def _split(x):
    hi = x.astype(BF16)
    lo = (x - hi.astype(F32)).astype(BF16)
    return hi, lo


def _mm3_kernel(a_ref, b_ref, o_ref):
    a_hi, a_lo = _split(a_ref[...])
    b_hi, b_lo = _split(b_ref[...])
    acc = jnp.dot(a_hi, b_hi, preferred_element_type=F32)
    acc += jnp.dot(a_hi, b_lo, preferred_element_type=F32)
    acc += jnp.dot(a_lo, b_hi, preferred_element_type=F32)
    o_ref[...] = acc


def _mm_call(kern, a, b, out_dtype, name):
    M, K = a.shape
    N = b.shape[1]
    tm = _pick(M, (1408, 1024, 768, 512, 256, 128, 64, 32, 16, 8))
    tn = _pick(N, (512, 256, 128))
    return pl.pallas_call(
        kern,
        grid=(M // tm, N // tn),
        in_specs=[pl.BlockSpec((tm, K), lambda i, j: (i, 0)),
                  pl.BlockSpec((K, tn), lambda i, j: (0, j))],
        out_specs=pl.BlockSpec((tm, tn), lambda i, j: (i, j)),
        out_shape=jax.ShapeDtypeStruct((M, N), out_dtype),
        compiler_params=_params(("parallel", "parallel")),
        name=name,
    )(a, b)


def mm(a, b, out_dtype=F32):
    return _mm_call(_mm_kernel, a.astype(BF16), b.astype(BF16), out_dtype, "mm")


def mm3(a, b):
    return _mm_call(_mm3_kernel, a.astype(F32), b.astype(F32), F32, "mm3")


def _mm_w32_kernel(a_ref, w_ref, o_ref, wb_ref):
    @pl.when(pl.program_id(1) == 0)
    def _():
        wb_ref[...] = w_ref[...].astype(BF16)

    o_ref[...] = jnp.dot(a_ref[...], wb_ref[...], preferred_element_type=F32).astype(o_ref.dtype)


def _swiglu_kernel(a_ref, w1_ref, w3_ref, o_ref, w1b_ref, w3b_ref):
    @pl.when(pl.program_id(1) == 0)
    def _():
        w1b_ref[...] = w1_ref[...].astype(BF16)
        w3b_ref[...] = w3_ref[...].astype(BF16)

    a = a_ref[...]
    h1 = jnp.dot(a, w1b_ref[...], preferred_element_type=F32)
    h3 = jnp.dot(a, w3b_ref[...], preferred_element_type=F32)
    o_ref[...] = (h1 * jax.nn.sigmoid(h1) * h3).astype(o_ref.dtype)


def _w32_call(kern, a, ws, tms, tn, out_dtype, name, lead=()):
    M, K = a.shape
    N = ws[0].shape[-1]
    tm = _pick(M, tms)
    wspec = pl.BlockSpec((None,) * len(lead) + (K, tn), lambda j, i: tuple(lead) + (0, j))
    return pl.pallas_call(
        kern,
        grid=(N // tn, M // tm),
        in_specs=[pl.BlockSpec((tm, K), lambda j, i: (i, 0))] + [wspec] * len(ws),
        out_specs=pl.BlockSpec((tm, tn), lambda j, i: (i, j)),
        out_shape=jax.ShapeDtypeStruct((M, N), out_dtype),
        scratch_shapes=[pltpu.VMEM((K, tn), BF16)] * len(ws),
        compiler_params=_params(("parallel", "arbitrary")),
        name=name,
    )(a, *ws)


def swiglu(a, w1, w3, lead=()):
    return _w32_call(_swiglu_kernel, a, (w1, w3), (1408, 768, 512, 256), 256, BF16, "swiglu", lead)


def mm_w32(a, w, lead=()):
    return _w32_call(_mm_w32_kernel, a, (w,), (768, 512, 256), 512, F32, "mm_w32", lead)


def _group_spec(d):
    return pl.BlockSpec((1, 1, d), lambda i: (jnp.minimum(i, 1), 0, 0))


def _norm_mod_kernel(x_ref, g_ref, sh_ref, sc_ref, o_ref):
    x = x_ref[...]
    y = x * lax.rsqrt(jnp.mean(x * x, axis=-1, keepdims=True) + RMS_EPS) * g_ref[...]
    o_ref[...] = (y * (1 + sc_ref[0]) + sh_ref[0]).astype(o_ref.dtype)


def norm_mod(x, g, shift, scale):
    R, D = x.shape
    row = pl.BlockSpec((ROW_TILE, D), lambda i: (i, 0))
    return pl.pallas_call(
        _norm_mod_kernel,
        grid=(R // ROW_TILE,),
        in_specs=[row, pl.BlockSpec((1, D), lambda i: (0, 0)), _group_spec(D), _group_spec(D)],
        out_specs=row,
        out_shape=jax.ShapeDtypeStruct((R, D), BF16),
        compiler_params=_params(("parallel",)),
        name="norm_mod",
    )(x, g.reshape(1, D), shift.reshape(2, 1, D), scale.reshape(2, 1, D))


def _post_res_kernel(x_ref, y_ref, g_ref, gate_ref, o_ref, *, coef):
    y = y_ref[...]
    yn = y * lax.rsqrt(jnp.mean(y * y, axis=-1, keepdims=True) + RMS_EPS) * g_ref[...]
    o_ref[...] = x_ref[...] + coef * gate_ref[0] * yn


def _post_res_norm_kernel(x_ref, y_ref, g_ref, gate_ref, gn_ref, sh_ref, sc_ref, o_ref, h_ref, *, coef):
    y = y_ref[...]
    yn = y * lax.rsqrt(jnp.mean(y * y, axis=-1, keepdims=True) + RMS_EPS) * g_ref[...]
    x = x_ref[...] + coef * gate_ref[0] * yn
    o_ref[...] = x
    h = x * lax.rsqrt(jnp.mean(x * x, axis=-1, keepdims=True) + RMS_EPS) * gn_ref[...]
    h_ref[...] = (h * (1 + sc_ref[0]) + sh_ref[0]).astype(h_ref.dtype)


def post_res_norm(x, y, g, gate, coef, g_next, shift_next, scale_next):
    R, D = x.shape
    row = pl.BlockSpec((ROW_TILE, D), lambda i: (i, 0))
    vec = pl.BlockSpec((1, D), lambda i: (0, 0))
    grp = lambda t: t.reshape(2, 1, D)
    return pl.pallas_call(
        functools.partial(_post_res_norm_kernel, coef=coef),
        grid=(R // ROW_TILE,),
        in_specs=[row, row, vec, _group_spec(D), vec, _group_spec(D), _group_spec(D)],
        out_specs=[row, row],
        out_shape=[jax.ShapeDtypeStruct((R, D), F32), jax.ShapeDtypeStruct((R, D), BF16)],
        compiler_params=_params(("parallel",)),
        name="post_res_norm",
    )(x, y, g.reshape(1, D), grp(gate), g_next.reshape(1, D), grp(shift_next), grp(scale_next))


def post_res(x, y, g, gate, coef):
    R, D = x.shape
    row = pl.BlockSpec((ROW_TILE, D), lambda i: (i, 0))
    return pl.pallas_call(
        functools.partial(_post_res_kernel, coef=coef),
        grid=(R // ROW_TILE,),
        in_specs=[row, row, pl.BlockSpec((1, D), lambda i: (0, 0)), _group_spec(D)],
        out_specs=row,
        out_shape=jax.ShapeDtypeStruct((R, D), F32),
        compiler_params=_params(("parallel",)),
        name="post_res",
    )(x, y, g.reshape(1, D), gate.reshape(2, 1, D))


WKV_TB = 8
WKV_KG = MIX_HALF // LANE
WKV_VC = RWKV_HEAD // 8
WKV_CG = 2
PREP_TB = 128


def _kmajor(w):
    lead = w.shape[:-1]
    return jnp.swapaxes(w.reshape(*lead, RWKV_HEADS, RWKV_HEAD), -1, -2).reshape(*lead, MIX_HALF)


def _fold_lanes(s):
    s = s + pltpu.roll(s, 64, 1)
    return s + pltpu.roll(s, 32, 1)


def _rwkv_prep_kernel(p_ref, hp_ref, hn_ref, mu_ref, w2_ref, a2_ref, g2_ref, vec_ref,
                      a_out, b0_out, b1_out, k0_out, k1_out, w0_out, w1_out, r_out, v_out, g_out, bonus_out):
    W = MIX_HALF
    i = pl.program_id(0)
    cb = CTX_LEN // PREP_TB
    has_prev = jnp.logical_and(i != 0, i != cb).astype(F32)
    has_next = jnp.logical_and(i != cb - 1, i != pl.num_programs(0) - 1).astype(F32)
    row = lax.broadcasted_iota(jnp.int32, (PREP_TB, 1), 0)

    def shifted(c0, width):
        x = p_ref[:, c0:c0 + width]
        up = jnp.where(row == 0, hp_ref[7:8, c0:c0 + width] * has_prev, pltpu.roll(x, 1, 0))
        dn = jnp.where(row == PREP_TB - 1, hn_ref[0:1, c0:c0 + width] * has_next,
                       pltpu.roll(x, PREP_TB - 1, 0))
        return x + mu_ref[0:1, c0:c0 + width] * (up - x) + mu_ref[1:2, c0:c0 + width] * (dn - x)

    lo = shifted(AB_WD, AB_NA - AB_WD)
    wd, ad, gd = lo[:, :LANE], lo[:, LANE:2 * LANE], lo[:, 2 * LANE:]
    wlo = jnp.dot(jnp.tanh(wd).astype(BF16), w2_ref[...], preferred_element_type=F32)
    alo = jnp.dot(ad.astype(BF16), a2_ref[...], preferred_element_type=F32)
    g_out[...] = jnp.dot(jax.nn.sigmoid(gd).astype(BF16), g2_ref[...], preferred_element_type=F32)

    ss = None
    for g in range(WKV_KG):
        sl = slice(g * LANE, (g + 1) * LANE)
        kk = shifted(W + g * LANE, LANE) * vec_ref[4:5, sl]
        ss = kk * kk if ss is None else ss + kk * kk
    nrm = jnp.maximum(jnp.sqrt(_fold_lanes(ss)), 1e-12)
    bonus = None
    for g in range(WKV_KG):
        sl = slice(g * LANE, (g + 1) * LANE)
        r = shifted(g * LANE, LANE)
        k = shifted(W + g * LANE, LANE)
        r_out[:, sl] = r
        v_out[:, sl] = shifted(2 * W + g * LANE, LANE)
        kk = k * vec_ref[4:5, sl] / nrm
        a_out[:, sl] = -kk
        for d, (w_out, k_out, b_out) in enumerate(((w0_out, k0_out, b0_out), (w1_out, k1_out, b1_out))):
            dl = slice(d * W + g * LANE, d * W + (g + 1) * LANE)
            logw = -jax.nn.softplus(-(vec_ref[d:d + 1, sl] + wlo[:, dl])) - 0.5
            w_out[:, sl] = jnp.exp(-jnp.exp(logw))
            a = jax.nn.sigmoid(vec_ref[2 + d:3 + d, sl] + alo[:, dl])
            kd = k * (1 + (a - 1) * vec_ref[5:6, sl])
            k_out[:, sl] = kd
            b_out[:, sl] = kk * a
            term = r * kd * vec_ref[6:7, sl]
            bonus = term if bonus is None else bonus + term
    bonus_out[...] = _fold_lanes(bonus)


def rwkv_prep(p, mu, w2c, a2c, g2, vec):
    R = p.shape[0]
    W = MIX_HALF
    hb = PREP_TB // 8
    full = lambda a: pl.BlockSpec(a.shape, lambda i: (0,) * a.ndim)
    wide = pl.BlockSpec((PREP_TB, W), lambda i: (i, 0))
    return pl.pallas_call(
        _rwkv_prep_kernel,
        grid=(R // PREP_TB,),
        in_specs=[pl.BlockSpec((PREP_TB, AB_NA), lambda i: (i, 0)),
                  pl.BlockSpec((8, AB_NA), lambda i: (jnp.maximum(i * hb - 1, 0), 0)),
                  pl.BlockSpec((8, AB_NA), lambda i: (jnp.minimum((i + 1) * hb, R // 8 - 1), 0)),
                  full(mu), full(w2c), full(a2c), full(g2), full(vec)],
        out_specs=[wide] * 10 + [pl.BlockSpec((PREP_TB, LANE), lambda i: (i, 0))],
        out_shape=[jax.ShapeDtypeStruct((R, W), F32)] * 10 + [jax.ShapeDtypeStruct((R, LANE), F32)],
        compiler_params=_params(("parallel",)),
        name="rwkv_prep",
    )(p, p, p, mu, w2c, a2c, g2, vec)


def _wkv_kernel(a0_ref, b0_ref, k0_ref, w0_ref, r0_ref, v0_ref, a1_ref, b1_ref, k1_ref, w1_ref, r1_ref, v1_ref,
                y0_ref, y1_ref, s_ref):
    @pl.when(pl.program_id(0) == 0)
    def _():
        s_ref[...] = jnp.zeros_like(s_ref)

    dirs = ((a0_ref, b0_ref, k0_ref, w0_ref, r0_ref, v0_ref, y0_ref),
            (a1_ref, b1_ref, k1_ref, w1_ref, r1_ref, v1_ref, y1_ref))

    for t in range(WKV_TB):
        for d, (a_ref, b_ref, k_ref, w_ref, r_ref, v_ref, y_ref) in enumerate(dirs):
            tt = t if d == 0 else WKV_TB - 1 - t
            row = lambda ref, g: ref[tt:tt + 1, g * LANE:(g + 1) * LANE]
            for c0 in range(0, WKV_VC, WKV_CG):
                cs = range(c0, c0 + WKV_CG)
                acc = {c: None for c in cs}
                for g in range(WKV_KG):
                    ar = row(a_ref, g)
                    for c in cs:
                        term = s_ref[d, g, pl.ds(c * 8, 8), :] * ar
                        acc[c] = term if acc[c] is None else acc[c] + term
                sa = {c: _fold_lanes(acc[c]) for c in cs}
                vv = {c: v_ref[tt, pl.ds(c * 8, 8), :] for c in cs}
                acc = {c: None for c in cs}
                for g in range(WKV_KG):
                    wr, br, kr, rr = row(w_ref, g), row(b_ref, g), row(k_ref, g), row(r_ref, g)
                    for c in cs:
                        s_new = s_ref[d, g, pl.ds(c * 8, 8), :] * wr + sa[c] * br + vv[c] * kr
                        s_ref[d, g, pl.ds(c * 8, 8), :] = s_new
                        term = s_new * rr
                        acc[c] = term if acc[c] is None else acc[c] + term
                for c in cs:
                    y_ref[tt, pl.ds(c * 8, 8), :] = _fold_lanes(acc[c])


def wkv_scan(a, b0, b1, k0, k1, w0, w1, r, vp):
    R = a.shape[0]
    nb = R // WKV_TB
    cb = CTX_LEN // WKV_TB
    fwd = lambda i: i
    bwd = lambda i: jnp.where(i < cb, cb - 1 - i, nb + cb - 1 - i)
    kspec = lambda f: pl.BlockSpec((WKV_TB, MIX_HALF), lambda i: (f(i), 0))
    vspec = lambda f: pl.BlockSpec((WKV_TB, RWKV_HEAD, LANE), lambda i: (f(i), 0, 0))
    yshape = jax.ShapeDtypeStruct((R, RWKV_HEAD, LANE), F32)
    return pl.pallas_call(
        _wkv_kernel,
        grid=(nb,),
        in_specs=[kspec(fwd)] * 5 + [vspec(fwd)] + [kspec(bwd)] * 5 + [vspec(bwd)],
        out_specs=[vspec(fwd), vspec(bwd)],
        out_shape=[yshape, yshape],
        scratch_shapes=[pltpu.VMEM((2, WKV_KG, RWKV_HEAD, LANE), F32)],
        compiler_params=_params(("arbitrary",)),
        name="wkv_scan",
    )(a, b0, k0, w0, r, vp, a, b1, k1, w1, r, vp)


def _shift_prev(y):
    return jnp.pad(y, ((1, 0), (0, 0)))[:-1]


def _shift_next(y):
    return jnp.pad(y, ((0, 1), (0, 0)))[1:]


AB_WD = 3 * MIX_HALF
AB_AD = AB_WD + LANE
AB_GD = AB_AD + LANE
AB_NA = AB_GD + RWKV_GATE_LORA


def _pack_cols(w, cuts, widths):
    pieces = jnp.split(w, cuts, axis=-1)
    pad = [(0, 0)] * (w.ndim - 1)
    return jnp.concatenate([jnp.pad(p, pad + [(0, wd - p.shape[-1])]) for p, wd in zip(pieces, widths)],
                           axis=-1)


_AB_CUTS = (AB_WD, AB_WD + RWKV_DECAY_LORA, AB_WD + RWKV_DECAY_LORA + RWKV_A_LORA, RWKV_COLS)
_AB_WIDTHS = (AB_WD, LANE, LANE, RWKV_GATE_LORA, 3 * MIX_HALF)


def rwkv_mix(p, mu, w0, w2, a0, a2, g2, k_k, k_a, r_k, gn):
    R = p.shape[0]
    km3 = lambda t: jnp.concatenate([_kmajor(x) for x in jnp.split(t, 3, axis=-1)], axis=-1)
    mu_p = _pack_cols(mu, _AB_CUTS[:3], _AB_WIDTHS[:4])
    mu_p = jnp.concatenate([km3(mu_p[:, :AB_WD]), mu_p[:, AB_WD:]], axis=-1)
    lora = lambda w: jnp.pad(jnp.concatenate([_kmajor(w[0]), _kmajor(w[1])], axis=-1),
                             ((0, LANE - w.shape[1]), (0, 0))).astype(BF16)
    vec = jnp.stack([_kmajor(t) for t in (w0[0], w0[1], a0[0], a0[1], k_k, k_a, r_k, jnp.zeros_like(r_k))])
    a, b0, b1, k0, k1, d0, d1, r, v, g, bonus = rwkv_prep(p, mu_p, lora(w2), lora(a2),
                                                          _kmajor(g2).astype(BF16), vec)
    tile = lambda t: t.reshape(R, RWKV_HEAD, RWKV_HEADS)
    vp = jnp.broadcast_to(tile(v)[:, :, None, :], (R, RWKV_HEAD, 4, RWKV_HEADS)).reshape(R, RWKV_HEAD, LANE)
    y0, y1 = wkv_scan(a, b0, b1, k0, k1, d0, d1, r, vp)
    y = (y0 + y1)[:, :, :RWKV_HEADS]
    mean = jnp.mean(y, axis=1, keepdims=True)
    var = jnp.mean(jnp.square(y - mean), axis=1, keepdims=True)
    gn_t = _kmajor(gn).reshape(2, RWKV_HEAD, RWKV_HEADS)
    yn = (y - mean) * lax.rsqrt(var + RWKV_GN_EPS) * gn_t[0] + gn_t[1]
    out = (yn + bonus[:, None, :RWKV_HEADS] * tile(v)) * tile(g)
    return out.reshape(R, MIX_HALF)


NA_WIN = NA_KH * GRID_W


NA_RB = 8


def _na_kernel(q_ref, k_ref, v_ref, bias_ref, o_ref, *, rows):
    dn = (((1,), (1,)), ((), ()))
    k_ctx = k_ref[pl.ds(0, CTX_LEN), :].astype(BF16)
    v_ctx = v_ref[pl.ds(0, CTX_LEN), :].astype(BF16)
    for j in range(NA_RB):
        r = pl.program_id(1) * NA_RB + j
        r0 = jnp.clip(r - NA_KH // 2, 0, rows - NA_KH)
        start = pl.multiple_of(CTX_LEN + r0 * GRID_W, GRID_W)
        qs = slice(j * GRID_W, (j + 1) * GRID_W)
        q_at = pl.multiple_of(CTX_LEN + r * GRID_W, GRID_W)
        q = (q_ref[pl.ds(q_at, GRID_W), :] * NA_HEAD_DIM ** -0.5).astype(BF16)
        k_win = k_ref[pl.ds(start, NA_WIN), :].astype(BF16)
        s_loc = lax.dot_general(q, k_win, dn, preferred_element_type=F32) + bias_ref[0, r - r0]
        s_ctx = lax.dot_general(q, k_ctx, dn, preferred_element_type=F32)
        m = jnp.maximum(jnp.max(s_loc, axis=-1, keepdims=True), jnp.max(s_ctx, axis=-1, keepdims=True))
        p_loc = jnp.exp(s_loc - m)
        p_ctx = jnp.exp(s_ctx - m)
        den = jnp.sum(p_loc, axis=-1, keepdims=True) + jnp.sum(p_ctx, axis=-1, keepdims=True)
        num = jnp.dot(p_loc.astype(BF16), v_ref[pl.ds(start, NA_WIN), :].astype(BF16),
                      preferred_element_type=F32)
        num += jnp.dot(p_ctx.astype(BF16), v_ctx, preferred_element_type=F32)
        o_ref[qs, :] = num / den


def _na_ctx_kernel(q_ref, k_ref, v_ref, o_ref):
    q = (q_ref[...] * NA_HEAD_DIM ** -0.5).astype(BF16)
    s = lax.dot_general(q, k_ref[...].astype(BF16), (((1,), (1,)), ((), ())), preferred_element_type=F32)
    p = jnp.exp(s - jnp.max(s, axis=-1, keepdims=True))
    num = jnp.dot(p.astype(BF16), v_ref[...].astype(BF16), preferred_element_type=F32)
    o_ref[...] = num / jnp.sum(p, axis=-1, keepdims=True)


def _na_bias(rpb, rows):
    kh = min(NA_KH, rows)
    col = np.arange(GRID_W)
    c0 = np.clip(col - NA_KW // 2, 0, GRID_W - NA_KW)
    key = np.arange(GRID_W)
    inside = (key[None, :] >= c0[:, None]) & (key[None, :] < c0[:, None] + NA_KW)
    dc = np.clip(key[None, :] - col[:, None] + (NA_KW - 1), 0, 2 * NA_KW - 2)
    onehot = np.zeros((GRID_W, GRID_W, 2 * NA_KW - 1), np.float32)
    onehot[col[:, None], key[None, :], dc] = 1.0
    toep = jnp.einsum("qcd,hrd->hrqc", onehot, rpb, precision=lax.Precision.HIGHEST)
    toep = jnp.where(jnp.asarray(inside), toep, NEG)
    return jnp.stack([jnp.concatenate([toep[:, i - j + NA_KH - 1] for i in range(kh)], axis=-1)
                      for j in range(NA_KH)], axis=1)


def na_mix(p, rpb):
    R = p.shape[0]
    T = R - CTX_LEN
    rows = T // GRID_W
    qb, kb, vb = [(AB_NA + i * MIX_HALF) // NA_HEAD_DIM for i in range(3)]
    qrows = NA_RB * GRID_W
    bias = _na_bias(rpb, rows)
    strip = lambda b: pl.BlockSpec((R, NA_HEAD_DIM), lambda h, r: (0, b + h))
    lat = pl.pallas_call(
        functools.partial(_na_kernel, rows=rows),
        grid=(NA_HEADS, rows // NA_RB),
        in_specs=[strip(qb), strip(kb), strip(vb),
                  pl.BlockSpec((1, NA_KH, GRID_W, NA_WIN), lambda h, r: (h, 0, 0, 0))],
        out_specs=pl.BlockSpec((qrows, NA_HEAD_DIM), lambda h, r: (r, h)),
        out_shape=jax.ShapeDtypeStruct((T, MIX_HALF), F32),
        compiler_params=_params(("parallel", "arbitrary")),
        name="na_latent",
    )(p, p, p, bias)
    blk = lambda b: pl.BlockSpec((CTX_LEN, NA_HEAD_DIM), lambda h: (0, b + h))
    ctx = pl.pallas_call(
        _na_ctx_kernel,
        grid=(NA_HEADS,),
        in_specs=[blk(qb), blk(kb), blk(vb)],
        out_specs=pl.BlockSpec((CTX_LEN, NA_HEAD_DIM), lambda h: (0, h)),
        out_shape=jax.ShapeDtypeStruct((CTX_LEN, MIX_HALF), F32),
        compiler_params=_params(("parallel",)),
        name="na_context",
    )(p, p, p)
    return jnp.concatenate([ctx, lat], axis=0)


ML_CHUNK = CTX_LEN
CD_GT = 2 * ML_HEADS * ML_DQK + 2 * MIX_HALF
CD_HY = CD_GT + LANE


def _mlstm_kernel(q_ref, k_ref, v_ref, lir_ref, lfr_ref, lic_ref, lfc_ref, o_ref, c_ref, n_ref, m_ref):
    L = ML_CHUNK

    @pl.when(pl.program_id(2) == 0)
    def _():
        c_ref[...] = jnp.zeros_like(c_ref)
        n_ref[...] = jnp.zeros_like(n_ref)
        m_ref[...] = jnp.zeros_like(m_ref)

    sgn = 1 - 2 * pl.program_id(0)
    row = lax.broadcasted_iota(jnp.int32, (L, L), 0)
    col = lax.broadcasted_iota(jnp.int32, (L, L), 1)
    seen = (row - col) * sgn >= 0
    seen_t = (col - row) * sgn >= 0
    q = q_ref[...]
    k = k_ref[...]
    vb = v_ref[...].astype(BF16)
    li_r, lf_r, li_c, lf_c = lir_ref[0], lfr_ref[0], lic_ref[0], lfc_ref[0]
    b_c = jnp.sum(jnp.where(seen, lf_r, 0.0), axis=1, keepdims=True)
    b_r = jnp.sum(jnp.where(seen_t, lf_c, 0.0), axis=0, keepdims=True)
    m_prev = m_ref[...]
    dmat = jnp.where(seen, b_c - b_r + li_r, NEG)
    inter = b_c + m_prev
    m_t = jnp.maximum(jnp.max(dmat, axis=1, keepdims=True), inter)
    qb = q.astype(BF16)
    s = lax.dot_general(qb, k.astype(BF16), (((1,), (1,)), ((), ())), preferred_element_type=F32)
    s = s * jnp.exp(dmat - m_t)
    dec = jnp.exp(inter - m_t)
    num = jnp.dot(s.astype(BF16), vb, preferred_element_type=F32)
    num += dec * jnp.dot(qb, c_ref[...].astype(BF16), preferred_element_type=F32)
    den = jnp.sum(s, axis=1, keepdims=True) + dec * jnp.sum(q * n_ref[...], axis=1, keepdims=True)
    o_ref[0] = num / jnp.maximum(jnp.abs(den), jnp.exp(-m_t))
    total = jnp.sum(lf_r, axis=1, keepdims=True)
    gl = total - b_c + li_c
    m_new = jnp.maximum(total + m_prev, jnp.max(gl, axis=0, keepdims=True))
    kw = k * jnp.exp(gl - m_new)
    sc = jnp.exp(total + m_prev - m_new)
    c_ref[...] = sc * c_ref[...] + lax.dot_general(kw.astype(BF16), vb, (((0,), (0,)), ((), ())),
                                                   preferred_element_type=F32)
    n_ref[...] = sc * n_ref[...] + jnp.sum(kw, axis=0, keepdims=True)
    m_ref[...] = m_new


def _axial_rope(x):
    T, d = x.shape[0], x.shape[-1]
    half = d // 2
    nf = half // 2
    t = jnp.arange(T)
    row = (t // GRID_W).astype(F32)
    col = (t % GRID_W).astype(F32)
    inv = ROPE_BASE ** (-jnp.arange(nf, dtype=F32) / nf)
    ang = jnp.concatenate([row[:, None] * inv, col[:, None] * inv], axis=-1)[:, None, :]
    cos, sin = jnp.cos(ang), jnp.sin(ang)
    x1, x2 = x[..., :half], x[..., half:]
    return jnp.concatenate([x1 * cos - x2 * sin, x2 * cos + x1 * sin], axis=-1)


def mlstm_mix(p, gate_b, norm_w):
    R = p.shape[0]
    L = ML_CHUNK
    nc = R // L
    QK = ML_HEADS * ML_DQK

    def rope_lat(t):
        th = t.reshape(R, ML_HEADS, ML_DQK)
        return jnp.concatenate([th[:CTX_LEN], _axial_rope(th[CTX_LEN:])], axis=0).reshape(R, QK)

    q = rope_lat(p[:, :QK])
    k = rope_lat(p[:, QK:2 * QK]) * ML_DQK ** -0.5
    o = p[:, 2 * QK + MIX_HALF:CD_GT]
    gt = p[:, CD_GT:CD_GT + 4 * ML_HEADS].reshape(R, 2, 2, ML_HEADS) + gate_b
    log_i = jnp.transpose(gt[:, :, 0], (1, 2, 0))
    log_f = jax.nn.log_sigmoid(jnp.transpose(gt[:, :, 1], (1, 2, 0)))
    as_rows = lambda t: t.reshape(2 * ML_HEADS * nc, 1, L)
    as_cols = lambda t: t.reshape(2 * ML_HEADS * nc, L, 1)

    def chunk(d, c):
        return jnp.where(d == 0, c, jnp.where(c == 0, 0, nc - c))

    gidx = lambda d, h, c: ((d * ML_HEADS + h) * nc + chunk(d, c), 0, 0)
    vb = 2 * QK // ML_DV
    h_dir = pl.pallas_call(
        _mlstm_kernel,
        grid=(2, ML_HEADS, nc),
        in_specs=[pl.BlockSpec((L, ML_DQK), lambda d, h, c: (chunk(d, c), h)),
                  pl.BlockSpec((L, ML_DQK), lambda d, h, c: (chunk(d, c), h)),
                  pl.BlockSpec((L, ML_DV), lambda d, h, c: (chunk(d, c), vb + h)),
                  pl.BlockSpec((1, 1, L), gidx), pl.BlockSpec((1, 1, L), gidx),
                  pl.BlockSpec((1, L, 1), gidx), pl.BlockSpec((1, L, 1), gidx)],
        out_specs=pl.BlockSpec((1, L, ML_DV), lambda d, h, c: (d, chunk(d, c), h)),
        out_shape=jax.ShapeDtypeStruct((2, R, MIX_HALF), F32),
        scratch_shapes=[pltpu.VMEM((ML_DQK, ML_DV), F32), pltpu.VMEM((1, ML_DQK), F32),
                        pltpu.VMEM((1, 1), F32)],
        compiler_params=_params(("parallel", "parallel", "arbitrary")),
        name="mlstm",
    )(q, k, p, as_rows(log_i), as_rows(log_f), as_cols(log_i), as_cols(log_f))
    h = (h_dir[0] + h_dir[1]).reshape(R, ML_HEADS, ML_DV)
    h = h * lax.rsqrt(jnp.mean(h * h, axis=-1, keepdims=True) + RMS_EPS)
    return h.reshape(R, MIX_HALF) * norm_w * jax.nn.sigmoid(o)


def _pad_to(x, axis, mult):
    n = -x.shape[axis] % mult
    if n == 0:
        return x
    pad = [(0, 0)] * x.ndim
    pad[axis] = (0, n)
    return jnp.pad(x, pad)


def mm3p(a, b):
    M, N = a.shape[0], b.shape[1]
    a = _pad_to(_pad_to(a, 1, LANE), 0, 8)
    b = _pad_to(_pad_to(b, 0, LANE), 1, LANE)
    return mm3(a, b)[:M, :N]


def hyena_filters(T, ff1, ff1_b, ff2, ff2_b, ff3, ff3_b, ff_out, freq, decay):
    t = jnp.linspace(0.0, 1.0, T, dtype=F32)[:, None]
    bands = (HY_EMB - 1) // 2
    f = jnp.linspace(1e-4, bands - 1, bands, dtype=F32)[None, :]
    w = (2 * math.pi / T) * jnp.arange(T, dtype=F32)[:, None]
    z = jnp.concatenate([t, jnp.cos(f * w), -jnp.sin(f * w)], axis=-1)
    h = jnp.sin(freq * (mm3p(z, ff1) + ff1_b))
    h = jnp.sin(freq * (mm3p(h, ff2) + ff2_b))
    h = jnp.sin(freq * (mm3p(h, ff3) + ff3_b))
    h = mm3p(h, ff_out) * jnp.exp(-t * jnp.abs(decay))
    return h.reshape(T, HY_ORDER, 2, MIX_HALF)


FFT_N2 = LANE


HY_Q = 4
HY_F1Q = FFT_N2 // HY_Q
HY_F2H = FFT_N2 // 2
HY_QR = HY_F1Q * HY_F2H
HY_NN = FFT_N2 * FFT_N2


@functools.lru_cache(maxsize=None)
def _hy_tables():
    n = FFT_N2
    i = np.arange(n)
    w = np.exp(-2j * np.pi * np.outer(i, i) / n)
    tw = np.exp(-2j * np.pi * np.outer(i, i) / (n * n))
    ca = np.conj(w)[: n // 2] / (n * n)
    f32 = lambda x: np.ascontiguousarray(x, dtype=np.float32)
    return dict(
        wa=f32(np.concatenate([w.real, w.imag], axis=0)),
        wbr=f32(w.real), wbi=f32(w.imag),
        wbc=f32(2 * np.block([[w.real[:, : n // 2], w.imag[:, : n // 2]],
                              [-w.imag[:, : n // 2], w.real[:, : n // 2]]])),
        twr=f32(tw.real).reshape(n, 1, n), twi=f32(tw.imag).reshape(n, 1, n),
        ctr=f32(tw.real.T).reshape(n, 1, n), cti=f32(-tw.imag.T).reshape(n, 1, n),
        car=f32(ca.real), cai=f32(ca.imag))


def _dot3(a_hi, a_lo, b):
    b_hi, b_lo = _split(b)
    acc = jnp.dot(a_hi, b_hi, preferred_element_type=F32)
    acc += jnp.dot(a_hi, b_lo, preferred_element_type=F32)
    return acc + jnp.dot(a_lo, b_hi, preferred_element_type=F32)


def _hy_fwd_kernel(x_ref, wa_ref, wbr_ref, wbi_ref, twr_ref, twi_ref, o_ref, e_ref, a_ref, *, k1):
    n = FFT_N2
    h = HY_F2H
    q = pl.program_id(1)

    @pl.when(q == 0)
    def _():
        wa_hi, wa_lo = _split(wa_ref[...])
        for t2 in range(n):
            y = _dot3(wa_hi, wa_lo, x_ref[pl.ds(t2, k1, stride=n), :])
            a_ref[pl.ds(t2, n, stride=n), :] = y[:n]
            a_ref[pl.ds(HY_NN + t2, n, stride=n), :] = y[n:]
        col = a_ref[pl.ds(0, n), :]
        sign = (1 - 2 * (lax.broadcasted_iota(jnp.int32, (n, 1), 0) % 2)).astype(F32)
        dc = jnp.sum(col, axis=0, keepdims=True)
        nyq = jnp.sum(col * sign, axis=0, keepdims=True)
        r = lax.broadcasted_iota(jnp.int32, e_ref.shape, 0)
        e_ref[...] = jnp.where(r == 0, dc, jnp.where(r == 1, nyq, 0.0))

    cr, ci = wbr_ref[pl.ds(0, h), :], wbi_ref[pl.ds(0, h), :]

    def body(j, carry):
        f1 = q * HY_F1Q + j
        base = pl.multiple_of(f1 * n, n)
        rhs = jnp.concatenate([a_ref[pl.ds(base, n), :], a_ref[pl.ds(HY_NN + base, n), :]], axis=0)
        twr, twi = twr_ref[f1], twi_ref[f1]
        pr = cr * twr - ci * twi
        pi = cr * twi + ci * twr
        lhs = jnp.concatenate([jnp.concatenate([pr, -pi], axis=1), jnp.concatenate([pi, pr], axis=1)], axis=0)
        out = _dot3(*_split(lhs), rhs)
        ob = pl.multiple_of(j * h, h)
        o_ref[pl.ds(ob, h), :] = out[:h]
        o_ref[pl.ds(HY_QR + ob, h), :] = out[h:]
        return carry

    lax.fori_loop(0, HY_F1Q, body, 0, unroll=4)


def hy_fwd(x):
    T_in, C = x.shape
    k1 = T_in // FFT_N2
    tb = _hy_tables()
    const = lambda a: pl.BlockSpec(a.shape, lambda c, q: (0,) * a.ndim)
    wa = tb["wa"][:, :k1]
    args = (wa, tb["wbr"], tb["wbi"], tb["twr"], tb["twi"])
    return pl.pallas_call(
        functools.partial(_hy_fwd_kernel, k1=k1),
        grid=(C // LANE, HY_Q),
        in_specs=[pl.BlockSpec((T_in, LANE), lambda c, q: (0, c))] + [const(a) for a in args],
        out_specs=[pl.BlockSpec((None, 2 * HY_QR, LANE), lambda c, q: (q, 0, c)),
                   pl.BlockSpec((8, LANE), lambda c, q: (0, c))],
        out_shape=[jax.ShapeDtypeStruct((HY_Q, 2 * HY_QR, C), F32), jax.ShapeDtypeStruct((8, C), F32)],
        scratch_shapes=[pltpu.VMEM((2 * HY_NN, LANE), F32)],
        compiler_params=_params(("parallel", "arbitrary")),
        name="hy_fwd",
    )(x, *args)


def _hy_conv_kernel(z_ref, ze_ref, f_ref, fe_ref, wbc_ref, ctr_ref, cti_ref, car_ref, cai_ref, y_ref, g_ref):
    n = FFT_N2
    h = HY_F2H
    q = pl.program_id(1)
    l_hi, l_lo = _split(wbc_ref[...])

    def body(j, carry):
        ob = pl.multiple_of(j * h, h)
        zr, zi = z_ref[pl.ds(ob, h), :], z_ref[pl.ds(HY_QR + ob, h), :]
        fr, fi = f_ref[pl.ds(ob, h), :], f_ref[pl.ds(HY_QR + ob, h), :]
        prod = jnp.concatenate([zr * fr - zi * fi, zr * fi + zi * fr], axis=0)
        out = _dot3(l_hi, l_lo, prod)
        base = pl.multiple_of((q * HY_F1Q + j) * n, n)
        g_ref[pl.ds(base, n), :] = out[:n]
        g_ref[pl.ds(HY_NN + base, n), :] = out[n:]
        return carry

    lax.fori_loop(0, HY_F1Q, body, 0, unroll=4)

    @pl.when(q == HY_Q - 1)
    def _():
        car, cai = car_ref[...], cai_ref[...]
        y_dc = ze_ref[0:1, :] * fe_ref[0:1, :] * (-1.0 / HY_NN)
        y_nyq = ze_ref[1:2, :] * fe_ref[1:2, :] * (1.0 / HY_NN)
        fix = (y_dc + y_nyq, y_dc - y_nyq)
        for t2 in range(n):
            ctr, cti = ctr_ref[t2], cti_ref[t2]
            er = car * ctr - cai * cti
            ei = car * cti + cai * ctr
            rhs = jnp.concatenate([g_ref[pl.ds(t2, n, stride=n), :],
                                   g_ref[pl.ds(HY_NN + t2, n, stride=n), :]], axis=0)
            y_ref[pl.ds(t2, n // 2, stride=n), :] = (
                _dot3(*_split(jnp.concatenate([er, -ei], axis=1)), rhs) + fix[t2 % 2])


def hy_conv(z, f, f_block=0):
    (zs, ze), (fs, fe) = z, f
    C = zs.shape[-1]
    T = HY_NN // 2
    tb = _hy_tables()
    const = lambda a: pl.BlockSpec(a.shape, lambda c, q: (0,) * a.ndim)
    spec = pl.BlockSpec((None, 2 * HY_QR, LANE), lambda c, q: (q, 0, c))
    fspec = pl.BlockSpec((None, 2 * HY_QR, LANE), lambda c, q: (q, 0, c + f_block))
    espec = pl.BlockSpec((8, LANE), lambda c, q: (0, c))
    fespec = pl.BlockSpec((8, LANE), lambda c, q: (0, c + f_block))
    args = (tb["wbc"], tb["ctr"], tb["cti"], tb["car"], tb["cai"])
    return pl.pallas_call(
        _hy_conv_kernel,
        grid=(C // LANE, HY_Q),
        in_specs=[spec, espec, fspec, fespec] + [const(a) for a in args],
        out_specs=pl.BlockSpec((T, LANE), lambda c, q: (0, c)),
        out_shape=jax.ShapeDtypeStruct((T, C), F32),
        scratch_shapes=[pltpu.VMEM((2 * HY_NN, LANE), F32)],
        compiler_params=_params(("parallel", "arbitrary")),
        name="hy_conv",
    )(zs, ze, fs, fe, *args)


def hyena_filter_taps(T, ff1, ff1_b, ff2, ff2_b, ff3, ff3_b, ff_out, freq, decay):
    t = jnp.linspace(0.0, 1.0, T, dtype=F32)[:, None]
    bands = (HY_EMB - 1) // 2
    f = jnp.linspace(1e-4, bands - 1, bands, dtype=F32)[None, :]
    w = (2 * math.pi / T) * jnp.arange(T, dtype=F32)[:, None]
    z = jnp.concatenate([t, jnp.cos(f * w), -jnp.sin(f * w)], axis=-1)
    h = jnp.sin(freq * (mm3p(z, ff1) + ff1_b))
    h = jnp.sin(freq * (mm3p(h, ff2) + ff2_b))
    h = jnp.sin(freq * (mm3p(h, ff3) + ff3_b))
    w_out = ff_out.reshape(-1, HY_ORDER, 2, MIX_HALF)
    dec = jnp.abs(decay).reshape(HY_ORDER, 2, MIX_HALF)
    side = lambda s: (w_out[:, :, s].reshape(-1, HY_ORDER * MIX_HALF), dec[:, s].reshape(HY_ORDER * MIX_HALF))
    (w_f, d_f), (w_b, d_b) = side(0), side(1)
    zero = jnp.zeros((1, h.shape[1]), F32)
    h_b = jnp.concatenate([zero, h[:0:-1]], axis=0)
    lhs = jnp.concatenate([jnp.concatenate([h, jnp.zeros_like(h)], axis=1),
                           jnp.concatenate([jnp.zeros_like(h), h_b], axis=1)], axis=0)
    t_b = jnp.concatenate([jnp.zeros((1, 1), F32), t[:0:-1]], axis=0)
    first = jnp.arange(2 * T)[:, None] < T
    scale = jnp.exp(-jnp.concatenate([t, t_b], axis=0) * jnp.where(first, d_f, d_b))
    return mm3p(lhs, jnp.concatenate([w_f, w_b], axis=0)) * scale


@functools.lru_cache(maxsize=None)
def _dft_small(n):
    i = np.arange(n)
    w = np.exp(-2j * np.pi * np.outer(i, i) / n)
    fwd = np.concatenate([w.real, w.imag], axis=0)
    inv = np.concatenate([w.real, w.imag], axis=1)[: n // 2] / n
    return np.asarray(fwd, np.float32), np.asarray(inv, np.float32)


def _short_conv(z, h_fwd, h_bwd):
    T, C = z.shape
    n = 2 * T
    fwd, inv = _dft_small(n)
    f = jnp.concatenate([h_fwd, jnp.zeros((1, C), F32), h_bwd[:0:-1]], axis=0)
    zs = mm3(fwd[:, :T], z)
    fs = mm3(fwd, f)
    prod = jnp.concatenate([zs[:n] * fs[:n] - zs[n:] * fs[n:], zs[:n] * fs[n:] + zs[n:] * fs[:n]], axis=0)
    return mm3(inv, prod)


def hyena_seq(p, short_w, short_b, conv, bias):
    u = short_w[0] * _shift_prev(p) + short_w[1] * p + short_w[2] * _shift_next(p) + short_b
    z, x1, x2 = jnp.split(u, 3, axis=-1)
    for n, gate in enumerate((x1, x2)):
        z = gate * (conv(z, n) + bias[n] * z)
    return z


def hyena_mix(p, short_w, short_b, filt_params, bias):
    T = p.shape[0] - CTX_LEN
    filt_c = hyena_filters(CTX_LEN, *filt_params)
    out_c = hyena_seq(p[:CTX_LEN], short_w, short_b,
                      lambda z, n: _short_conv(z, filt_c[:, n, 0], filt_c[:, n, 1]), bias)
    assert 2 * T == HY_NN, "the latent convolution kernels are built for 8192 latent tokens"
    spectra = hy_fwd(hyena_filter_taps(T, *filt_params))
    out = hyena_seq(p[CTX_LEN:], short_w, short_b,
                    lambda z, n: hy_conv(hy_fwd(z), spectra, n * MIX_HALF // LANE), bias)
    return jnp.concatenate([out_c, out], axis=0)


def kernel(x, c, ctx, c_ctx, ada_down, ada_up, ada_bias, norm_g, ffn_w1, ffn_w3, ffn_w2, ab_w_in, ab_w_out, rwkv_mu, rwkv_w0, rwkv_w2, rwkv_a0, rwkv_a2, rwkv_g2, rwkv_kk, rwkv_ka, rwkv_rk, rwkv_gn, na_rpb, cd_w_in, cd_w_out, ml_gate_b, ml_norm, hy_short_w, hy_short_b, hy_ff1, hy_ff1_b, hy_ff2, hy_ff2_b, hy_ff3, hy_ff3_b, hy_ff_out, hy_freq, hy_decay, hy_bias):
    assert x.shape[0] == 1 and ctx.shape[1] == CTX_LEN
    X = jnp.concatenate([ctx[0], x[0]], axis=0)
    s2 = _pad_to(jnp.stack([jax.nn.silu(c_ctx), jax.nn.silu(c[0])]), 0, 16)
    n = jnp.arange(MIX_HALF)
    perm = ((n % RWKV_HEADS) * RWKV_HEAD + n // RWKV_HEADS == n[:, None]).astype(BF16)
    mods = [(mm(mm(s2, ada_down[l]), ada_up[l])[:2] + ada_bias[l]).reshape(2, N_SUB, 3, D_MODEL)
            for l in range(DEPTH)]
    hn = norm_mod(X, norm_g[0, 0], mods[0][:, 0, 0], mods[0][:, 0, 1])
    for l in range(DEPTH):
        i = l // 2
        mod = mods[l]
        g = norm_g[l]
        y = mm_w32(swiglu(hn, ffn_w1, ffn_w3, (l, 0)), ffn_w2, (l, 0))
        X, xn = post_res_norm(X, y, g[1], mod[:, 0, 2], FFN_RES, g[2], mod[:, 1, 0], mod[:, 1, 1])
        if l % 2 == 0:
            w_ab = ab_w_in[i].astype(BF16)
            w_rkv = [mm(t, perm, BF16) for t in jnp.split(w_ab[:, :AB_WD], 3, axis=-1)]
            p = mm(xn, _pack_cols(jnp.concatenate(w_rkv + [w_ab[:, AB_WD:]], axis=-1), _AB_CUTS, _AB_WIDTHS))
            ya = rwkv_mix(p, rwkv_mu[i], rwkv_w0[i], rwkv_w2[i], rwkv_a0[i], rwkv_a2[i],
                          rwkv_g2[i], rwkv_kk[i], rwkv_ka[i], rwkv_rk[i], rwkv_gn[i])
            y = jnp.concatenate([ya, na_mix(p, na_rpb[i])], axis=-1)
            w_out = ab_w_out[i].astype(BF16)
            w_rw = mm(perm.T, w_out[:MIX_HALF], BF16)
            yo = mm(y, jnp.concatenate([w_rw, w_out[MIX_HALF:]], axis=0))
        else:
            w_in = _pack_cols(cd_w_in[i].astype(BF16), (CD_GT, ML_COLS), (CD_GT, LANE, P_COLS - CD_HY))
            p = mm(xn, w_in)
            filt_params = (hy_ff1[i], hy_ff1_b[i], hy_ff2[i], hy_ff2_b[i], hy_ff3[i], hy_ff3_b[i],
                           hy_ff_out[i], hy_freq[i], hy_decay[i])
            yd = hyena_mix(p[:, CD_HY:CD_HY + 3 * MIX_HALF], hy_short_w[i], hy_short_b[i], filt_params,
                           hy_bias[i])
            y = jnp.concatenate([mlstm_mix(p, ml_gate_b[i], ml_norm[i]), yd], axis=-1)
            yo = mm_w32(y.astype(BF16), cd_w_out, (i,))
        X, hn = post_res_norm(X, yo, g[3], mod[:, 1, 2], 1.0, g[4], mod[:, 2, 0], mod[:, 2, 1])
        y = mm_w32(swiglu(hn, ffn_w1, ffn_w3, (l, 1)), ffn_w2, (l, 1))
        if l + 1 < DEPTH:
            nxt = mods[l + 1]
            X, hn = post_res_norm(X, y, g[5], mod[:, 2, 2], FFN_RES, norm_g[l + 1, 0], nxt[:, 0, 0], nxt[:, 0, 1])
        else:
            X = post_res(X, y, g[5], mod[:, 2, 2], FFN_RES)
    return X[CTX_LEN:][None]
```
